```python
import jax
import jax.numpy as jnp
from jax import lax
import numpy as np

D_MODEL = 2048
BATCH = 2
SEQ = 4096
DEPTH = 2

POOL_WINDOWS = (2, 4, 8, 16)
POOL_GROUP = 128
POOL_WIDTH = 512
ATT_GROUPS = ((128, 1), (512, 4), (2048, 16))
ATT_HEADS_PER_GROUP = 4
ATT_HEAD_DIM = 64
ATT_HEADS = 12
ATT_WIDTH = 768
ATT_OUT_WIDTH = 256
ATT_BLOCK = 128
ALIBI_SLOPES = tuple(2.0 ** (-8.0 * (h + 1) / ATT_HEADS) for h in range(ATT_HEADS))
GLA_HEADS = 4
GLA_DK = 64
GLA_DV = 128
GLA_KEY_WIDTH = 256
GLA_VAL_WIDTH = 512
GLA_GATE_RANK = 16
GLA_GATE_TEMP = 16.0
GLA_CHUNK = 32
GLA_NORM_EPS = 1e-6
RWKV_HEADS = 8
RWKV_HEAD_DIM = 64
RWKV_WIDTH = 512
RWKV_DECAY_RANK = 32
RWKV_AAA_RANK = 32
RWKV_GATE_RANK = 96
RWKV_SPLITS = (RWKV_WIDTH, RWKV_WIDTH, RWKV_WIDTH, RWKV_DECAY_RANK, RWKV_AAA_RANK, RWKV_GATE_RANK)
RWKV_SHIFT_WIDTH = sum(RWKV_SPLITS)
RWKV_GN_EPS = 64e-5
N_BRANCHES = 4
IN_SPLITS = (POOL_WIDTH, ATT_WIDTH, ATT_WIDTH, ATT_WIDTH, GLA_KEY_WIDTH, GLA_KEY_WIDTH,
             GLA_VAL_WIDTH, GLA_VAL_WIDTH, GLA_GATE_RANK, RWKV_SHIFT_WIDTH, N_BRANCHES * D_MODEL)
D_IN = sum(IN_SPLITS)
N_EXPERTS = 32
TOP_K = 4
EXPERT_FF = 1024
SWIGLU_LIMIT = 7.0
SWIGLU_ALPHA = 1.702
MOE_BLOCK = 128
LN_EPS = 1e-5
DEEPNORM_ALPHA = (2 * DEPTH) ** 0.25
DEEPNORM_BETA = (8 * DEPTH) ** -0.25

kernel_name = 'hybrid_pool_dilated_gla_rwkv7_moe_deepnorm'


def _split(t, sizes):
    return jnp.split(t, np.cumsum(sizes)[:-1].tolist(), axis=-1)


def layer_norm(x, g, b):
    xf = x.astype(jnp.float32)
    mu = jnp.mean(xf, -1, keepdims=True)
    var = jnp.mean(jnp.square(xf - mu), -1, keepdims=True)
    return ((xf - mu) * lax.rsqrt(var + LN_EPS)).astype(x.dtype) * g + b


def pool_mixer(v, pool_w, pool_scale):
    B, S, _ = v.shape
    vg = v.reshape(B, S, len(POOL_WINDOWS), POOL_GROUP)
    cs = jnp.cumsum(vg.astype(jnp.float32), axis=1)
    cs = jnp.concatenate([jnp.zeros_like(cs[:, :1]), cs], axis=1)
    t = jnp.arange(S)
    means = []
    for g, w in enumerate(POOL_WINDOWS):
        csg = cs[:, :, g]
        lo = jnp.maximum(t + 1 - w, 0)
        cnt = (t + 1 - lo).astype(jnp.float32)
        means.append((csg[:, 1:] - csg[:, lo]) / cnt[:, None])
    pooled = jnp.stack(means, axis=2).astype(v.dtype)
    y = jnp.einsum('bsgc,gcd->bsgd', pooled - vg, pool_w)
    return y.reshape(B, S, POOL_WIDTH) * pool_scale


def dilated_group_attention(q, k, v, window, dilation, slopes):
    B, S, H, Dh = q.shape
    L = S // dilation
    nb = -(-L // ATT_BLOCK)
    Lp = nb * ATT_BLOCK
    span = window // dilation

    def to_blocks(t):
        t = t.reshape(B, L, dilation, H, Dh).transpose(0, 2, 1, 3, 4)
        t = jnp.pad(t, ((0, 0), (0, 0), (0, Lp - L), (0, 0), (0, 0)))
        return t.reshape(B, dilation, nb, ATT_BLOCK, H, Dh)

    def with_prev(t):
        prev = jnp.pad(t[:, :, :-1], ((0, 0), (0, 0), (1, 0), (0, 0), (0, 0), (0, 0)))
        return jnp.concatenate([prev, t], axis=3)

    def from_blocks(t):
        t = t.reshape(B, dilation, Lp, *t.shape[4:])[:, :, :L]
        return jnp.swapaxes(t, 1, 2).reshape(B, S, *t.shape[3:])

    f32 = jnp.float32
    qb = to_blocks(q).astype(f32)
    kw = with_prev(to_blocks(k)).astype(f32)
    vw = with_prev(to_blocks(v)).astype(f32)
    scores = jnp.einsum('brnqhc,brnkhc->brnhqk', qb, kw) * (Dh ** -0.5)
    qi = jnp.arange(ATT_BLOCK)[:, None] + ATT_BLOCK
    kj = jnp.arange(2 * ATT_BLOCK)[None, :]
    dist = qi - kj
    blk = jnp.arange(nb)[:, None, None]
    valid = (dist >= 0) & (dist <= span) & ((blk > 0) | (kj >= ATT_BLOCK))
    alibi = -jnp.asarray(slopes, f32)[:, None, None] * (dist * dilation).astype(f32)
    scores = jnp.where(valid[:, None], scores + alibi, -jnp.inf)
    lse = jax.nn.logsumexp(scores, axis=-1)
    p = jnp.exp(scores - lse[..., None])
    out = jnp.einsum('brnhqk,brnkhc->brnqhc', p, vw)
    return from_blocks(out), from_blocks(jnp.swapaxes(lse, 3, 4))


def dilated_attention(q, k, v):
    B, S, _ = q.shape
    shp = (B, S, len(ATT_GROUPS), ATT_HEADS_PER_GROUP, ATT_HEAD_DIM)
    q, k, v = q.reshape(shp), k.reshape(shp), v.reshape(shp)
    outs, lses = [], []
    for g, (window, dilation) in enumerate(ATT_GROUPS):
        slopes = ALIBI_SLOPES[g * ATT_HEADS_PER_GROUP:(g + 1) * ATT_HEADS_PER_GROUP]
        o, s = dilated_group_attention(q[:, :, g], k[:, :, g], v[:, :, g], window, dilation, slopes)
        outs.append(o)
        lses.append(s)
    wts = jax.nn.softmax(jnp.stack(lses, 0), axis=0)
    out = jnp.einsum('gbsh,gbshc->bshc', wts, jnp.stack(outs, 0))
    return out.reshape(B, S, ATT_OUT_WIDTH).astype(q.dtype)


def gla_chunked(q, k, v, log_a):
    B, H, S, dk = q.shape
    dv = v.shape[-1]
    n = S // GLA_CHUNK
    q, k, log_a = (t.reshape(B, H, n, GLA_CHUNK, dk) for t in (q, k, log_a))
    v = v.reshape(B, H, n, GLA_CHUNK, dv)
    b = jnp.cumsum(log_a, axis=3)
    causal = jnp.tril(jnp.ones((GLA_CHUNK, GLA_CHUNK), bool))
    rel = jnp.where(causal[:, :, None], b[..., :, None, :] - b[..., None, :, :], -jnp.inf)
    scores = jnp.einsum('bhnic,bhnjc,bhnijc->bhnij', q, k, jnp.exp(rel))
    o = jnp.einsum('bhnij,bhnjv->bhniv', scores, v)
    b_last = b[..., -1, :]
    chunk_kv = jnp.einsum('bhnjc,bhnjv->bhncv', k * jnp.exp(b_last[..., None, :] - b), v)

    def step(state, inp):
        g_last, kv = inp
        return state * jnp.exp(g_last)[..., None] + kv, state

    _, s_prev = lax.scan(step, jnp.zeros((B, H, dk, dv), jnp.float32),
                         (jnp.moveaxis(b_last, 2, 0), jnp.moveaxis(chunk_kv, 2, 0)))
    o = o + jnp.einsum('bhnic,bhncv->bhniv', q * jnp.exp(b), jnp.moveaxis(s_prev, 0, 2))
    return o.reshape(B, H, S, dv)


def gla_mixer(q, k, v, g, a_lo, w_alpha, b_alpha, norm_g):
    B, S, _ = q.shape
    f32 = jnp.float32
    log_a = jax.nn.log_sigmoid((a_lo @ w_alpha + b_alpha).astype(f32)) / GLA_GATE_TEMP

    def heads(t, d):
        return t.reshape(B, S, GLA_HEADS, d).transpose(0, 2, 1, 3).astype(f32)

    o = gla_chunked(heads(q, GLA_DK) * (GLA_DK ** -0.5), heads(k, GLA_DK),
                    heads(v, GLA_DV), heads(log_a, GLA_DK))
    o = o * lax.rsqrt(jnp.mean(jnp.square(o), -1, keepdims=True) + GLA_NORM_EPS)
    o = o.transpose(0, 2, 1, 3).reshape(B, S, GLA_VAL_WIDTH).astype(q.dtype)
    return o * norm_g * jax.nn.silu(g)


def rwkv7_scan(r, decay, k, v, kk, a):
    B, S, H, N = r.shape

    def step(state, inp):
        r_t, w_t, k_t, v_t, kk_t, a_t = inp
        removed = jnp.einsum('bhvk,bhk->bhv', state, kk_t)
        state = (state * w_t[:, :, None, :]
                 - removed[..., None] * (kk_t * a_t)[:, :, None, :]
                 + v_t[..., None] * k_t[:, :, None, :])
        return state, jnp.einsum('bhvk,bhk->bhv', state, r_t)

    xs = tuple(jnp.moveaxis(t, 1, 0) for t in (r, decay, k, v, kk, a))
    _, y = lax.scan(step, jnp.zeros((B, H, N, N), jnp.float32), xs)
    return jnp.moveaxis(y, 0, 1)


def rwkv7_mixer(p, mu, w0, w2, a0, a2, g2, k_k, k_a, r_k, ln_g, ln_b):
    B, S, _ = p.shape
    f32 = jnp.float32
    p = p + (jnp.pad(p[:, :-1], ((0, 0), (1, 0), (0, 0))) - p) * mu
    r, k, v, w_lo, a_lo, g_lo = _split(p, RWKV_SPLITS)
    log_w = -jax.nn.softplus(-(w0 + jnp.tanh(w_lo) @ w2).astype(f32)) - 0.5
    a = jax.nn.sigmoid((a0 + a_lo @ a2).astype(f32))
    gate = jax.nn.sigmoid(g_lo) @ g2
    hd = (B, S, RWKV_HEADS, RWKV_HEAD_DIM)
    r, k, v = (t.astype(f32).reshape(hd) for t in (r, k, v))
    decay = jnp.exp(-jnp.exp(log_w)).reshape(hd)
    a = a.reshape(hd)
    kk = k * k_k.reshape(RWKV_HEADS, RWKV_HEAD_DIM).astype(f32)
    kk = kk / jnp.maximum(jnp.sqrt(jnp.sum(jnp.square(kk), -1, keepdims=True)), 1e-12)
    k = k * (1.0 + (a - 1.0) * k_a.reshape(RWKV_HEADS, RWKV_HEAD_DIM).astype(f32))
    y = rwkv7_scan(r, decay, k, v, kk, a)
    mean = jnp.mean(y, -1, keepdims=True)
    var = jnp.mean(jnp.square(y - mean), -1, keepdims=True)
    y = ((y - mean) * lax.rsqrt(var + RWKV_GN_EPS)).reshape(B, S, RWKV_WIDTH) * ln_g + ln_b
    bonus = jnp.sum(r * k * r_k.astype(f32), -1, keepdims=True) * v
    return (y + bonus.reshape(B, S, RWKV_WIDTH)).astype(p.dtype) * gate


def hybrid_mixer(h, w_in, pool_w, pool_scale, gla_w_alpha, gla_b_alpha, gla_norm_g,
                 rwkv_mu, rwkv_w0, rwkv_w2, rwkv_a0, rwkv_a2, rwkv_g2, rwkv_k_k, rwkv_k_a,
                 rwkv_r_k, rwkv_ln_g, rwkv_ln_b, w_branch_a, w_branch_b, w_branch_c,
                 w_branch_d, w_out):
    B, S, D = h.shape
    (pool_v, att_q, att_k, att_v, gla_q, gla_k, gla_v, gla_g, gla_a_lo, rwkv_p,
     gate_logits) = _split(h @ w_in, IN_SPLITS)
    y_a = pool_mixer(pool_v, pool_w, pool_scale)
    y_b = dilated_attention(att_q, att_k, att_v)
    y_c = gla_mixer(gla_q, gla_k, gla_v, gla_g, gla_a_lo, gla_w_alpha, gla_b_alpha, gla_norm_g)
    y_d = rwkv7_mixer(rwkv_p, rwkv_mu, rwkv_w0, rwkv_w2, rwkv_a0, rwkv_a2, rwkv_g2,
                      rwkv_k_k, rwkv_k_a, rwkv_r_k, rwkv_ln_g, rwkv_ln_b)
    gates = jax.nn.sigmoid(gate_logits.reshape(B, S, N_BRANCHES, D))
    merged = (gates[:, :, 0] * (y_a @ w_branch_a) + gates[:, :, 1] * (y_b @ w_branch_b)
              + gates[:, :, 2] * (y_c @ w_branch_c) + gates[:, :, 3] * (y_d @ w_branch_d))
    return merged @ w_out


def routed_ffn(h, router_w, router_b, w_gate_up, b_gate_up, w_down, b_down):
    T, D = h.shape
    logits = (h @ router_w + router_b).astype(jnp.float32)
    top_val, top_idx = lax.top_k(logits, TOP_K)
    weights = jax.nn.softmax(top_val, axis=-1).astype(h.dtype)
    n_pairs = T * TOP_K
    flat_e = top_idx.reshape(-1)
    flat_tok = jnp.arange(n_pairs, dtype=jnp.int32) // TOP_K
    flat_w = weights.reshape(-1)
    order = jnp.argsort(flat_e)
    sorted_e = flat_e[order]
    counts = jnp.bincount(flat_e, length=N_EXPERTS)
    start = jnp.cumsum(counts) - counts
    padded = ((counts + MOE_BLOCK - 1) // MOE_BLOCK) * MOE_BLOCK
    pend = jnp.cumsum(padded)
    pstart = pend - padded
    dest = pstart[sorted_e] + (jnp.arange(n_pairs) - start[sorted_e])
    n_blocks = -(-n_pairs // MOE_BLOCK) + N_EXPERTS
    rows = n_blocks * MOE_BLOCK
    slot_tok = jnp.full((rows,), T, jnp.int32).at[dest].set(flat_tok[order])
    slot_w = jnp.zeros((rows,), h.dtype).at[dest].set(flat_w[order])
    block_e = jnp.minimum(jnp.searchsorted(pend, jnp.arange(n_blocks) * MOE_BLOCK, side='right'),
                          N_EXPERTS - 1)
    h_pad = jnp.concatenate([h, jnp.zeros((1, D), h.dtype)], axis=0)
    xb = h_pad[slot_tok].reshape(n_blocks, MOE_BLOCK, D)

    def expert_block(args):
        xblk, e = args
        gu = xblk @ w_gate_up[e] + b_gate_up[e]
        glu = jnp.minimum(gu[:, :EXPERT_FF], SWIGLU_LIMIT)
        lin = jnp.clip(gu[:, EXPERT_FF:], -SWIGLU_LIMIT, SWIGLU_LIMIT)
        act = glu * jax.nn.sigmoid(SWIGLU_ALPHA * glu) * (lin + 1.0)
        return act @ w_down[e] + b_down[e]

    yb = lax.map(expert_block, (xb, block_e))
    y = jnp.zeros((T + 1, D), h.dtype).at[slot_tok].add(yb.reshape(rows, D) * slot_w[:, None])
    return y[:T]


def setup_inputs(seed: int = 0) -> dict:
    key = jax.random.key(seed)
    keys = iter(jax.random.split(key, 64))
    f32 = jnp.float32
    L, D, E, F = DEPTH, D_MODEL, N_EXPERTS, EXPERT_FF

    def normal(shape, scale):
        return jax.random.normal(next(keys), shape, f32) * scale

    def near_one(shape):
        return 1.0 + normal(shape, 0.02)

    return {
        'x': normal((BATCH, SEQ, D), 1.0),
        'c': normal((BATCH, D), 1.0),
        'ada_w': normal((L, D, 6 * D), 0.5 * D ** -0.5),
        'ada_b': normal((L, 6 * D), 0.02),
        'w_in': normal((L, D, D_IN), D ** -0.5),
        'pool_w': normal((L, len(POOL_WINDOWS), POOL_GROUP, POOL_GROUP), POOL_GROUP ** -0.5),
        'pool_scale': near_one((L, POOL_WIDTH)),
        'gla_w_alpha': normal((L, GLA_GATE_RANK, GLA_KEY_WIDTH), GLA_GATE_RANK ** -0.5),
        'gla_b_alpha': normal((L, GLA_KEY_WIDTH), 0.1),
        'gla_norm_g': near_one((L, GLA_VAL_WIDTH)),
        'rwkv_mu': jax.random.uniform(next(keys), (L, RWKV_SHIFT_WIDTH), f32),
        'rwkv_w0': jnp.linspace(-6.0, 1.0, RWKV_WIDTH, dtype=f32) + normal((L, RWKV_WIDTH), 0.1),
        'rwkv_w2': normal((L, RWKV_DECAY_RANK, RWKV_WIDTH), RWKV_DECAY_RANK ** -0.5),
        'rwkv_a0': normal((L, RWKV_WIDTH), 0.1),
        'rwkv_a2': normal((L, RWKV_AAA_RANK, RWKV_WIDTH), RWKV_AAA_RANK ** -0.5),
        'rwkv_g2': normal((L, RWKV_GATE_RANK, RWKV_WIDTH), RWKV_GATE_RANK ** -0.5),
        'rwkv_k_k': 0.85 + normal((L, RWKV_WIDTH), 0.02),
        'rwkv_k_a': near_one((L, RWKV_WIDTH)),
        'rwkv_r_k': normal((L, RWKV_HEADS, RWKV_HEAD_DIM), 0.1),
        'rwkv_ln_g': near_one((L, RWKV_WIDTH)),
        'rwkv_ln_b': normal((L, RWKV_WIDTH), 0.02),
        'w_branch_a': normal((L, POOL_WIDTH, D), POOL_WIDTH ** -0.5),
        'w_branch_b': normal((L, ATT_OUT_WIDTH, D), ATT_OUT_WIDTH ** -0.5),
        'w_branch_c': normal((L, GLA_VAL_WIDTH, D), GLA_VAL_WIDTH ** -0.5),
        'w_branch_d': normal((L, RWKV_WIDTH, D), RWKV_WIDTH ** -0.5),
        'w_out': normal((L, D, D), DEEPNORM_BETA * D ** -0.5),
        'ln1_g': near_one((L, D)),
        'ln1_b': normal((L, D), 0.02),
        'router_w': normal((L, D, E), D ** -0.5),
        'router_b': normal((L, E), 0.01),
        'w_gate_up': normal((L, E, D, 2 * F), D ** -0.5),
        'b_gate_up': normal((L, E, 2 * F), 0.02),
        'w_down': normal((L, E, F, D), DEEPNORM_BETA * F ** -0.5),
        'b_down': normal((L, E, D), 0.02),
        'ln2_g': near_one((L, D)),
        'ln2_b': normal((L, D), 0.02),
    }


def reference(x, c, ada_w, ada_b, w_in, pool_w, pool_scale, gla_w_alpha, gla_b_alpha,
              gla_norm_g, rwkv_mu, rwkv_w0, rwkv_w2, rwkv_a0, rwkv_a2, rwkv_g2, rwkv_k_k,
              rwkv_k_a, rwkv_r_k, rwkv_ln_g, rwkv_ln_b, w_branch_a, w_branch_b, w_branch_c,
              w_branch_d, w_out, ln1_g, ln1_b, router_w, router_b, w_gate_up, b_gate_up,
              w_down, b_down, ln2_g, ln2_b):
    B, S, D = x.shape
    for l in range(DEPTH):
        mod = jax.nn.silu(c) @ ada_w[l] + ada_b[l]
        shift1, scale1, gate1, shift2, scale2, gate2 = jnp.split(mod[:, None, :], 6, axis=-1)
        h = x * (1.0 + scale1) + shift1
        y = hybrid_mixer(h, w_in[l], pool_w[l], pool_scale[l], gla_w_alpha[l], gla_b_alpha[l],
                         gla_norm_g[l], rwkv_mu[l], rwkv_w0[l], rwkv_w2[l], rwkv_a0[l],
                         rwkv_a2[l], rwkv_g2[l], rwkv_k_k[l], rwkv_k_a[l], rwkv_r_k[l],
                         rwkv_ln_g[l], rwkv_ln_b[l], w_branch_a[l], w_branch_b[l],
                         w_branch_c[l], w_branch_d[l], w_out[l])
        x = layer_norm(DEEPNORM_ALPHA * x + gate1 * y, ln1_g[l], ln1_b[l])
        h = x * (1.0 + scale2) + shift2
        y = routed_ffn(h.reshape(B * S, D), router_w[l], router_b[l], w_gate_up[l],
                       b_gate_up[l], w_down[l], b_down[l]).reshape(B, S, D)
        x = layer_norm(DEEPNORM_ALPHA * x + gate2 * y, ln2_g[l], ln2_b[l])
    return x
```

```python
import functools

import numpy as np
import jax
import jax.numpy as jnp
from jax import lax
from jax.experimental import pallas as pl
from jax.experimental.pallas import tpu as pltpu

F32 = jnp.float32
BF16 = jnp.bfloat16
HIGHEST = lax.Precision.HIGHEST

D_MODEL = 2048
DEPTH = 2
POOL_WINDOWS = (2, 4, 8, 16)
POOL_GROUP = 128
POOL_WIDTH = 512
ATT_GROUPS = ((128, 1), (512, 4), (2048, 16))
ATT_HEAD_DIM = 64
ATT_HEADS = 12
ATT_WIDTH = 768
ATT_OUT_WIDTH = 256
ATT_BLOCK = 128
ALIBI_SLOPES = tuple(2.0 ** (-8.0 * (h + 1) / ATT_HEADS) for h in range(ATT_HEADS))
GLA_HEADS = 4
GLA_DK = 64
GLA_DV = 128
GLA_KEY_WIDTH = 256
GLA_VAL_WIDTH = 512
GLA_GATE_RANK = 16
GLA_GATE_TEMP = 16.0
GLA_CHUNK = 32
GLA_NORM_EPS = 1e-6
RWKV_HEADS = 8
RWKV_HEAD_DIM = 64
RWKV_WIDTH = 512
RWKV_DECAY_RANK = 32
RWKV_AAA_RANK = 32
RWKV_GATE_RANK = 96
RWKV_GN_EPS = 64e-5
N_BRANCHES = 4
N_EXPERTS = 32
TOP_K = 4
EXPERT_FF = 1024
SWIGLU_LIMIT = 7.0
SWIGLU_ALPHA = 1.702
LN_EPS = 1e-5
DEEPNORM_ALPHA = (2 * DEPTH) ** 0.25

LANES = 128
SUBLANES = 8
VMEM_LIMIT = 56 * 1024 * 1024

SRC_POOL = (0, 512)
SRC_ATT = (512, 2816)
SRC_GLA_Q = (2816, 3072)
SRC_GLA_K = (3072, 3328)
SRC_GLA_V = (3328, 3840)
SRC_GLA_G = (3840, 4352)
SRC_GLA_A = (4352, 4368)
SRC_RWKV_RKV = (4368, 5904)
SRC_RWKV_LO = (5904, 6064)
SRC_GATES = (6064, 14256)
OFF_POOL = 0
OFF_GLA_V = 512
OFF_GLA_G = 1024
OFF_RWKV_RKV = 1536
OFF_ATT_Q = 3072
OFF_ATT_K = OFF_ATT_Q + ATT_WIDTH
OFF_ATT_V = OFF_ATT_K + ATT_WIDTH
OFF_GLA_Q = 5376
OFF_GLA_K = 5632
OFF_RWKV_LO = 5888
RWKV_LO_PAD = 256
OFF_GLA_A = 6144
GLA_A_PAD = 128
MIX_WIDTH = 6272
MIX_TN = 896

MOE_BM = 256
NEG_BIG = -1e30


def _cparams(sem):
    return pltpu.CompilerParams(dimension_semantics=sem, vmem_limit_bytes=VMEM_LIMIT)


def _ada_kernel(c_ref, w_ref, b_ref, o_ref):
    c = c_ref[...]
    s = c * jax.nn.sigmoid(c)
    o_ref[0] = jnp.dot(s, w_ref[0], precision=HIGHEST, preferred_element_type=F32) + b_ref[0]


def ada_modulation(c, ada_w, ada_b):
    L, D, N = ada_w.shape
    B = c.shape[0]
    cp = jnp.zeros((SUBLANES, D), F32).at[:B].set(c)
    tn = 1024
    out = pl.pallas_call(
        _ada_kernel,
        out_shape=jax.ShapeDtypeStruct((L, SUBLANES, N), F32),
        grid=(L, N // tn),
        in_specs=[pl.BlockSpec((SUBLANES, D), lambda l, j: (0, 0)),
                  pl.BlockSpec((1, D, tn), lambda l, j: (l, 0, j)),
                  pl.BlockSpec((1, 1, tn), lambda l, j: (l, 0, j))],
        out_specs=pl.BlockSpec((1, SUBLANES, tn), lambda l, j: (l, 0, j)),
        compiler_params=_cparams(("parallel", "parallel")),
        name="ada_modulation",
    )(cp, ada_w, ada_b.reshape(L, 1, N))
    return out[:, :B]


def _modulate_kernel(x_ref, sc_ref, sh_ref, o_ref):
    o_ref[0] = (x_ref[0] * (1.0 + sc_ref[0]) + sh_ref[0]).astype(o_ref.dtype)


def modulate(x, scale, shift):
    B, S, D = x.shape
    ts = 1024
    return pl.pallas_call(
        _modulate_kernel,
        out_shape=jax.ShapeDtypeStruct((B, S, D), BF16),
        grid=(B, S // ts),
        in_specs=[pl.BlockSpec((1, ts, D), lambda b, i: (b, i, 0)),
                  pl.BlockSpec((1, 1, D), lambda b, i: (b, 0, 0)),
                  pl.BlockSpec((1, 1, D), lambda b, i: (b, 0, 0))],
        out_specs=pl.BlockSpec((1, ts, D), lambda b, i: (b, i, 0)),
        compiler_params=_cparams(("parallel", "parallel")),
        name="modulate",
    )(x, scale, shift)


def _mm_kernel(x_ref, w_ref, o_ref):
    o_ref[...] = jnp.dot(x_ref[...], w_ref[...], preferred_element_type=F32).astype(o_ref.dtype)


def matmul(x, w, tm, tn, out_dtype=F32):
    M, K = x.shape
    N = w.shape[1]
    return pl.pallas_call(
        _mm_kernel,
        out_shape=jax.ShapeDtypeStruct((M, N), out_dtype),
        grid=(N // tn, M // tm),
        in_specs=[pl.BlockSpec((tm, K), lambda j, i: (i, 0)),
                  pl.BlockSpec((K, tn), lambda j, i: (0, j))],
        out_specs=pl.BlockSpec((tm, tn), lambda j, i: (i, j)),
        compiler_params=_cparams(("parallel", "parallel")),
        name="matmul",
    )(x, w)


def _pool_kernel(p_ref, w_ref, sc_ref, o_ref):
    g = pl.program_id(1)
    v = p_ref[0]
    S = v.shape[0]
    row = lax.broadcasted_iota(jnp.int32, v.shape, 0)
    win = jnp.left_shift(2, g)
    s = v
    pooled_sum = v
    for k, sh in enumerate((1, 2, 4, 8)):
        s = s + jnp.where(row >= sh, pltpu.roll(s, sh, 0), 0.0)
        pooled_sum = jnp.where(g >= k, s, pooled_sum)
    cnt = jnp.minimum(row + 1, win).astype(F32)
    diff = pooled_sum / cnt - v
    y = jnp.dot(diff.astype(BF16), w_ref[0].astype(BF16), preferred_element_type=F32)
    o_ref[0] = y * sc_ref[...]


def pool_mixer(P, pool_w, pool_scale):
    B, S, _ = P.shape
    G = len(POOL_WINDOWS)
    return pl.pallas_call(
        _pool_kernel,
        out_shape=jax.ShapeDtypeStruct((B, S, POOL_WIDTH), F32),
        grid=(B, G),
        in_specs=[pl.BlockSpec((1, S, POOL_GROUP), lambda b, g: (b, 0, OFF_POOL // POOL_GROUP + g)),
                  pl.BlockSpec((1, POOL_GROUP, POOL_GROUP), lambda b, g: (g, 0, 0)),
                  pl.BlockSpec((1, POOL_GROUP), lambda b, g: (0, g))],
        out_specs=pl.BlockSpec((1, S, POOL_GROUP), lambda b, g: (b, 0, g)),
        compiler_params=_cparams(("parallel", "parallel")),
        name="pool_mixer",
    )(P, pool_w, pool_scale.reshape(1, POOL_WIDTH))


def _att_kernel(q_ref, kc_ref, kp_ref, vc_ref, vp_ref, num_ref, den_ref, m_ref, *, slopes, dilation):
    hp = pl.program_id(2)
    n = pl.program_id(3)
    i = lax.broadcasted_iota(jnp.int32, (ATT_BLOCK, ATT_BLOCK), 0)
    j = lax.broadcasted_iota(jnp.int32, (ATT_BLOCK, ATT_BLOCK), 1)
    dist_c = i - j
    dist_p = dist_c + ATT_BLOCK
    valid_c = dist_c >= 0
    valid_p = jnp.logical_and(dist_p <= ATT_BLOCK, n > 0)
    q2, kc2, kp2, vc2, vp2 = q_ref[0], kc_ref[0], kp_ref[0], vc_ref[0], vp_ref[0]
    dn = (((1,), (1,)), ((), ()))
    for h in range(2):
        sl = slice(h * ATT_HEAD_DIM, (h + 1) * ATT_HEAD_DIM)
        slope = jnp.where(hp == 0, slopes[h], slopes[2 + h]) * float(dilation)
        q = (q2[:, sl] * (ATT_HEAD_DIM ** -0.5)).astype(BF16)
        s_c = lax.dot_general(q, kc2[:, sl].astype(BF16), dn, preferred_element_type=F32)
        s_p = lax.dot_general(q, kp2[:, sl].astype(BF16), dn, preferred_element_type=F32)
        s_c = jnp.where(valid_c, s_c - slope * dist_c.astype(F32), NEG_BIG)
        s_p = jnp.where(valid_p, s_p - slope * dist_p.astype(F32), NEG_BIG)
        m = jnp.maximum(jnp.max(s_c, -1, keepdims=True), jnp.max(s_p, -1, keepdims=True))
        p_c = jnp.exp(s_c - m)
        p_p = jnp.exp(s_p - m)
        den = jnp.sum(p_c, -1, keepdims=True) + jnp.sum(p_p, -1, keepdims=True)
        num = (jnp.dot(p_c.astype(BF16), vc2[:, sl].astype(BF16), preferred_element_type=F32)
               + jnp.dot(p_p.astype(BF16), vp2[:, sl].astype(BF16), preferred_element_type=F32))
        num_ref[0, :, sl] = num
        den_ref[0, :, sl] = jnp.broadcast_to(den, num.shape)
        m_ref[0, :, sl] = jnp.broadcast_to(m, num.shape)


def att_group(P, g, dilation):
    B, S, NP = P.shape
    L = S // dilation
    nb = L // ATT_BLOCK
    npb = NP // LANES
    Pv = P.reshape(B, L, dilation * NP)
    qo, ko, vo = (o // LANES + g * 2 for o in (OFF_ATT_Q, OFF_ATT_K, OFF_ATT_V))
    blk = (1, ATT_BLOCK, LANES)
    cur = lambda off: pl.BlockSpec(blk, lambda b, r, hp, n: (b, n, r * npb + off + hp))
    prev = lambda off: pl.BlockSpec(blk, lambda b, r, hp, n: (b, jnp.maximum(n - 1, 0), r * npb + off + hp))
    ospec = pl.BlockSpec(blk, lambda b, r, hp, n: (b, n, r * 2 + hp))
    oshape = jax.ShapeDtypeStruct((B, L, dilation * ATT_OUT_WIDTH), F32)
    slopes = ALIBI_SLOPES[g * 4:(g + 1) * 4]
    num, den, m = pl.pallas_call(
        functools.partial(_att_kernel, slopes=slopes, dilation=dilation),
        out_shape=(oshape, oshape, oshape),
        grid=(B, dilation, 2, nb),
        in_specs=[cur(qo), cur(ko), prev(ko), cur(vo), prev(vo)],
        out_specs=(ospec, ospec, ospec),
        compiler_params=_cparams(("parallel", "parallel", "parallel", "arbitrary")),
        name=f"dilated_attention_g{g}",
    )(Pv, Pv, Pv, Pv, Pv)
    shp = (B, S, ATT_OUT_WIDTH)
    return num.reshape(shp), den.reshape(shp), m.reshape(shp)


def _att_merge_kernel(*refs):
    o_ref = refs[-1]
    nums, dens, ms = refs[0:3], refs[3:6], refs[6:9]
    mx = jnp.maximum(jnp.maximum(ms[0][0], ms[1][0]), ms[2][0])
    num = jnp.zeros_like(mx)
    den = jnp.zeros_like(mx)
    for g in range(3):
        e = jnp.exp(ms[g][0] - mx)
        num = num + nums[g][0] * e
        den = den + dens[g][0] * e
    o_ref[0] = num / den


def dilated_attention(P):
    B, S, _ = P.shape
    parts = [att_group(P, g, d) for g, (_, d) in enumerate(ATT_GROUPS)]
    args = [p[0] for p in parts] + [p[1] for p in parts] + [p[2] for p in parts]
    ts = 1024
    spec = pl.BlockSpec((1, ts, ATT_OUT_WIDTH), lambda b, i: (b, i, 0))
    return pl.pallas_call(
        _att_merge_kernel,
        out_shape=jax.ShapeDtypeStruct((B, S, ATT_OUT_WIDTH), F32),
        grid=(B, S // ts),
        in_specs=[spec] * 9,
        out_specs=spec,
        compiler_params=_cparams(("parallel", "parallel")),
        name="dilated_attention_merge",
    )(*args)


GLA_TS = 256


def _gla_kernel(q_ref, k_ref, v_ref, g_ref, a_ref, wa_ref, ba_ref, ng_ref, o_ref, st_ref):
    @pl.when(pl.program_id(1) == 0)
    def _():
        st_ref[...] = jnp.zeros_like(st_ref)

    TS = GLA_TS
    nchunk = TS // GLA_CHUNK
    logit = jnp.dot(a_ref[0], wa_ref[...], precision=HIGHEST, preferred_element_type=F32) + ba_ref[...]
    log_a = (jnp.minimum(logit, 0.0) - jnp.log1p(jnp.exp(-jnp.abs(logit)))) / GLA_GATE_TEMP
    i = lax.broadcasted_iota(jnp.int32, (TS, TS), 0)
    j = lax.broadcasted_iota(jnp.int32, (TS, TS), 1)
    same_chunk_causal = jnp.logical_and(i // GLA_CHUNK == j // GLA_CHUNK, j <= i)
    tri = same_chunk_causal.astype(F32)
    bcum = jnp.dot(tri, log_a, precision=HIGHEST, preferred_element_type=F32)
    eb = jnp.exp(bcum)
    qd = q_ref[0] * (GLA_DK ** -0.5) * eb
    kd = k_ref[0] * jnp.exp(-bcum)
    k_all = k_ref[0]
    v_all = v_ref[0]
    dn_nt = (((1,), (1,)), ((), ()))
    dn_tn = (((0,), (0,)), ((), ()))
    for h in range(GLA_HEADS):
        ks = slice(h * GLA_DK, (h + 1) * GLA_DK)
        vs = slice(h * GLA_DV, (h + 1) * GLA_DV)
        qh = qd[:, ks].astype(BF16)
        vh = v_all[:, vs].astype(BF16)
        scores = lax.dot_general(qh, kd[:, ks].astype(BF16), dn_nt, preferred_element_type=F32)
        scores = jnp.where(same_chunk_causal, scores, 0.0)
        o_intra = jnp.dot(scores.astype(BF16), vh, preferred_element_type=F32)
        st = st_ref[h]
        outs = []
        for c in range(nchunk):
            rs = slice(c * GLA_CHUNK, (c + 1) * GLA_CHUNK)
            last = (c + 1) * GLA_CHUNK - 1
            o_c = o_intra[rs] + lax.dot_general(qh[rs], st.astype(BF16), dn_nt, preferred_element_type=F32)
            outs.append(o_c)
            b_last = bcum[last:last + 1, ks]
            k_tail = (k_all[rs, ks] * jnp.exp(b_last - bcum[rs, ks])).astype(BF16)
            st = st * jnp.exp(b_last) + lax.dot_general(vh[rs], k_tail, dn_tn, preferred_element_type=F32)
        st_ref[h] = st
        o = jnp.concatenate(outs, axis=0)
        o = o * lax.rsqrt(jnp.mean(o * o, -1, keepdims=True) + GLA_NORM_EPS)
        gg = g_ref[0][:, vs]
        o_ref[0, :, vs] = o * ng_ref[:, vs] * (gg * jax.nn.sigmoid(gg))


def gla_mixer(P, w_alpha, b_alpha, norm_g):
    B, S, _ = P.shape
    TS = GLA_TS
    wa = jnp.zeros((GLA_A_PAD, GLA_KEY_WIDTH), F32).at[:GLA_GATE_RANK].set(w_alpha)
    col = lambda off, w: pl.BlockSpec((1, TS, w), lambda b, i: (b, i, off // w))
    full = lambda r, c: pl.BlockSpec((r, c), lambda b, i: (0, 0))
    return pl.pallas_call(
        _gla_kernel,
        out_shape=jax.ShapeDtypeStruct((B, S, GLA_VAL_WIDTH), F32),
        grid=(B, S // TS),
        in_specs=[col(OFF_GLA_Q, GLA_KEY_WIDTH), col(OFF_GLA_K, GLA_KEY_WIDTH),
                  col(OFF_GLA_V, GLA_VAL_WIDTH), col(OFF_GLA_G, GLA_VAL_WIDTH),
                  col(OFF_GLA_A, GLA_A_PAD),
                  full(GLA_A_PAD, GLA_KEY_WIDTH), full(1, GLA_KEY_WIDTH), full(1, GLA_VAL_WIDTH)],
        out_specs=pl.BlockSpec((1, TS, GLA_VAL_WIDTH), lambda b, i: (b, i, 0)),
        scratch_shapes=[pltpu.VMEM((GLA_HEADS, GLA_DV, GLA_DK), F32)],
        compiler_params=_cparams(("parallel", "arbitrary")),
        name="gla_mixer",
    )(P, P, P, P, P, wa, b_alpha.reshape(1, -1), norm_g.reshape(1, -1))


RWKV_PREP_TS = 512
RWKV_SHIFT_LO = RWKV_DECAY_RANK + RWKV_AAA_RANK + RWKV_GATE_RANK


def _segment_ones(n, seg):
    i = lax.broadcasted_iota(jnp.int32, (n, n), 0)
    j = lax.broadcasted_iota(jnp.int32, (n, n), 1)
    return (i // seg == j // seg).astype(F32)


def _rwkv_prep_kernel(rkv_ref, lo_ref, rkv_prev_ref, lo_prev_ref, mu_rkv_ref, mu_lo_ref, w0_ref, w2_ref,
                      a0_ref, a2_ref, g2_ref, kk_ref, ka_ref, rk_ref,
                      r_out, w_out, k_out, v_out, kk_out, b_out, gate_out, bonus_out):
    first = pl.program_id(1) == 0

    def shifted(cur, prev_ref, mu):
        prev_row = jnp.where(first, 0.0, prev_ref[0][SUBLANES - 1:SUBLANES, :])
        row = lax.broadcasted_iota(jnp.int32, cur.shape, 0)
        prev = jnp.where(row == 0, prev_row, pltpu.roll(cur, 1, 0))
        return cur + (prev - cur) * mu

    xs = shifted(rkv_ref[0], rkv_prev_ref, mu_rkv_ref[...])
    lo = shifted(lo_ref[0], lo_prev_ref, mu_lo_ref[...])
    W = RWKV_WIDTH
    r, k, v = xs[:, 0:W], xs[:, W:2 * W], xs[:, 2 * W:3 * W]
    hdot = functools.partial(jnp.dot, precision=HIGHEST, preferred_element_type=F32)
    z = w0_ref[...] + hdot(jnp.tanh(lo), w2_ref[...])
    decay = jnp.exp(-jax.nn.sigmoid(z) * float(np.exp(-0.5)))
    a = jax.nn.sigmoid(a0_ref[...] + hdot(lo, a2_ref[...]))
    gate = hdot(jax.nn.sigmoid(lo), g2_ref[...])
    seg = _segment_ones(W, RWKV_HEAD_DIM)
    kk = k * kk_ref[...]
    kk_norm = jnp.sqrt(hdot(kk * kk, seg))
    kk = kk / jnp.maximum(kk_norm, 1e-12)
    k2 = k * (1.0 + (a - 1.0) * ka_ref[...])
    bonus = hdot(r * k2 * rk_ref[...], seg) * v
    r_out[0] = r
    w_out[0] = decay
    k_out[0] = k2
    v_out[0] = v
    kk_out[0] = kk
    b_out[0] = kk * a
    gate_out[0] = gate
    bonus_out[0] = bonus


def rwkv_prep(P, mu, w0, w2, a0, a2, g2, k_k, k_a, r_k):
    B, S, _ = P.shape
    TS = RWKV_PREP_TS
    W = RWKV_WIDTH
    W3 = 3 * W
    n_rkv = SRC_RWKV_RKV[1] - SRC_RWKV_RKV[0]
    mu_rkv = mu[:n_rkv].reshape(1, W3)
    mu_lo = jnp.zeros((1, RWKV_LO_PAD), F32).at[0, :RWKV_SHIFT_LO].set(mu[n_rkv:])
    w2p = jnp.zeros((RWKV_LO_PAD, W), F32).at[0:RWKV_DECAY_RANK].set(w2)
    a2p = jnp.zeros((RWKV_LO_PAD, W), F32).at[RWKV_DECAY_RANK:RWKV_DECAY_RANK + RWKV_AAA_RANK].set(a2)
    g2p = jnp.zeros((RWKV_LO_PAD, W), F32).at[RWKV_DECAY_RANK + RWKV_AAA_RANK:RWKV_SHIFT_LO].set(g2)
    row = lambda a: a.reshape(1, -1)
    full = lambda r, c: pl.BlockSpec((r, c), lambda b, i: (0, 0))
    tpb = TS // SUBLANES
    prev_map = lambda off, w: pl.BlockSpec((1, SUBLANES, w), lambda b, i: (b, jnp.maximum(i * tpb - 1, 0), off // w))
    oshape = jax.ShapeDtypeStruct((B, S, W), F32)
    ospec = pl.BlockSpec((1, TS, W), lambda b, i: (b, i, 0))
    return pl.pallas_call(
        _rwkv_prep_kernel,
        out_shape=(oshape,) * 8,
        grid=(B, S // TS),
        in_specs=[pl.BlockSpec((1, TS, W3), lambda b, i: (b, i, OFF_RWKV_RKV // W3)),
                  pl.BlockSpec((1, TS, RWKV_LO_PAD), lambda b, i: (b, i, OFF_RWKV_LO // RWKV_LO_PAD)),
                  prev_map(OFF_RWKV_RKV, W3), prev_map(OFF_RWKV_LO, RWKV_LO_PAD),
                  full(1, W3), full(1, RWKV_LO_PAD), full(1, W), full(RWKV_LO_PAD, W),
                  full(1, W), full(RWKV_LO_PAD, W), full(RWKV_LO_PAD, W),
                  full(1, W), full(1, W), full(1, W)],
        out_specs=(ospec,) * 8,
        compiler_params=_cparams(("parallel", "arbitrary")),
        name="rwkv7_prep",
    )(P, P, P, P, mu_rkv, mu_lo, row(w0), w2p, row(a0), a2p, g2p, row(k_k), row(k_a), row(r_k))


RWKV_SCAN_TT = 512
RWKV_SCAN_UNROLL = 8
RWKV_SCAN_PAIRS = 2


def _rwkv_scan_kernel(r_ref, w_ref, k_ref, v_ref, kk_ref, b_ref, y_ref, st_ref, slab_ref):
    @pl.when(pl.program_id(2) == 0)
    def _():
        st_ref[...] = jnp.zeros_like(st_ref)

    slab_ref[...] = jnp.zeros_like(slab_ref)
    N = RWKV_HEAD_DIM
    NP = RWKV_SCAN_PAIRS
    U = RWKV_SCAN_UNROLL
    subs = LANES // U
    lane = lax.broadcasted_iota(jnp.int32, (N, LANES), 1)
    rowi = lax.broadcasted_iota(jnp.int32, (N, LANES), 0)
    lo_half = lane < N
    diag_lo = jnp.logical_and(lane == rowi, lo_half)
    diag_hi = lane == rowi + N

    def half_sums(x):
        s0 = jnp.sum(jnp.where(lo_half, x, 0.0), -1, keepdims=True)
        s1 = jnp.sum(jnp.where(lo_half, 0.0, x), -1, keepdims=True)
        return s0, s1

    def block(tb, states):
        t0 = pl.multiple_of(tb * U, U)
        rows = [ref[0, pl.ds(t0, U), :] for ref in (r_ref, w_ref, k_ref, v_ref, kk_ref, b_ref)]
        sub = tb % subs
        base = sub * U
        new_states = []
        for p in range(NP):
            ls = slice(p * LANES, (p + 1) * LANES)
            S = states[p]
            y0s, y1s = [], []
            for u in range(U):
                r_t, w_t, k_t, v_t, kk_t, b_t = (x[u:u + 1, ls] for x in rows)
                rem0, rem1 = half_sums(S * kk_t)
                rem = jnp.where(lo_half, rem0, rem1)
                v0 = jnp.sum(jnp.where(diag_lo, v_t, 0.0), -1, keepdims=True)
                v1 = jnp.sum(jnp.where(diag_hi, v_t, 0.0), -1, keepdims=True)
                vcol = jnp.where(lo_half, v0, v1)
                S = S * w_t - rem * b_t + vcol * k_t
                y0, y1 = half_sums(S * r_t)
                y0s.append(y0)
                y1s.append(y1)
            new_states.append(S)
            for hh, ys in enumerate((y0s, y1s)):
                slab = slab_ref[2 * p + hh]
                for u in range(U):
                    slab = jnp.where(lane == base + u, ys[u], slab)
                slab_ref[2 * p + hh] = slab

        @pl.when(sub == subs - 1)
        def _():
            for h in range(2 * NP):
                y_ref[0, h, tb // subs] = slab_ref[h]

        return tuple(new_states)

    init = tuple(st_ref[p] for p in range(NP))
    final = lax.fori_loop(0, RWKV_SCAN_TT // U, block, init)
    for p in range(NP):
        st_ref[p] = final[p]


def rwkv_scan(r, w, k, v, kk, b):
    B, S, W = r.shape
    TT = RWKV_SCAN_TT
    lanes = RWKV_SCAN_PAIRS * LANES
    hpi = 2 * RWKV_SCAN_PAIRS
    ng = W // lanes
    spec = pl.BlockSpec((1, TT, lanes), lambda bb, g, i: (bb, i, g))
    return pl.pallas_call(
        _rwkv_scan_kernel,
        out_shape=jax.ShapeDtypeStruct((B, RWKV_HEADS, S // LANES, RWKV_HEAD_DIM, LANES), F32),
        grid=(B, ng, S // TT),
        in_specs=[spec] * 6,
        out_specs=pl.BlockSpec((1, hpi, TT // LANES, RWKV_HEAD_DIM, LANES), lambda bb, g, i: (bb, g, i, 0, 0)),
        scratch_shapes=[pltpu.VMEM((RWKV_SCAN_PAIRS, RWKV_HEAD_DIM, LANES), F32),
                        pltpu.VMEM((hpi, RWKV_HEAD_DIM, LANES), F32)],
        compiler_params=_cparams(("parallel", "parallel", "arbitrary")),
        name="rwkv7_scan",
    )(r, w, k, v, kk, b)


def _rwkv_post_kernel(y_ref, gate_ref, bonus_ref, g_ref, b_ref, o_ref):
    seg = _segment_ones(RWKV_WIDTH, RWKV_HEAD_DIM) * (1.0 / RWKV_HEAD_DIM)
    hdot = functools.partial(jnp.dot, precision=HIGHEST, preferred_element_type=F32)
    y = y_ref[0]
    mean = hdot(y, seg)
    d = y - mean
    var = hdot(d * d, seg)
    yn = d * lax.rsqrt(var + RWKV_GN_EPS) * g_ref[...] + b_ref[...]
    o_ref[0] = (yn + bonus_ref[0]) * gate_ref[0]


def rwkv_post(y, gate, bonus, ln_g, ln_b):
    B, S, W = y.shape
    ts = 512
    spec = pl.BlockSpec((1, ts, W), lambda b, i: (b, i, 0))
    full = pl.BlockSpec((1, W), lambda b, i: (0, 0))
    return pl.pallas_call(
        _rwkv_post_kernel,
        out_shape=jax.ShapeDtypeStruct((B, S, W), F32),
        grid=(B, S // ts),
        in_specs=[spec, spec, spec, full, full],
        out_specs=spec,
        compiler_params=_cparams(("parallel", "parallel")),
        name="rwkv7_post",
    )(y, gate, bonus, ln_g.reshape(1, W), ln_b.reshape(1, W))


def _merge_kernel(h_ref, wg_ref, ya_ref, yb_ref, yc_ref, yd_ref, wa_ref, wb_ref, wc_ref, wd_ref, o_ref):
    h = h_ref[...]
    acc = None
    for br, (y_ref, w_ref) in enumerate(((ya_ref, wa_ref), (yb_ref, wb_ref), (yc_ref, wc_ref), (yd_ref, wd_ref))):
        gate = jax.nn.sigmoid(jnp.dot(h, wg_ref[br], preferred_element_type=F32))
        proj = jnp.dot(y_ref[...].astype(BF16), w_ref[...], preferred_element_type=F32)
        acc = gate * proj if acc is None else acc + gate * proj
    o_ref[...] = acc.astype(o_ref.dtype)


def branch_merge(hb, wg, ys, ws):
    T, D = hb.shape
    tm, tn = 512, 512
    yspecs = [pl.BlockSpec((tm, y.shape[1]), lambda j, i: (i, 0)) for y in ys]
    wspecs = [pl.BlockSpec((w.shape[0], tn), lambda j, i: (0, j)) for w in ws]
    return pl.pallas_call(
        _merge_kernel,
        out_shape=jax.ShapeDtypeStruct((T, D), BF16),
        grid=(D // tn, T // tm),
        in_specs=[pl.BlockSpec((tm, D), lambda j, i: (i, 0)),
                  pl.BlockSpec((N_BRANCHES, D, tn), lambda j, i: (0, 0, j))] + yspecs + wspecs,
        out_specs=pl.BlockSpec((tm, tn), lambda j, i: (i, j)),
        compiler_params=_cparams(("parallel", "parallel")),
        name="branch_merge",
    )(hb, wg, *ys, *ws)


def _layer_norm(z, g, b):
    mu = jnp.mean(z, -1, keepdims=True)
    d = z - mu
    var = jnp.mean(d * d, -1, keepdims=True)
    return d * lax.rsqrt(var + LN_EPS) * g + b


def _out_ln_router_kernel(m_ref, wo_ref, x_ref, gate_ref, g_ref, b_ref, sc_ref, sh_ref, rw_ref, rb_ref,
                          x1_ref, h2_ref, idx_ref, wt_ref):
    y = jnp.dot(m_ref[...], wo_ref[...], preferred_element_type=F32)
    x1 = _layer_norm(DEEPNORM_ALPHA * x_ref[...] + gate_ref[0] * y, g_ref[...], b_ref[...])
    x1_ref[...] = x1
    h2 = x1 * (1.0 + sc_ref[0]) + sh_ref[0]
    h2_ref[...] = h2.astype(h2_ref.dtype)
    logits = jnp.dot(h2, rw_ref[...], precision=HIGHEST, preferred_element_type=F32) + rb_ref[...]
    lane = lax.broadcasted_iota(jnp.int32, logits.shape, 1).astype(F32)
    vals, idxs = [], []
    cur = logits
    for _ in range(TOP_K):
        m = jnp.max(cur, -1, keepdims=True)
        ix = jnp.min(jnp.where(cur == m, lane, float(LANES)), -1, keepdims=True)
        vals.append(m)
        idxs.append(ix)
        cur = jnp.where(lane == ix, -jnp.inf, cur)
    es = [jnp.exp(v - vals[0]) for v in vals]
    tot = es[0] + es[1] + es[2] + es[3]
    idx_o = jnp.zeros_like(logits)
    wt_o = jnp.zeros_like(logits)
    for kq in range(TOP_K):
        idx_o = jnp.where(lane == kq, idxs[kq], idx_o)
        wt_o = jnp.where(lane == kq, es[kq] / tot, wt_o)
    idx_ref[...] = idx_o.astype(jnp.int32)
    wt_ref[...] = wt_o


def out_ln_router(merged, w_out_b, x2d, gate1, ln_g, ln_b, scale2, shift2, router_w, router_b, S):
    T, D = x2d.shape
    tm = 256
    spb = S // tm
    rw = jnp.zeros((D, LANES), F32).at[:, :N_EXPERTS].set(router_w)
    rb = jnp.full((1, LANES), NEG_BIG, F32).at[0, :N_EXPERTS].set(router_b)
    rowblk = lambda w: pl.BlockSpec((tm, w), lambda i: (i, 0))
    full = lambda r, c: pl.BlockSpec((r, c), lambda i: (0, 0))
    perb = pl.BlockSpec((1, 1, D), lambda i: (i // spb, 0, 0))
    return pl.pallas_call(
        _out_ln_router_kernel,
        out_shape=(jax.ShapeDtypeStruct((T, D), F32), jax.ShapeDtypeStruct((T, D), BF16),
                   jax.ShapeDtypeStruct((T, LANES), jnp.int32), jax.ShapeDtypeStruct((T, LANES), F32)),
        grid=(T // tm,),
        in_specs=[rowblk(D), full(D, D), rowblk(D), perb, full(1, D), full(1, D), perb, perb,
                  full(D, LANES), full(1, LANES)],
        out_specs=(rowblk(D), rowblk(D), rowblk(LANES), rowblk(LANES)),
        compiler_params=_cparams(("parallel",)),
        name="out_proj_ln_router",
    )(merged, w_out_b, x2d, gate1, ln_g.reshape(1, D), ln_b.reshape(1, D), scale2, shift2, rw, rb)


def _expert_kernel(be_ref, nu_ref, x_ref, wgu_ref, bgu_ref, wd_ref, bd_ref, o_ref):
    i = pl.program_id(0)

    @pl.when(i < nu_ref[0])
    def _():
        gu = jnp.dot(x_ref[...], wgu_ref[0], preferred_element_type=F32) + bgu_ref[0]
        glu = jnp.minimum(gu[:, :EXPERT_FF], SWIGLU_LIMIT)
        lin = jnp.clip(gu[:, EXPERT_FF:], -SWIGLU_LIMIT, SWIGLU_LIMIT)
        act = glu * jax.nn.sigmoid(SWIGLU_ALPHA * glu) * (lin + 1.0)
        o_ref[...] = jnp.dot(act.astype(BF16), wd_ref[0], preferred_element_type=F32) + bd_ref[0]

    @pl.when(i >= nu_ref[0])
    def _():
        o_ref[...] = jnp.zeros_like(o_ref)


def expert_ffn(block_e, n_used, xb, wgu, bgu, wd, bd):
    rows, D = xb.shape
    E, _, F2 = wgu.shape
    nblk = rows // MOE_BM
    grid_spec = pltpu.PrefetchScalarGridSpec(
        num_scalar_prefetch=2,
        grid=(nblk,),
        in_specs=[pl.BlockSpec((MOE_BM, D), lambda i, be, nu: (i, 0)),
                  pl.BlockSpec((1, D, F2), lambda i, be, nu: (be[i], 0, 0)),
                  pl.BlockSpec((1, 1, F2), lambda i, be, nu: (be[i], 0, 0)),
                  pl.BlockSpec((1, F2 // 2, D), lambda i, be, nu: (be[i], 0, 0)),
                  pl.BlockSpec((1, 1, D), lambda i, be, nu: (be[i], 0, 0))],
        out_specs=pl.BlockSpec((MOE_BM, D), lambda i, be, nu: (i, 0)),
    )
    return pl.pallas_call(
        _expert_kernel,
        out_shape=jax.ShapeDtypeStruct((rows, D), F32),
        grid_spec=grid_spec,
        compiler_params=_cparams(("arbitrary",)),
        name="expert_ffn",
    )(block_e, n_used, xb, wgu, bgu.reshape(E, 1, F2), wd, bd.reshape(E, 1, D))


def routed_ffn(h2, top_idx, top_w, wgu, bgu, wd, bd):
    T, D = h2.shape
    E = N_EXPERTS
    n_pairs = T * TOP_K
    flat_e = top_idx.reshape(-1)
    onehot = (flat_e[:, None] == jnp.arange(E, dtype=jnp.int32)[None, :]).astype(jnp.int32)
    csum = jnp.cumsum(onehot, axis=0)
    counts = csum[-1]
    rank = jnp.take_along_axis(csum, flat_e[:, None], axis=1)[:, 0] - 1
    padded = ((counts + MOE_BM - 1) // MOE_BM) * MOE_BM
    pend = jnp.cumsum(padded)
    pstart = pend - padded
    dest = pstart[flat_e] + rank
    nblk = n_pairs // MOE_BM + E
    rows = nblk * MOE_BM
    slot_tok = jnp.zeros((rows,), jnp.int32).at[dest].set(jnp.arange(n_pairs, dtype=jnp.int32) // TOP_K)
    block_e = jnp.minimum(jnp.searchsorted(pend, jnp.arange(nblk, dtype=jnp.int32) * MOE_BM, side='right'),
                          E - 1).astype(jnp.int32)
    n_used = (pend[-1] // MOE_BM).astype(jnp.int32).reshape(1)
    xb = h2[slot_tok]
    yb = expert_ffn(block_e, n_used, xb, wgu, bgu, wd, bd)
    picked = yb[dest].reshape(T, TOP_K, D)
    return jnp.sum(picked * top_w[:, :, None], axis=1)


def _ln2_kernel(x_ref, y_ref, gate_ref, g_ref, b_ref, o_ref):
    o_ref[...] = _layer_norm(DEEPNORM_ALPHA * x_ref[...] + gate_ref[0] * y_ref[...], g_ref[...], b_ref[...])


def ln2(x1, y, gate2, ln_g, ln_b, S):
    T, D = x1.shape
    tm = 512
    spb = S // tm
    rowblk = pl.BlockSpec((tm, D), lambda i: (i, 0))
    full = pl.BlockSpec((1, D), lambda i: (0, 0))
    return pl.pallas_call(
        _ln2_kernel,
        out_shape=jax.ShapeDtypeStruct((T, D), F32),
        grid=(T // tm,),
        in_specs=[rowblk, rowblk, pl.BlockSpec((1, 1, D), lambda i: (i // spb, 0, 0)), full, full],
        out_specs=rowblk,
        compiler_params=_cparams(("parallel",)),
        name="deepnorm_ln2",
    )(x1, y, gate2, ln_g.reshape(1, D), ln_b.reshape(1, D))


def _mix_weights(w_in_l):
    def cols(rng, pad=0):
        part = w_in_l[:, rng[0]:rng[1]]
        if pad:
            part = jnp.pad(part, ((0, 0), (0, pad)))
        return part
    w_mix = jnp.concatenate([
        cols(SRC_POOL), cols(SRC_GLA_V), cols(SRC_GLA_G), cols(SRC_RWKV_RKV), cols(SRC_ATT),
        cols(SRC_GLA_Q), cols(SRC_GLA_K),
        cols(SRC_RWKV_LO, RWKV_LO_PAD - (SRC_RWKV_LO[1] - SRC_RWKV_LO[0])),
        cols(SRC_GLA_A, GLA_A_PAD - (SRC_GLA_A[1] - SRC_GLA_A[0]))], axis=1).astype(BF16)
    D = w_in_l.shape[0]
    w_gates = w_in_l[:, SRC_GATES[0]:].reshape(D, N_BRANCHES, D).transpose(1, 0, 2).astype(BF16)
    return w_mix, w_gates


def kernel(x, c, ada_w, ada_b, w_in, pool_w, pool_scale, gla_w_alpha, gla_b_alpha, gla_norm_g, rwkv_mu, rwkv_w0, rwkv_w2, rwkv_a0, rwkv_a2, rwkv_g2, rwkv_k_k, rwkv_k_a, rwkv_r_k, rwkv_ln_g, rwkv_ln_b, w_branch_a, w_branch_b, w_branch_c, w_branch_d, w_out, ln1_g, ln1_b, router_w, router_b, w_gate_up, b_gate_up, w_down, b_down, ln2_g, ln2_b):
    B, S, D = x.shape
    T = B * S
    mod = ada_modulation(c, ada_w, ada_b)
    for l in range(DEPTH):
        shift1, scale1, gate1, shift2, scale2, gate2 = (
            mod[l, :, None, i * D:(i + 1) * D] for i in range(6))
        w_mix, w_gates = _mix_weights(w_in[l])
        hb = modulate(x, scale1, shift1)
        hb2d = hb.reshape(T, D)
        P = matmul(hb2d, w_mix, 1024, MIX_TN).reshape(B, S, MIX_WIDTH)
        y_a = pool_mixer(P, pool_w[l], pool_scale[l])
        y_b = dilated_attention(P)
        y_c = gla_mixer(P, gla_w_alpha[l], gla_b_alpha[l], gla_norm_g[l])
        r_, w_, k_, v_, kk_, b_, gate_, bonus_ = rwkv_prep(
            P, rwkv_mu[l], rwkv_w0[l], rwkv_w2[l], rwkv_a0[l], rwkv_a2[l], rwkv_g2[l],
            rwkv_k_k[l], rwkv_k_a[l], rwkv_r_k[l].reshape(-1))
        y5 = rwkv_scan(r_, w_, k_, v_, kk_, b_)
        y_t = y5.transpose(0, 2, 4, 1, 3).reshape(B, S, RWKV_WIDTH)
        y_d = rwkv_post(y_t, gate_, bonus_, rwkv_ln_g[l], rwkv_ln_b[l])
        ys = [y.reshape(T, -1) for y in (y_a, y_b, y_c, y_d)]
        ws = [w[l].astype(BF16) for w in (w_branch_a, w_branch_b, w_branch_c, w_branch_d)]
        merged = branch_merge(hb2d, w_gates, ys, ws)
        x1, h2, idx, wt = out_ln_router(merged, w_out[l].astype(BF16), x.reshape(T, D), gate1,
                                        ln1_g[l], ln1_b[l], scale2, shift2, router_w[l], router_b[l], S)
        y_moe = routed_ffn(h2, idx[:, :TOP_K], wt[:, :TOP_K], w_gate_up[l].astype(BF16), b_gate_up[l],
                           w_down[l].astype(BF16), b_down[l])
        x = ln2(x1, y_moe, gate2, ln2_g[l], ln2_b[l], S).reshape(B, S, D)
    return x
```

```python
import functools

import numpy as np
import jax
import jax.numpy as jnp
from jax import lax
from jax.experimental import pallas as pl
from jax.experimental.pallas import tpu as pltpu

F32 = jnp.float32
BF16 = jnp.bfloat16
HIGHEST = lax.Precision.HIGHEST

D_MODEL = 2048
DEPTH = 2
POOL_WINDOWS = (2, 4, 8, 16)
POOL_GROUP = 128
POOL_WIDTH = 512
ATT_GROUPS = ((128, 1), (512, 4), (2048, 16))
ATT_HEAD_DIM = 64
ATT_HEADS = 12
ATT_WIDTH = 768
ATT_OUT_WIDTH = 256
ATT_BLOCK = 128
ALIBI_SLOPES = tuple(2.0 ** (-8.0 * (h + 1) / ATT_HEADS) for h in range(ATT_HEADS))
GLA_HEADS = 4
GLA_DK = 64
GLA_DV = 128
GLA_KEY_WIDTH = 256
GLA_VAL_WIDTH = 512
GLA_GATE_RANK = 16
GLA_GATE_TEMP = 16.0
GLA_CHUNK = 32
GLA_NORM_EPS = 1e-6
RWKV_HEADS = 8
RWKV_HEAD_DIM = 64
RWKV_WIDTH = 512
RWKV_DECAY_RANK = 32
RWKV_AAA_RANK = 32
RWKV_GATE_RANK = 96
RWKV_GN_EPS = 64e-5
N_BRANCHES = 4
N_EXPERTS = 32
TOP_K = 4
EXPERT_FF = 1024
SWIGLU_LIMIT = 7.0
SWIGLU_ALPHA = 1.702
LN_EPS = 1e-5
DEEPNORM_ALPHA = (2 * DEPTH) ** 0.25

LANES = 128
SUBLANES = 8
VMEM_LIMIT = 56 * 1024 * 1024

SRC_POOL = (0, 512)
SRC_ATT = (512, 2816)
SRC_GLA_Q = (2816, 3072)
SRC_GLA_K = (3072, 3328)
SRC_GLA_V = (3328, 3840)
SRC_GLA_G = (3840, 4352)
SRC_GLA_A = (4352, 4368)
SRC_RWKV_RKV = (4368, 5904)
SRC_RWKV_LO = (5904, 6064)
SRC_GATES = (6064, 14256)
OFF_POOL = 0
OFF_GLA_V = 512
OFF_GLA_G = 1024
OFF_RWKV_RKV = 1536
OFF_ATT_Q = 3072
OFF_ATT_K = OFF_ATT_Q + ATT_WIDTH
OFF_ATT_V = OFF_ATT_K + ATT_WIDTH
OFF_GLA_Q = 5376
OFF_GLA_K = 5632
OFF_RWKV_LO = 5888
RWKV_LO_PAD = 256
OFF_GLA_A = 6144
GLA_A_PAD = 128
MIX_WIDTH = 6272
MIX_TN = 896

MOE_BM = 256
NEG_BIG = -1e30


def _cparams(sem):
    return pltpu.CompilerParams(dimension_semantics=sem, vmem_limit_bytes=VMEM_LIMIT)


def _ada_kernel(c_ref, w_ref, b_ref, o_ref):
    c = c_ref[...]
    s = c * jax.nn.sigmoid(c)
    o_ref[0] = jnp.dot(s, w_ref[0], precision=HIGHEST, preferred_element_type=F32) + b_ref[0]


def ada_modulation(c, ada_w, ada_b):
    L, D, N = ada_w.shape
    B = c.shape[0]
    cp = jnp.zeros((SUBLANES, D), F32).at[:B].set(c)
    tn = 1024
    out = pl.pallas_call(
        _ada_kernel,
        out_shape=jax.ShapeDtypeStruct((L, SUBLANES, N), F32),
        grid=(L, N // tn),
        in_specs=[pl.BlockSpec((SUBLANES, D), lambda l, j: (0, 0)),
                  pl.BlockSpec((1, D, tn), lambda l, j: (l, 0, j)),
                  pl.BlockSpec((1, 1, tn), lambda l, j: (l, 0, j))],
        out_specs=pl.BlockSpec((1, SUBLANES, tn), lambda l, j: (l, 0, j)),
        compiler_params=_cparams(("parallel", "parallel")),
        name="ada_modulation",
    )(cp, ada_w, ada_b.reshape(L, 1, N))
    return out[:, :B]


def _modulate_kernel(x_ref, sc_ref, sh_ref, o_ref):
    o_ref[0] = (x_ref[0] * (1.0 + sc_ref[0]) + sh_ref[0]).astype(o_ref.dtype)


def modulate(x, scale, shift):
    B, S, D = x.shape
    ts = 1024
    return pl.pallas_call(
        _modulate_kernel,
        out_shape=jax.ShapeDtypeStruct((B, S, D), BF16),
        grid=(B, S // ts),
        in_specs=[pl.BlockSpec((1, ts, D), lambda b, i: (b, i, 0)),
                  pl.BlockSpec((1, 1, D), lambda b, i: (b, 0, 0)),
                  pl.BlockSpec((1, 1, D), lambda b, i: (b, 0, 0))],
        out_specs=pl.BlockSpec((1, ts, D), lambda b, i: (b, i, 0)),
        compiler_params=_cparams(("parallel", "parallel")),
        name="modulate",
    )(x, scale, shift)


def _mm_kernel(x_ref, w_ref, o_ref):
    o_ref[...] = jnp.dot(x_ref[...], w_ref[...], preferred_element_type=F32).astype(o_ref.dtype)


def matmul(x, w, tm, tn, out_dtype=F32):
    M, K = x.shape
    N = w.shape[1]
    return pl.pallas_call(
        _mm_kernel,
        out_shape=jax.ShapeDtypeStruct((M, N), out_dtype),
        grid=(N // tn, M // tm),
        in_specs=[pl.BlockSpec((tm, K), lambda j, i: (i, 0)),
                  pl.BlockSpec((K, tn), lambda j, i: (0, j))],
        out_specs=pl.BlockSpec((tm, tn), lambda j, i: (i, j)),
        compiler_params=_cparams(("parallel", "parallel")),
        name="matmul",
    )(x, w)


def _pool_kernel(p_ref, w_ref, sc_ref, o_ref):
    g = pl.program_id(1)
    v = p_ref[0]
    S = v.shape[0]
    row = lax.broadcasted_iota(jnp.int32, v.shape, 0)
    win = jnp.left_shift(2, g)
    s = v
    pooled_sum = v
    for k, sh in enumerate((1, 2, 4, 8)):
        s = s + jnp.where(row >= sh, pltpu.roll(s, sh, 0), 0.0)
        pooled_sum = jnp.where(g >= k, s, pooled_sum)
    cnt = jnp.minimum(row + 1, win).astype(F32)
    diff = pooled_sum / cnt - v
    y = jnp.dot(diff.astype(BF16), w_ref[0].astype(BF16), preferred_element_type=F32)
    o_ref[0] = y * sc_ref[...]


def pool_mixer(P, pool_w, pool_scale):
    B, S, _ = P.shape
    G = len(POOL_WINDOWS)
    return pl.pallas_call(
        _pool_kernel,
        out_shape=jax.ShapeDtypeStruct((B, S, POOL_WIDTH), F32),
        grid=(B, G),
        in_specs=[pl.BlockSpec((1, S, POOL_GROUP), lambda b, g: (b, 0, OFF_POOL // POOL_GROUP + g)),
                  pl.BlockSpec((1, POOL_GROUP, POOL_GROUP), lambda b, g: (g, 0, 0)),
                  pl.BlockSpec((1, POOL_GROUP), lambda b, g: (0, g))],
        out_specs=pl.BlockSpec((1, S, POOL_GROUP), lambda b, g: (b, 0, g)),
        compiler_params=_cparams(("parallel", "parallel")),
        name="pool_mixer",
    )(P, pool_w, pool_scale.reshape(1, POOL_WIDTH))


def _att_kernel(q_ref, kc_ref, kp_ref, vc_ref, vp_ref, num_ref, den_ref, m_ref, *, slopes, dilation):
    hp = pl.program_id(1)
    n = pl.program_id(2)
    i = lax.broadcasted_iota(jnp.int32, (ATT_BLOCK, ATT_BLOCK), 0)
    j = lax.broadcasted_iota(jnp.int32, (ATT_BLOCK, ATT_BLOCK), 1)
    dist_c = i - j
    dist_p = dist_c + ATT_BLOCK
    valid_c = dist_c >= 0
    valid_p = jnp.logical_and(dist_p <= ATT_BLOCK, n > 0)
    lane_lo = lax.broadcasted_iota(jnp.int32, (ATT_BLOCK, LANES), 1) < ATT_HEAD_DIM
    dn = (((1,), (1,)), ((), ()))

    def residue(r, carry):
        rows = slice(None) if dilation == 1 else pl.ds(r, ATT_BLOCK, stride=dilation)
        q2, kc2, kp2, vc2, vp2 = (ref[0, rows, :] for ref in (q_ref, kc_ref, kp_ref, vc_ref, vp_ref))
        nums, dens, ms = [], [], []
        for h in range(2):
            sl = slice(h * ATT_HEAD_DIM, (h + 1) * ATT_HEAD_DIM)
            slope = jnp.where(hp == 0, slopes[h], slopes[2 + h]) * float(dilation)
            q = (q2[:, sl] * (ATT_HEAD_DIM ** -0.5)).astype(BF16)
            s_c = lax.dot_general(q, kc2[:, sl].astype(BF16), dn, preferred_element_type=F32)
            s_p = lax.dot_general(q, kp2[:, sl].astype(BF16), dn, preferred_element_type=F32)
            s_c = jnp.where(valid_c, s_c - slope * dist_c.astype(F32), NEG_BIG)
            s_p = jnp.where(valid_p, s_p - slope * dist_p.astype(F32), NEG_BIG)
            m = jnp.maximum(jnp.max(s_c, -1, keepdims=True), jnp.max(s_p, -1, keepdims=True))
            p_c = jnp.exp(s_c - m)
            p_p = jnp.exp(s_p - m)
            dens.append(jnp.sum(p_c, -1, keepdims=True) + jnp.sum(p_p, -1, keepdims=True))
            nums.append(jnp.dot(p_c.astype(BF16), vc2[:, sl].astype(BF16), preferred_element_type=F32)
                        + jnp.dot(p_p.astype(BF16), vp2[:, sl].astype(BF16), preferred_element_type=F32))
            ms.append(m)
        num_ref[0, rows, :] = jnp.concatenate(nums, axis=-1)
        den_ref[0, rows, :] = jnp.where(lane_lo, dens[0], dens[1])
        m_ref[0, rows, :] = jnp.where(lane_lo, ms[0], ms[1])
        return carry

    if dilation == 1:
        residue(0, 0)
    else:
        lax.fori_loop(0, dilation, residue, 0)


def att_group(P, g, dilation):
    B, S, NP = P.shape
    rows = ATT_BLOCK * dilation
    nb = S // rows
    qo, ko, vo = (o // LANES + g * 2 for o in (OFF_ATT_Q, OFF_ATT_K, OFF_ATT_V))
    blk = (1, rows, LANES)
    cur = lambda off: pl.BlockSpec(blk, lambda b, hp, n: (b, n, off + hp))
    prev = lambda off: pl.BlockSpec(blk, lambda b, hp, n: (b, jnp.maximum(n - 1, 0), off + hp))
    ospec = pl.BlockSpec(blk, lambda b, hp, n: (b, n, hp))
    oshape = jax.ShapeDtypeStruct((B, S, ATT_OUT_WIDTH), F32)
    slopes = ALIBI_SLOPES[g * 4:(g + 1) * 4]
    return pl.pallas_call(
        functools.partial(_att_kernel, slopes=slopes, dilation=dilation),
        out_shape=(oshape, oshape, oshape),
        grid=(B, 2, nb),
        in_specs=[cur(qo), cur(ko), prev(ko), cur(vo), prev(vo)],
        out_specs=(ospec, ospec, ospec),
        compiler_params=_cparams(("parallel", "parallel", "arbitrary")),
        name=f"dilated_attention_g{g}",
    )(P, P, P, P, P)


def _att_merge_kernel(*refs):
    o_ref = refs[-1]
    nums, dens, ms = refs[0:3], refs[3:6], refs[6:9]
    mx = jnp.maximum(jnp.maximum(ms[0][0], ms[1][0]), ms[2][0])
    num = jnp.zeros_like(mx)
    den = jnp.zeros_like(mx)
    for g in range(3):
        e = jnp.exp(ms[g][0] - mx)
        num = num + nums[g][0] * e
        den = den + dens[g][0] * e
    o_ref[0] = num / den


def dilated_attention(P):
    B, S, _ = P.shape
    parts = [att_group(P, g, d) for g, (_, d) in enumerate(ATT_GROUPS)]
    args = [p[0] for p in parts] + [p[1] for p in parts] + [p[2] for p in parts]
    ts = 1024
    spec = pl.BlockSpec((1, ts, ATT_OUT_WIDTH), lambda b, i: (b, i, 0))
    return pl.pallas_call(
        _att_merge_kernel,
        out_shape=jax.ShapeDtypeStruct((B, S, ATT_OUT_WIDTH), F32),
        grid=(B, S // ts),
        in_specs=[spec] * 9,
        out_specs=spec,
        compiler_params=_cparams(("parallel", "parallel")),
        name="dilated_attention_merge",
    )(*args)


GLA_TS = 256


def _gla_kernel(q_ref, k_ref, v_ref, g_ref, a_ref, wa_ref, ba_ref, ng_ref, o_ref, st_ref):
    @pl.when(pl.program_id(1) == 0)
    def _():
        st_ref[...] = jnp.zeros_like(st_ref)

    TS = GLA_TS
    nchunk = TS // GLA_CHUNK
    logit = jnp.dot(a_ref[0], wa_ref[...], precision=HIGHEST, preferred_element_type=F32) + ba_ref[...]
    log_a = (jnp.minimum(logit, 0.0) - jnp.log1p(jnp.exp(-jnp.abs(logit)))) / GLA_GATE_TEMP
    i = lax.broadcasted_iota(jnp.int32, (TS, TS), 0)
    j = lax.broadcasted_iota(jnp.int32, (TS, TS), 1)
    same_chunk_causal = jnp.logical_and(i // GLA_CHUNK == j // GLA_CHUNK, j <= i)
    tri = same_chunk_causal.astype(F32)
    bcum = jnp.dot(tri, log_a, precision=HIGHEST, preferred_element_type=F32)
    eb = jnp.exp(bcum)
    qd = q_ref[0] * (GLA_DK ** -0.5) * eb
    kd = k_ref[0] * jnp.exp(-bcum)
    k_all = k_ref[0]
    v_all = v_ref[0]
    dn_nt = (((1,), (1,)), ((), ()))
    dn_tn = (((0,), (0,)), ((), ()))
    for h in range(GLA_HEADS):
        ks = slice(h * GLA_DK, (h + 1) * GLA_DK)
        vs = slice(h * GLA_DV, (h + 1) * GLA_DV)
        qh = qd[:, ks].astype(BF16)
        vh = v_all[:, vs].astype(BF16)
        scores = lax.dot_general(qh, kd[:, ks].astype(BF16), dn_nt, preferred_element_type=F32)
        scores = jnp.where(same_chunk_causal, scores, 0.0)
        o_intra = jnp.dot(scores.astype(BF16), vh, preferred_element_type=F32)
        st = st_ref[h]
        outs = []
        for c in range(nchunk):
            rs = slice(c * GLA_CHUNK, (c + 1) * GLA_CHUNK)
            last = (c + 1) * GLA_CHUNK - 1
            o_c = o_intra[rs] + lax.dot_general(qh[rs], st.astype(BF16), dn_nt, preferred_element_type=F32)
            outs.append(o_c)
            b_last = bcum[last:last + 1, ks]
            k_tail = (k_all[rs, ks] * jnp.exp(b_last - bcum[rs, ks])).astype(BF16)
            st = st * jnp.exp(b_last) + lax.dot_general(vh[rs], k_tail, dn_tn, preferred_element_type=F32)
        st_ref[h] = st
        o = jnp.concatenate(outs, axis=0)
        o = o * lax.rsqrt(jnp.mean(o * o, -1, keepdims=True) + GLA_NORM_EPS)
        gg = g_ref[0][:, vs]
        o_ref[0, :, vs] = o * ng_ref[:, vs] * (gg * jax.nn.sigmoid(gg))


def gla_mixer(P, w_alpha, b_alpha, norm_g):
    B, S, _ = P.shape
    TS = GLA_TS
    wa = jnp.zeros((GLA_A_PAD, GLA_KEY_WIDTH), F32).at[:GLA_GATE_RANK].set(w_alpha)
    col = lambda off, w: pl.BlockSpec((1, TS, w), lambda b, i: (b, i, off // w))
    full = lambda r, c: pl.BlockSpec((r, c), lambda b, i: (0, 0))
    return pl.pallas_call(
        _gla_kernel,
        out_shape=jax.ShapeDtypeStruct((B, S, GLA_VAL_WIDTH), F32),
        grid=(B, S // TS),
        in_specs=[col(OFF_GLA_Q, GLA_KEY_WIDTH), col(OFF_GLA_K, GLA_KEY_WIDTH),
                  col(OFF_GLA_V, GLA_VAL_WIDTH), col(OFF_GLA_G, GLA_VAL_WIDTH),
                  col(OFF_GLA_A, GLA_A_PAD),
                  full(GLA_A_PAD, GLA_KEY_WIDTH), full(1, GLA_KEY_WIDTH), full(1, GLA_VAL_WIDTH)],
        out_specs=pl.BlockSpec((1, TS, GLA_VAL_WIDTH), lambda b, i: (b, i, 0)),
        scratch_shapes=[pltpu.VMEM((GLA_HEADS, GLA_DV, GLA_DK), F32)],
        compiler_params=_cparams(("parallel", "arbitrary")),
        name="gla_mixer",
    )(P, P, P, P, P, wa, b_alpha.reshape(1, -1), norm_g.reshape(1, -1))


RWKV_PREP_TS = 512
RWKV_SHIFT_LO = RWKV_DECAY_RANK + RWKV_AAA_RANK + RWKV_GATE_RANK


def _segment_ones(n, seg):
    i = lax.broadcasted_iota(jnp.int32, (n, n), 0)
    j = lax.broadcasted_iota(jnp.int32, (n, n), 1)
    return (i // seg == j // seg).astype(F32)


def _rwkv_prep_kernel(rkv_ref, lo_ref, rkv_prev_ref, lo_prev_ref, mu_rkv_ref, mu_lo_ref, w0_ref, w2_ref,
                      a0_ref, a2_ref, g2_ref, kk_ref, ka_ref, rk_ref,
                      r_out, w_out, k_out, v_out, kk_out, b_out, gate_out, bonus_out):
    first = pl.program_id(1) == 0

    def shifted(cur, prev_ref, mu):
        prev_row = jnp.where(first, 0.0, prev_ref[0][SUBLANES - 1:SUBLANES, :])
        row = lax.broadcasted_iota(jnp.int32, cur.shape, 0)
        prev = jnp.where(row == 0, prev_row, pltpu.roll(cur, 1, 0))
        return cur + (prev - cur) * mu

    xs = shifted(rkv_ref[0], rkv_prev_ref, mu_rkv_ref[...])
    lo = shifted(lo_ref[0], lo_prev_ref, mu_lo_ref[...])
    W = RWKV_WIDTH
    r, k, v = xs[:, 0:W], xs[:, W:2 * W], xs[:, 2 * W:3 * W]
    hdot = functools.partial(jnp.dot, precision=HIGHEST, preferred_element_type=F32)
    z = w0_ref[...] + hdot(jnp.tanh(lo), w2_ref[...])
    log_decay = -jax.nn.sigmoid(z) * float(np.exp(-0.5))
    a = jax.nn.sigmoid(a0_ref[...] + hdot(lo, a2_ref[...]))
    gate = hdot(jax.nn.sigmoid(lo), g2_ref[...])
    seg = _segment_ones(W, RWKV_HEAD_DIM)
    kk = k * kk_ref[...]
    kk_norm = jnp.sqrt(hdot(kk * kk, seg))
    kk = kk / jnp.maximum(kk_norm, 1e-12)
    k2 = k * (1.0 + (a - 1.0) * ka_ref[...])
    bonus = hdot(r * k2 * rk_ref[...], seg) * v
    r_out[0] = r
    w_out[0] = log_decay
    k_out[0] = k2
    v_out[0] = v
    kk_out[0] = kk
    b_out[0] = kk * a
    gate_out[0] = gate
    bonus_out[0] = bonus


def rwkv_prep(P, mu, w0, w2, a0, a2, g2, k_k, k_a, r_k):
    B, S, _ = P.shape
    TS = RWKV_PREP_TS
    W = RWKV_WIDTH
    W3 = 3 * W
    n_rkv = SRC_RWKV_RKV[1] - SRC_RWKV_RKV[0]
    mu_rkv = mu[:n_rkv].reshape(1, W3)
    mu_lo = jnp.zeros((1, RWKV_LO_PAD), F32).at[0, :RWKV_SHIFT_LO].set(mu[n_rkv:])
    w2p = jnp.zeros((RWKV_LO_PAD, W), F32).at[0:RWKV_DECAY_RANK].set(w2)
    a2p = jnp.zeros((RWKV_LO_PAD, W), F32).at[RWKV_DECAY_RANK:RWKV_DECAY_RANK + RWKV_AAA_RANK].set(a2)
    g2p = jnp.zeros((RWKV_LO_PAD, W), F32).at[RWKV_DECAY_RANK + RWKV_AAA_RANK:RWKV_SHIFT_LO].set(g2)
    row = lambda a: a.reshape(1, -1)
    full = lambda r, c: pl.BlockSpec((r, c), lambda b, i: (0, 0))
    tpb = TS // SUBLANES
    prev_map = lambda off, w: pl.BlockSpec((1, SUBLANES, w), lambda b, i: (b, jnp.maximum(i * tpb - 1, 0), off // w))
    oshape = jax.ShapeDtypeStruct((B, S, W), F32)
    ospec = pl.BlockSpec((1, TS, W), lambda b, i: (b, i, 0))
    return pl.pallas_call(
        _rwkv_prep_kernel,
        out_shape=(oshape,) * 8,
        grid=(B, S // TS),
        in_specs=[pl.BlockSpec((1, TS, W3), lambda b, i: (b, i, OFF_RWKV_RKV // W3)),
                  pl.BlockSpec((1, TS, RWKV_LO_PAD), lambda b, i: (b, i, OFF_RWKV_LO // RWKV_LO_PAD)),
                  prev_map(OFF_RWKV_RKV, W3), prev_map(OFF_RWKV_LO, RWKV_LO_PAD),
                  full(1, W3), full(1, RWKV_LO_PAD), full(1, W), full(RWKV_LO_PAD, W),
                  full(1, W), full(RWKV_LO_PAD, W), full(RWKV_LO_PAD, W),
                  full(1, W), full(1, W), full(1, W)],
        out_specs=(ospec,) * 8,
        compiler_params=_cparams(("parallel", "arbitrary")),
        name="rwkv7_prep",
    )(P, P, P, P, mu_rkv, mu_lo, row(w0), w2p, row(a0), a2p, g2p, row(k_k), row(k_a), row(r_k))


RWKV_CHUNK = 64
RWKV_CHUNK_TT = 512


def _rwkv_chunk_kernel(r_ref, lw_ref, k_ref, v_ref, kk_ref, b_ref, y_ref, h_ref):
    @pl.when(pl.program_id(2) == 0)
    def _():
        h_ref[...] = jnp.zeros_like(h_ref)

    C = RWKV_CHUNK
    TT = RWKV_CHUNK_TT
    N = RWKV_HEAD_DIM
    R2 = 2 * C
    bdot = lambda x, y: jnp.dot(x.astype(BF16), y.astype(BF16), preferred_element_type=F32)
    r, lw, k, v, kk, b = (ref[0] for ref in (r_ref, lw_ref, k_ref, v_ref, kk_ref, b_ref))
    i = lax.broadcasted_iota(jnp.int32, (TT, TT), 0)
    j = lax.broadcasted_iota(jnp.int32, (TT, TT), 1)
    tri = jnp.logical_and(i // C == j // C, j <= i).astype(F32)
    cum = jnp.dot(tri, lw, precision=HIGHEST, preferred_element_type=F32)
    lane_lo = lax.broadcasted_iota(jnp.int32, (C, LANES), 1) < N

    def stack(x):
        return jnp.concatenate([jnp.where(lane_lo, x, 0.0), jnp.where(lane_lo, 0.0, x)], axis=0)

    ti = lax.broadcasted_iota(jnp.int32, (R2, R2), 0)
    si = lax.broadcasted_iota(jnp.int32, (R2, R2), 1)
    strict = ti > si
    incl = ti >= si
    eye = (ti == si).astype(F32)
    blk16 = ti // 16 == si // 16
    off32 = jnp.logical_and(ti // 32 == si // 32, jnp.logical_not(blk16))
    off64 = jnp.logical_and(ti // 64 == si // 64, ti // 32 != si // 32)
    ones_c = jnp.ones((C, LANES), F32)

    chunks = range(TT // C)
    each = lambda f, *lists: [f(*args) for args in zip(*lists)]
    rows = [slice(c * C, (c + 1) * C) for c in chunks]
    cu = [cum[rw] for rw in rows]
    cu_last = [cum[(c + 1) * C - 1:(c + 1) * C] for c in chunks]
    e_neg = each(lambda x: jnp.exp(-x), cu)
    e_tail = each(lambda x, xl: jnp.exp(xl - x), cu, cu_last)
    a_t = each(lambda rw, x: stack(-kk[rw] * jnp.exp(x - lw[rw])), rows, cu)
    r_t = each(lambda rw, x: stack(r[rw] * jnp.exp(x)), rows, cu)
    b_t = each(lambda rw, e: stack(b[rw] * e), rows, e_neg)
    k_t = each(lambda rw, e: stack(k[rw] * e), rows, e_neg)
    v_s = each(lambda rw: stack(v[rw]), rows)
    gram = each(lambda a_, r_, b_, k_: lax.dot_general(
        jnp.concatenate([a_, r_], 0).astype(BF16), jnp.concatenate([b_, k_], 0).astype(BF16),
        (((1,), (1,)), ((), ())), preferred_element_type=F32), a_t, r_t, b_t, k_t)
    n_ab = each(lambda g: jnp.where(strict, g[:R2, :R2], 0.0), gram)
    a_ak = each(lambda g: jnp.where(strict, g[:R2, R2:], 0.0), gram)
    a_rb = each(lambda g: jnp.where(incl, g[R2:, :R2], 0.0), gram)
    a_rk = each(lambda g: jnp.where(incl, g[R2:, R2:], 0.0), gram)
    av = each(lambda ak, rk, vv: bdot(jnp.concatenate([ak, rk], 0), vv), a_ak, a_rk, v_s)
    n1 = each(lambda n: jnp.where(blk16, n, 0.0), n_ab)
    n2 = each(lambda x: bdot(x, x), n1)
    n4 = each(lambda x: bdot(x, x), n2)
    n8 = each(lambda x: bdot(x, x), n4)
    xa = each(lambda x1, x2: x1 + x2 + bdot(x1, x2), n1, n2)
    xb = each(lambda x4, x8: x4 + x8 + bdot(x4, x8), n4, n8)
    t_inv = each(lambda p, q: eye + p + q + bdot(p, q), xa, xb)
    for off in (off32, off64):
        mid = each(lambda n, t: bdot(jnp.where(off, n, 0.0), t), n_ab, t_inv)
        t_inv = each(lambda t, m_: t + bdot(t, m_), t_inv, mid)
    tw = each(lambda t, a_, av_: bdot(t, jnp.concatenate([a_, av_[:R2]], axis=1)), t_inv, a_t, av)
    bk_t = each(lambda rw, e: jnp.transpose(jnp.concatenate([stack(b[rw] * e), stack(k[rw] * e)], 0)),
                rows, e_tail)
    pc_col = each(lambda rw: jnp.exp(jnp.dot(jnp.transpose(lw[rw]), ones_c, precision=HIGHEST,
                                             preferred_element_type=F32)), rows)
    w1r = each(lambda t, r_: jnp.concatenate([t[:, :LANES], r_], 0), tw, r_t)

    h = h_ref[...]
    for c in chunks:
        x = bdot(w1r[c], h)
        u = x[:R2] + tw[c][:, LANES:]
        y_bd = x[R2:] + bdot(a_rb[c], u) + av[c][R2:]
        y_ref[0, rows[c], :] = y_bd[:C] + y_bd[C:]
        h = pc_col[c] * h + bdot(bk_t[c], jnp.concatenate([u, v_s[c]], 0))
    h_ref[...] = h


def rwkv_chunked(r, lw, k, v, kk, b):
    B, S, W = r.shape
    TT = RWKV_CHUNK_TT
    spec = pl.BlockSpec((1, TT, LANES), lambda bb, p, i: (bb, i, p))
    return pl.pallas_call(
        _rwkv_chunk_kernel,
        out_shape=jax.ShapeDtypeStruct((B, S, W), F32),
        grid=(B, W // LANES, S // TT),
        in_specs=[spec] * 6,
        out_specs=spec,
        scratch_shapes=[pltpu.VMEM((LANES, LANES), F32)],
        compiler_params=_cparams(("parallel", "parallel", "arbitrary")),
        name="rwkv7_chunked",
    )(r, lw, k, v, kk, b)


def _rwkv_post_kernel(y_ref, gate_ref, bonus_ref, g_ref, b_ref, o_ref):
    seg = _segment_ones(RWKV_WIDTH, RWKV_HEAD_DIM) * (1.0 / RWKV_HEAD_DIM)
    hdot = functools.partial(jnp.dot, precision=HIGHEST, preferred_element_type=F32)
    y = y_ref[0]
    mean = hdot(y, seg)
    d = y - mean
    var = hdot(d * d, seg)
    yn = d * lax.rsqrt(var + RWKV_GN_EPS) * g_ref[...] + b_ref[...]
    o_ref[0] = (yn + bonus_ref[0]) * gate_ref[0]


def rwkv_post(y, gate, bonus, ln_g, ln_b):
    B, S, W = y.shape
    ts = 512
    spec = pl.BlockSpec((1, ts, W), lambda b, i: (b, i, 0))
    full = pl.BlockSpec((1, W), lambda b, i: (0, 0))
    return pl.pallas_call(
        _rwkv_post_kernel,
        out_shape=jax.ShapeDtypeStruct((B, S, W), F32),
        grid=(B, S // ts),
        in_specs=[spec, spec, spec, full, full],
        out_specs=spec,
        compiler_params=_cparams(("parallel", "parallel")),
        name="rwkv7_post",
    )(y, gate, bonus, ln_g.reshape(1, W), ln_b.reshape(1, W))


def _merge_kernel(h_ref, wg_ref, ya_ref, yb_ref, yc_ref, yd_ref, wa_ref, wb_ref, wc_ref, wd_ref, o_ref):
    h = h_ref[...]
    acc = None
    for br, (y_ref, w_ref) in enumerate(((ya_ref, wa_ref), (yb_ref, wb_ref), (yc_ref, wc_ref), (yd_ref, wd_ref))):
        gate = jax.nn.sigmoid(jnp.dot(h, wg_ref[br], preferred_element_type=F32))
        proj = jnp.dot(y_ref[...].astype(BF16), w_ref[...], preferred_element_type=F32)
        acc = gate * proj if acc is None else acc + gate * proj
    o_ref[...] = acc.astype(o_ref.dtype)


def branch_merge(hb, wg, ys, ws):
    T, D = hb.shape
    tm, tn = 512, 512
    yspecs = [pl.BlockSpec((tm, y.shape[1]), lambda j, i: (i, 0)) for y in ys]
    wspecs = [pl.BlockSpec((w.shape[0], tn), lambda j, i: (0, j)) for w in ws]
    return pl.pallas_call(
        _merge_kernel,
        out_shape=jax.ShapeDtypeStruct((T, D), BF16),
        grid=(D // tn, T // tm),
        in_specs=[pl.BlockSpec((tm, D), lambda j, i: (i, 0)),
                  pl.BlockSpec((N_BRANCHES, D, tn), lambda j, i: (0, 0, j))] + yspecs + wspecs,
        out_specs=pl.BlockSpec((tm, tn), lambda j, i: (i, j)),
        compiler_params=_cparams(("parallel", "parallel")),
        name="branch_merge",
    )(hb, wg, *ys, *ws)


def _layer_norm(z, g, b):
    mu = jnp.mean(z, -1, keepdims=True)
    d = z - mu
    var = jnp.mean(d * d, -1, keepdims=True)
    return d * lax.rsqrt(var + LN_EPS) * g + b


def _out_ln_router_kernel(m_ref, wo_ref, x_ref, gate_ref, g_ref, b_ref, sc_ref, sh_ref, rw_ref, rb_ref,
                          x1_ref, h2_ref, idx_ref, wt_ref):
    y = jnp.dot(m_ref[...], wo_ref[...], preferred_element_type=F32)
    x1 = _layer_norm(DEEPNORM_ALPHA * x_ref[...] + gate_ref[0] * y, g_ref[...], b_ref[...])
    x1_ref[...] = x1
    h2 = x1 * (1.0 + sc_ref[0]) + sh_ref[0]
    h2_ref[...] = h2.astype(h2_ref.dtype)
    logits = jnp.dot(h2, rw_ref[...], precision=HIGHEST, preferred_element_type=F32) + rb_ref[...]
    lane = lax.broadcasted_iota(jnp.int32, logits.shape, 1).astype(F32)
    vals, idxs = [], []
    cur = logits
    for _ in range(TOP_K):
        m = jnp.max(cur, -1, keepdims=True)
        ix = jnp.min(jnp.where(cur == m, lane, float(LANES)), -1, keepdims=True)
        vals.append(m)
        idxs.append(ix)
        cur = jnp.where(lane == ix, -jnp.inf, cur)
    es = [jnp.exp(v - vals[0]) for v in vals]
    tot = es[0] + es[1] + es[2] + es[3]
    idx_o = jnp.zeros_like(logits)
    wt_o = jnp.zeros_like(logits)
    for kq in range(TOP_K):
        idx_o = jnp.where(lane == kq, idxs[kq], idx_o)
        wt_o = jnp.where(lane == kq, es[kq] / tot, wt_o)
    idx_ref[...] = idx_o.astype(jnp.int32)
    wt_ref[...] = wt_o


def out_ln_router(merged, w_out_b, x2d, gate1, ln_g, ln_b, scale2, shift2, router_w, router_b, S):
    T, D = x2d.shape
    tm = 256
    spb = S // tm
    rw = jnp.zeros((D, LANES), F32).at[:, :N_EXPERTS].set(router_w)
    rb = jnp.full((1, LANES), NEG_BIG, F32).at[0, :N_EXPERTS].set(router_b)
    rowblk = lambda w: pl.BlockSpec((tm, w), lambda i: (i, 0))
    full = lambda r, c: pl.BlockSpec((r, c), lambda i: (0, 0))
    perb = pl.BlockSpec((1, 1, D), lambda i: (i // spb, 0, 0))
    return pl.pallas_call(
        _out_ln_router_kernel,
        out_shape=(jax.ShapeDtypeStruct((T, D), F32), jax.ShapeDtypeStruct((T, D), BF16),
                   jax.ShapeDtypeStruct((T, LANES), jnp.int32), jax.ShapeDtypeStruct((T, LANES), F32)),
        grid=(T // tm,),
        in_specs=[rowblk(D), full(D, D), rowblk(D), perb, full(1, D), full(1, D), perb, perb,
                  full(D, LANES), full(1, LANES)],
        out_specs=(rowblk(D), rowblk(D), rowblk(LANES), rowblk(LANES)),
        compiler_params=_cparams(("parallel",)),
        name="out_proj_ln_router",
    )(merged, w_out_b, x2d, gate1, ln_g.reshape(1, D), ln_b.reshape(1, D), scale2, shift2, rw, rb)


def _expert_kernel(be_ref, nu_ref, x_ref, wgu_ref, bgu_ref, wd_ref, bd_ref, o_ref):
    i = pl.program_id(0)

    @pl.when(i < nu_ref[0])
    def _():
        gu = jnp.dot(x_ref[...], wgu_ref[0], preferred_element_type=F32) + bgu_ref[0]
        glu = jnp.minimum(gu[:, :EXPERT_FF], SWIGLU_LIMIT)
        lin = jnp.clip(gu[:, EXPERT_FF:], -SWIGLU_LIMIT, SWIGLU_LIMIT)
        act = glu * jax.nn.sigmoid(SWIGLU_ALPHA * glu) * (lin + 1.0)
        o_ref[...] = jnp.dot(act.astype(BF16), wd_ref[0], preferred_element_type=F32) + bd_ref[0]

    @pl.when(i >= nu_ref[0])
    def _():
        o_ref[...] = jnp.zeros_like(o_ref)


def expert_ffn(block_e, n_used, xb, wgu, bgu, wd, bd):
    rows, D = xb.shape
    E, _, F2 = wgu.shape
    nblk = rows // MOE_BM
    grid_spec = pltpu.PrefetchScalarGridSpec(
        num_scalar_prefetch=2,
        grid=(nblk,),
        in_specs=[pl.BlockSpec((MOE_BM, D), lambda i, be, nu: (i, 0)),
                  pl.BlockSpec((1, D, F2), lambda i, be, nu: (be[i], 0, 0)),
                  pl.BlockSpec((1, 1, F2), lambda i, be, nu: (be[i], 0, 0)),
                  pl.BlockSpec((1, F2 // 2, D), lambda i, be, nu: (be[i], 0, 0)),
                  pl.BlockSpec((1, 1, D), lambda i, be, nu: (be[i], 0, 0))],
        out_specs=pl.BlockSpec((MOE_BM, D), lambda i, be, nu: (i, 0)),
    )
    return pl.pallas_call(
        _expert_kernel,
        out_shape=jax.ShapeDtypeStruct((rows, D), F32),
        grid_spec=grid_spec,
        compiler_params=_cparams(("arbitrary",)),
        name="expert_ffn",
    )(block_e, n_used, xb, wgu, bgu.reshape(E, 1, F2), wd, bd.reshape(E, 1, D))


def routed_ffn(h2, top_idx, top_w, wgu, bgu, wd, bd):
    T, D = h2.shape
    E = N_EXPERTS
    n_pairs = T * TOP_K
    flat_e = top_idx.reshape(-1)
    onehot = (flat_e[:, None] == jnp.arange(E, dtype=jnp.int32)[None, :]).astype(jnp.int32)
    csum = jnp.cumsum(onehot, axis=0)
    counts = csum[-1]
    rank = jnp.take_along_axis(csum, flat_e[:, None], axis=1)[:, 0] - 1
    padded = ((counts + MOE_BM - 1) // MOE_BM) * MOE_BM
    pend = jnp.cumsum(padded)
    pstart = pend - padded
    dest = pstart[flat_e] + rank
    nblk = n_pairs // MOE_BM + E
    rows = nblk * MOE_BM
    slot_tok = jnp.zeros((rows,), jnp.int32).at[dest].set(jnp.arange(n_pairs, dtype=jnp.int32) // TOP_K)
    blk_start = jnp.arange(nblk, dtype=jnp.int32) * MOE_BM
    block_e = jnp.minimum(jnp.sum((pend[None, :] <= blk_start[:, None]).astype(jnp.int32), axis=1), E - 1)
    n_used = (pend[-1] // MOE_BM).astype(jnp.int32).reshape(1)
    xb = h2[slot_tok]
    yb = expert_ffn(block_e, n_used, xb, wgu, bgu, wd, bd)
    picked = yb[dest].reshape(T, TOP_K, D)
    return jnp.sum(picked * top_w[:, :, None], axis=1)


def _ln2_kernel(x_ref, y_ref, gate_ref, g_ref, b_ref, o_ref):
    o_ref[...] = _layer_norm(DEEPNORM_ALPHA * x_ref[...] + gate_ref[0] * y_ref[...], g_ref[...], b_ref[...])


def ln2(x1, y, gate2, ln_g, ln_b, S):
    T, D = x1.shape
    tm = 512
    spb = S // tm
    rowblk = pl.BlockSpec((tm, D), lambda i: (i, 0))
    full = pl.BlockSpec((1, D), lambda i: (0, 0))
    return pl.pallas_call(
        _ln2_kernel,
        out_shape=jax.ShapeDtypeStruct((T, D), F32),
        grid=(T // tm,),
        in_specs=[rowblk, rowblk, pl.BlockSpec((1, 1, D), lambda i: (i // spb, 0, 0)), full, full],
        out_specs=rowblk,
        compiler_params=_cparams(("parallel",)),
        name="deepnorm_ln2",
    )(x1, y, gate2, ln_g.reshape(1, D), ln_b.reshape(1, D))


def _mix_weights(w_in_l):
    def cols(rng, pad=0):
        part = w_in_l[:, rng[0]:rng[1]]
        if pad:
            part = jnp.pad(part, ((0, 0), (0, pad)))
        return part
    w_mix = jnp.concatenate([
        cols(SRC_POOL), cols(SRC_GLA_V), cols(SRC_GLA_G), cols(SRC_RWKV_RKV), cols(SRC_ATT),
        cols(SRC_GLA_Q), cols(SRC_GLA_K),
        cols(SRC_RWKV_LO, RWKV_LO_PAD - (SRC_RWKV_LO[1] - SRC_RWKV_LO[0])),
        cols(SRC_GLA_A, GLA_A_PAD - (SRC_GLA_A[1] - SRC_GLA_A[0]))], axis=1).astype(BF16)
    D = w_in_l.shape[0]
    w_gates = w_in_l[:, SRC_GATES[0]:].reshape(D, N_BRANCHES, D).transpose(1, 0, 2).astype(BF16)
    return w_mix, w_gates


def kernel(x, c, ada_w, ada_b, w_in, pool_w, pool_scale, gla_w_alpha, gla_b_alpha, gla_norm_g, rwkv_mu, rwkv_w0, rwkv_w2, rwkv_a0, rwkv_a2, rwkv_g2, rwkv_k_k, rwkv_k_a, rwkv_r_k, rwkv_ln_g, rwkv_ln_b, w_branch_a, w_branch_b, w_branch_c, w_branch_d, w_out, ln1_g, ln1_b, router_w, router_b, w_gate_up, b_gate_up, w_down, b_down, ln2_g, ln2_b):
    B, S, D = x.shape
    T = B * S
    mod = ada_modulation(c, ada_w, ada_b)
    for l in range(DEPTH):
        shift1, scale1, gate1, shift2, scale2, gate2 = (
            mod[l, :, None, i * D:(i + 1) * D] for i in range(6))
        w_mix, w_gates = _mix_weights(w_in[l])
        hb = modulate(x, scale1, shift1)
        hb2d = hb.reshape(T, D)
        P = matmul(hb2d, w_mix, 1024, MIX_TN).reshape(B, S, MIX_WIDTH)
        y_a = pool_mixer(P, pool_w[l], pool_scale[l])
        y_b = dilated_attention(P)
        y_c = gla_mixer(P, gla_w_alpha[l], gla_b_alpha[l], gla_norm_g[l])
        r_, w_, k_, v_, kk_, b_, gate_, bonus_ = rwkv_prep(
            P, rwkv_mu[l], rwkv_w0[l], rwkv_w2[l], rwkv_a0[l], rwkv_a2[l], rwkv_g2[l],
            rwkv_k_k[l], rwkv_k_a[l], rwkv_r_k[l].reshape(-1))
        y_t = rwkv_chunked(r_, w_, k_, v_, kk_, b_)
        y_d = rwkv_post(y_t, gate_, bonus_, rwkv_ln_g[l], rwkv_ln_b[l])
        ys = [y.reshape(T, -1) for y in (y_a, y_b, y_c, y_d)]
        ws = [w[l].astype(BF16) for w in (w_branch_a, w_branch_b, w_branch_c, w_branch_d)]
        merged = branch_merge(hb2d, w_gates, ys, ws)
        x1, h2, idx, wt = out_ln_router(merged, w_out[l].astype(BF16), x.reshape(T, D), gate1,
                                        ln1_g[l], ln1_b[l], scale2, shift2, router_w[l], router_b[l], S)
        y_moe = routed_ffn(h2, idx[:, :TOP_K], wt[:, :TOP_K], w_gate_up[l].astype(BF16), b_gate_up[l],
                           w_down[l].astype(BF16), b_down[l])
        x = ln2(x1, y_moe, gate2, ln2_g[l], ln2_b[l], S).reshape(B, S, D)
    return x
```

```python
import functools

import numpy as np
import jax
import jax.numpy as jnp
from jax import lax
from jax.experimental import pallas as pl
from jax.experimental.pallas import tpu as pltpu

F32 = jnp.float32
BF16 = jnp.bfloat16
HIGHEST = lax.Precision.HIGHEST

D_MODEL = 2048
DEPTH = 2
POOL_WINDOWS = (2, 4, 8, 16)
POOL_GROUP = 128
POOL_WIDTH = 512
ATT_GROUPS = ((128, 1), (512, 4), (2048, 16))
ATT_HEAD_DIM = 64
ATT_HEADS = 12
ATT_WIDTH = 768
ATT_OUT_WIDTH = 256
ATT_BLOCK = 128
ALIBI_SLOPES = tuple(2.0 ** (-8.0 * (h + 1) / ATT_HEADS) for h in range(ATT_HEADS))
GLA_HEADS = 4
GLA_DK = 64
GLA_DV = 128
GLA_KEY_WIDTH = 256
GLA_VAL_WIDTH = 512
GLA_GATE_RANK = 16
GLA_GATE_TEMP = 16.0
GLA_CHUNK = 32
GLA_NORM_EPS = 1e-6
RWKV_HEADS = 8
RWKV_HEAD_DIM = 64
RWKV_WIDTH = 512
RWKV_DECAY_RANK = 32
RWKV_AAA_RANK = 32
RWKV_GATE_RANK = 96
RWKV_GN_EPS = 64e-5
N_BRANCHES = 4
N_EXPERTS = 32
TOP_K = 4
EXPERT_FF = 1024
SWIGLU_LIMIT = 7.0
SWIGLU_ALPHA = 1.702
LN_EPS = 1e-5
DEEPNORM_ALPHA = (2 * DEPTH) ** 0.25

LANES = 128
SUBLANES = 8
VMEM_LIMIT = 56 * 1024 * 1024

SRC_POOL = (0, 512)
SRC_ATT = (512, 2816)
SRC_GLA_Q = (2816, 3072)
SRC_GLA_K = (3072, 3328)
SRC_GLA_V = (3328, 3840)
SRC_GLA_G = (3840, 4352)
SRC_GLA_A = (4352, 4368)
SRC_RWKV_RKV = (4368, 5904)
SRC_RWKV_LO = (5904, 6064)
SRC_GATES = (6064, 14256)
OFF_POOL = 0
OFF_GLA_V = 512
OFF_GLA_G = 1024
OFF_RWKV_RKV = 1536
OFF_ATT_Q = 3072
OFF_ATT_K = OFF_ATT_Q + ATT_WIDTH
OFF_ATT_V = OFF_ATT_K + ATT_WIDTH
OFF_GLA_Q = 5376
OFF_GLA_K = 5632
OFF_RWKV_LO = 5888
RWKV_LO_PAD = 256
OFF_GLA_A = 6144
GLA_A_PAD = 128
MIX_WIDTH = 6272
MIX_TN = 896

MOE_BM = 256
NEG_BIG = -1e30


def _cparams(sem):
    return pltpu.CompilerParams(dimension_semantics=sem, vmem_limit_bytes=VMEM_LIMIT)


def _ada_kernel(c_ref, w_ref, b_ref, o_ref):
    c = c_ref[...]
    s = c * jax.nn.sigmoid(c)
    o_ref[0] = jnp.dot(s, w_ref[0], precision=HIGHEST, preferred_element_type=F32) + b_ref[0]


def ada_modulation(c, ada_w, ada_b):
    L, D, N = ada_w.shape
    B = c.shape[0]
    cp = jnp.zeros((SUBLANES, D), F32).at[:B].set(c)
    tn = 1024
    out = pl.pallas_call(
        _ada_kernel,
        out_shape=jax.ShapeDtypeStruct((L, SUBLANES, N), F32),
        grid=(L, N // tn),
        in_specs=[pl.BlockSpec((SUBLANES, D), lambda l, j: (0, 0)),
                  pl.BlockSpec((1, D, tn), lambda l, j: (l, 0, j)),
                  pl.BlockSpec((1, 1, tn), lambda l, j: (l, 0, j))],
        out_specs=pl.BlockSpec((1, SUBLANES, tn), lambda l, j: (l, 0, j)),
        compiler_params=_cparams(("parallel", "parallel")),
        name="ada_modulation",
    )(cp, ada_w, ada_b.reshape(L, 1, N))
    return out[:, :B]


def _modulate_kernel(x_ref, sc_ref, sh_ref, o_ref):
    o_ref[0] = (x_ref[0] * (1.0 + sc_ref[0]) + sh_ref[0]).astype(o_ref.dtype)


def modulate(x, scale, shift):
    B, S, D = x.shape
    ts = 1024
    return pl.pallas_call(
        _modulate_kernel,
        out_shape=jax.ShapeDtypeStruct((B, S, D), BF16),
        grid=(B, S // ts),
        in_specs=[pl.BlockSpec((1, ts, D), lambda b, i: (b, i, 0)),
                  pl.BlockSpec((1, 1, D), lambda b, i: (b, 0, 0)),
                  pl.BlockSpec((1, 1, D), lambda b, i: (b, 0, 0))],
        out_specs=pl.BlockSpec((1, ts, D), lambda b, i: (b, i, 0)),
        compiler_params=_cparams(("parallel", "parallel")),
        name="modulate",
    )(x, scale, shift)


def _mm_kernel(x_ref, w_ref, o_ref):
    o_ref[...] = jnp.dot(x_ref[...], w_ref[...], preferred_element_type=F32).astype(o_ref.dtype)


def matmul(x, w, tm, tn, out_dtype=F32):
    M, K = x.shape
    N = w.shape[1]
    return pl.pallas_call(
        _mm_kernel,
        out_shape=jax.ShapeDtypeStruct((M, N), out_dtype),
        grid=(N // tn, M // tm),
        in_specs=[pl.BlockSpec((tm, K), lambda j, i: (i, 0)),
                  pl.BlockSpec((K, tn), lambda j, i: (0, j))],
        out_specs=pl.BlockSpec((tm, tn), lambda j, i: (i, j)),
        compiler_params=_cparams(("parallel", "parallel")),
        name="matmul",
    )(x, w)


def _pool_kernel(p_ref, w_ref, sc_ref, o_ref):
    g = pl.program_id(1)
    v = p_ref[0]
    S = v.shape[0]
    row = lax.broadcasted_iota(jnp.int32, v.shape, 0)
    win = jnp.left_shift(2, g)
    s = v
    pooled_sum = v
    for k, sh in enumerate((1, 2, 4, 8)):
        s = s + jnp.where(row >= sh, pltpu.roll(s, sh, 0), 0.0)
        pooled_sum = jnp.where(g >= k, s, pooled_sum)
    cnt = jnp.minimum(row + 1, win).astype(F32)
    diff = pooled_sum / cnt - v
    y = jnp.dot(diff.astype(BF16), w_ref[0].astype(BF16), preferred_element_type=F32)
    o_ref[0] = y * sc_ref[...]


def pool_mixer(P, pool_w, pool_scale):
    B, S, _ = P.shape
    G = len(POOL_WINDOWS)
    return pl.pallas_call(
        _pool_kernel,
        out_shape=jax.ShapeDtypeStruct((B, S, POOL_WIDTH), F32),
        grid=(B, G),
        in_specs=[pl.BlockSpec((1, S, POOL_GROUP), lambda b, g: (b, 0, OFF_POOL // POOL_GROUP + g)),
                  pl.BlockSpec((1, POOL_GROUP, POOL_GROUP), lambda b, g: (g, 0, 0)),
                  pl.BlockSpec((1, POOL_GROUP), lambda b, g: (0, g))],
        out_specs=pl.BlockSpec((1, S, POOL_GROUP), lambda b, g: (b, 0, g)),
        compiler_params=_cparams(("parallel", "parallel")),
        name="pool_mixer",
    )(P, pool_w, pool_scale.reshape(1, POOL_WIDTH))


def _att_kernel(q_ref, kc_ref, kp_ref, vc_ref, vp_ref, num_ref, den_ref, m_ref, *, slopes, dilation):
    hp = pl.program_id(1)
    n = pl.program_id(2)
    i = lax.broadcasted_iota(jnp.int32, (ATT_BLOCK, ATT_BLOCK), 0)
    j = lax.broadcasted_iota(jnp.int32, (ATT_BLOCK, ATT_BLOCK), 1)
    dist_c = i - j
    dist_p = dist_c + ATT_BLOCK
    valid_c = dist_c >= 0
    valid_p = jnp.logical_and(dist_p <= ATT_BLOCK, n > 0)
    lane_lo = lax.broadcasted_iota(jnp.int32, (ATT_BLOCK, LANES), 1) < ATT_HEAD_DIM
    dn = (((1,), (1,)), ((), ()))

    def residue(r, carry):
        rows = slice(None) if dilation == 1 else pl.ds(r, ATT_BLOCK, stride=dilation)
        q2, kc2, kp2, vc2, vp2 = (ref[0, rows, :] for ref in (q_ref, kc_ref, kp_ref, vc_ref, vp_ref))
        nums, dens, ms = [], [], []
        for h in range(2):
            sl = slice(h * ATT_HEAD_DIM, (h + 1) * ATT_HEAD_DIM)
            slope = jnp.where(hp == 0, slopes[h], slopes[2 + h]) * float(dilation)
            q = (q2[:, sl] * (ATT_HEAD_DIM ** -0.5)).astype(BF16)
            s_c = lax.dot_general(q, kc2[:, sl].astype(BF16), dn, preferred_element_type=F32)
            s_p = lax.dot_general(q, kp2[:, sl].astype(BF16), dn, preferred_element_type=F32)
            s_c = jnp.where(valid_c, s_c - slope * dist_c.astype(F32), NEG_BIG)
            s_p = jnp.where(valid_p, s_p - slope * dist_p.astype(F32), NEG_BIG)
            m = jnp.maximum(jnp.max(s_c, -1, keepdims=True), jnp.max(s_p, -1, keepdims=True))
            p_c = jnp.exp(s_c - m)
            p_p = jnp.exp(s_p - m)
            dens.append(jnp.sum(p_c, -1, keepdims=True) + jnp.sum(p_p, -1, keepdims=True))
            nums.append(jnp.dot(p_c.astype(BF16), vc2[:, sl].astype(BF16), preferred_element_type=F32)
                        + jnp.dot(p_p.astype(BF16), vp2[:, sl].astype(BF16), preferred_element_type=F32))
            ms.append(m)
        num_ref[0, rows, :] = jnp.concatenate(nums, axis=-1)
        den_ref[0, rows, :] = jnp.where(lane_lo, dens[0], dens[1])
        m_ref[0, rows, :] = jnp.where(lane_lo, ms[0], ms[1])
        return carry

    if dilation == 1:
        residue(0, 0)
    else:
        lax.fori_loop(0, dilation, residue, 0)


def att_group(P, g, dilation):
    B, S, NP = P.shape
    rows = ATT_BLOCK * dilation
    nb = S // rows
    qo, ko, vo = (o // LANES + g * 2 for o in (OFF_ATT_Q, OFF_ATT_K, OFF_ATT_V))
    blk = (1, rows, LANES)
    cur = lambda off: pl.BlockSpec(blk, lambda b, hp, n: (b, n, off + hp))
    prev = lambda off: pl.BlockSpec(blk, lambda b, hp, n: (b, jnp.maximum(n - 1, 0), off + hp))
    ospec = pl.BlockSpec(blk, lambda b, hp, n: (b, n, hp))
    oshape = jax.ShapeDtypeStruct((B, S, ATT_OUT_WIDTH), F32)
    slopes = ALIBI_SLOPES[g * 4:(g + 1) * 4]
    return pl.pallas_call(
        functools.partial(_att_kernel, slopes=slopes, dilation=dilation),
        out_shape=(oshape, oshape, oshape),
        grid=(B, 2, nb),
        in_specs=[cur(qo), cur(ko), prev(ko), cur(vo), prev(vo)],
        out_specs=(ospec, ospec, ospec),
        compiler_params=_cparams(("parallel", "parallel", "arbitrary")),
        name=f"dilated_attention_g{g}",
    )(P, P, P, P, P)


def _att_merge_kernel(*refs):
    o_ref = refs[-1]
    nums, dens, ms = refs[0:3], refs[3:6], refs[6:9]
    mx = jnp.maximum(jnp.maximum(ms[0][0], ms[1][0]), ms[2][0])
    num = jnp.zeros_like(mx)
    den = jnp.zeros_like(mx)
    for g in range(3):
        e = jnp.exp(ms[g][0] - mx)
        num = num + nums[g][0] * e
        den = den + dens[g][0] * e
    o_ref[0] = num / den


def dilated_attention(P):
    B, S, _ = P.shape
    parts = [att_group(P, g, d) for g, (_, d) in enumerate(ATT_GROUPS)]
    args = [p[0] for p in parts] + [p[1] for p in parts] + [p[2] for p in parts]
    ts = 1024
    spec = pl.BlockSpec((1, ts, ATT_OUT_WIDTH), lambda b, i: (b, i, 0))
    return pl.pallas_call(
        _att_merge_kernel,
        out_shape=jax.ShapeDtypeStruct((B, S, ATT_OUT_WIDTH), F32),
        grid=(B, S // ts),
        in_specs=[spec] * 9,
        out_specs=spec,
        compiler_params=_cparams(("parallel", "parallel")),
        name="dilated_attention_merge",
    )(*args)


GLA_TS = 256


def _gla_kernel(q_ref, k_ref, v_ref, g_ref, a_ref, wa_ref, ba_ref, ng_ref, o_ref, st_ref):
    @pl.when(pl.program_id(1) == 0)
    def _():
        st_ref[...] = jnp.zeros_like(st_ref)

    TS = GLA_TS
    nchunk = TS // GLA_CHUNK
    logit = jnp.dot(a_ref[0], wa_ref[...], precision=HIGHEST, preferred_element_type=F32) + ba_ref[...]
    log_a = (jnp.minimum(logit, 0.0) - jnp.log1p(jnp.exp(-jnp.abs(logit)))) / GLA_GATE_TEMP
    i = lax.broadcasted_iota(jnp.int32, (TS, TS), 0)
    j = lax.broadcasted_iota(jnp.int32, (TS, TS), 1)
    same_chunk_causal = jnp.logical_and(i // GLA_CHUNK == j // GLA_CHUNK, j <= i)
    tri = same_chunk_causal.astype(F32)
    bcum = jnp.dot(tri, log_a, precision=HIGHEST, preferred_element_type=F32)
    eb = jnp.exp(bcum)
    qd = q_ref[0] * (GLA_DK ** -0.5) * eb
    kd = k_ref[0] * jnp.exp(-bcum)
    k_all = k_ref[0]
    v_all = v_ref[0]
    dn_nt = (((1,), (1,)), ((), ()))
    dn_tn = (((0,), (0,)), ((), ()))
    for h in range(GLA_HEADS):
        ks = slice(h * GLA_DK, (h + 1) * GLA_DK)
        vs = slice(h * GLA_DV, (h + 1) * GLA_DV)
        qh = qd[:, ks].astype(BF16)
        vh = v_all[:, vs].astype(BF16)
        scores = lax.dot_general(qh, kd[:, ks].astype(BF16), dn_nt, preferred_element_type=F32)
        scores = jnp.where(same_chunk_causal, scores, 0.0)
        o_intra = jnp.dot(scores.astype(BF16), vh, preferred_element_type=F32)
        st = st_ref[h]
        outs = []
        for c in range(nchunk):
            rs = slice(c * GLA_CHUNK, (c + 1) * GLA_CHUNK)
            last = (c + 1) * GLA_CHUNK - 1
            o_c = o_intra[rs] + lax.dot_general(qh[rs], st.astype(BF16), dn_nt, preferred_element_type=F32)
            outs.append(o_c)
            b_last = bcum[last:last + 1, ks]
            k_tail = (k_all[rs, ks] * jnp.exp(b_last - bcum[rs, ks])).astype(BF16)
            st = st * jnp.exp(b_last) + lax.dot_general(vh[rs], k_tail, dn_tn, preferred_element_type=F32)
        st_ref[h] = st
        o = jnp.concatenate(outs, axis=0)
        o = o * lax.rsqrt(jnp.mean(o * o, -1, keepdims=True) + GLA_NORM_EPS)
        gg = g_ref[0][:, vs]
        o_ref[0, :, vs] = o * ng_ref[:, vs] * (gg * jax.nn.sigmoid(gg))


def gla_mixer(P, w_alpha, b_alpha, norm_g):
    B, S, _ = P.shape
    TS = GLA_TS
    wa = jnp.zeros((GLA_A_PAD, GLA_KEY_WIDTH), F32).at[:GLA_GATE_RANK].set(w_alpha)
    col = lambda off, w: pl.BlockSpec((1, TS, w), lambda b, i: (b, i, off // w))
    full = lambda r, c: pl.BlockSpec((r, c), lambda b, i: (0, 0))
    return pl.pallas_call(
        _gla_kernel,
        out_shape=jax.ShapeDtypeStruct((B, S, GLA_VAL_WIDTH), F32),
        grid=(B, S // TS),
        in_specs=[col(OFF_GLA_Q, GLA_KEY_WIDTH), col(OFF_GLA_K, GLA_KEY_WIDTH),
                  col(OFF_GLA_V, GLA_VAL_WIDTH), col(OFF_GLA_G, GLA_VAL_WIDTH),
                  col(OFF_GLA_A, GLA_A_PAD),
                  full(GLA_A_PAD, GLA_KEY_WIDTH), full(1, GLA_KEY_WIDTH), full(1, GLA_VAL_WIDTH)],
        out_specs=pl.BlockSpec((1, TS, GLA_VAL_WIDTH), lambda b, i: (b, i, 0)),
        scratch_shapes=[pltpu.VMEM((GLA_HEADS, GLA_DV, GLA_DK), F32)],
        compiler_params=_cparams(("parallel", "arbitrary")),
        name="gla_mixer",
    )(P, P, P, P, P, wa, b_alpha.reshape(1, -1), norm_g.reshape(1, -1))


RWKV_PREP_TS = 512
RWKV_SHIFT_LO = RWKV_DECAY_RANK + RWKV_AAA_RANK + RWKV_GATE_RANK


def _segment_ones(n, seg):
    i = lax.broadcasted_iota(jnp.int32, (n, n), 0)
    j = lax.broadcasted_iota(jnp.int32, (n, n), 1)
    return (i // seg == j // seg).astype(F32)


def _rwkv_prep_kernel(rkv_ref, lo_ref, rkv_prev_ref, lo_prev_ref, mu_rkv_ref, mu_lo_ref, w0_ref, w2_ref,
                      a0_ref, a2_ref, g2_ref, kk_ref, ka_ref, rk_ref,
                      r_out, w_out, k_out, v_out, kk_out, b_out, gate_out, bonus_out):
    first = pl.program_id(1) == 0

    def shifted(cur, prev_ref, mu):
        prev_row = jnp.where(first, 0.0, prev_ref[0][SUBLANES - 1:SUBLANES, :])
        row = lax.broadcasted_iota(jnp.int32, cur.shape, 0)
        prev = jnp.where(row == 0, prev_row, pltpu.roll(cur, 1, 0))
        return cur + (prev - cur) * mu

    xs = shifted(rkv_ref[0], rkv_prev_ref, mu_rkv_ref[...])
    lo = shifted(lo_ref[0], lo_prev_ref, mu_lo_ref[...])
    W = RWKV_WIDTH
    r, k, v = xs[:, 0:W], xs[:, W:2 * W], xs[:, 2 * W:3 * W]
    hdot = functools.partial(jnp.dot, precision=HIGHEST, preferred_element_type=F32)
    z = w0_ref[...] + hdot(jnp.tanh(lo), w2_ref[...])
    log_decay = -jax.nn.sigmoid(z) * float(np.exp(-0.5))
    a = jax.nn.sigmoid(a0_ref[...] + hdot(lo, a2_ref[...]))
    gate = hdot(jax.nn.sigmoid(lo), g2_ref[...])
    seg = _segment_ones(W, RWKV_HEAD_DIM)
    kk = k * kk_ref[...]
    kk_norm = jnp.sqrt(hdot(kk * kk, seg))
    kk = kk / jnp.maximum(kk_norm, 1e-12)
    k2 = k * (1.0 + (a - 1.0) * ka_ref[...])
    bonus = hdot(r * k2 * rk_ref[...], seg) * v
    r_out[0] = r
    w_out[0] = log_decay
    k_out[0] = k2
    v_out[0] = v
    kk_out[0] = kk
    b_out[0] = kk * a
    gate_out[0] = gate
    bonus_out[0] = bonus


def rwkv_prep(P, mu, w0, w2, a0, a2, g2, k_k, k_a, r_k):
    B, S, _ = P.shape
    TS = RWKV_PREP_TS
    W = RWKV_WIDTH
    W3 = 3 * W
    n_rkv = SRC_RWKV_RKV[1] - SRC_RWKV_RKV[0]
    mu_rkv = mu[:n_rkv].reshape(1, W3)
    mu_lo = jnp.zeros((1, RWKV_LO_PAD), F32).at[0, :RWKV_SHIFT_LO].set(mu[n_rkv:])
    w2p = jnp.zeros((RWKV_LO_PAD, W), F32).at[0:RWKV_DECAY_RANK].set(w2)
    a2p = jnp.zeros((RWKV_LO_PAD, W), F32).at[RWKV_DECAY_RANK:RWKV_DECAY_RANK + RWKV_AAA_RANK].set(a2)
    g2p = jnp.zeros((RWKV_LO_PAD, W), F32).at[RWKV_DECAY_RANK + RWKV_AAA_RANK:RWKV_SHIFT_LO].set(g2)
    row = lambda a: a.reshape(1, -1)
    full = lambda r, c: pl.BlockSpec((r, c), lambda b, i: (0, 0))
    tpb = TS // SUBLANES
    prev_map = lambda off, w: pl.BlockSpec((1, SUBLANES, w), lambda b, i: (b, jnp.maximum(i * tpb - 1, 0), off // w))
    oshape = jax.ShapeDtypeStruct((B, S, W), F32)
    ospec = pl.BlockSpec((1, TS, W), lambda b, i: (b, i, 0))
    return pl.pallas_call(
        _rwkv_prep_kernel,
        out_shape=(oshape,) * 8,
        grid=(B, S // TS),
        in_specs=[pl.BlockSpec((1, TS, W3), lambda b, i: (b, i, OFF_RWKV_RKV // W3)),
                  pl.BlockSpec((1, TS, RWKV_LO_PAD), lambda b, i: (b, i, OFF_RWKV_LO // RWKV_LO_PAD)),
                  prev_map(OFF_RWKV_RKV, W3), prev_map(OFF_RWKV_LO, RWKV_LO_PAD),
                  full(1, W3), full(1, RWKV_LO_PAD), full(1, W), full(RWKV_LO_PAD, W),
                  full(1, W), full(RWKV_LO_PAD, W), full(RWKV_LO_PAD, W),
                  full(1, W), full(1, W), full(1, W)],
        out_specs=(ospec,) * 8,
        compiler_params=_cparams(("parallel", "arbitrary")),
        name="rwkv7_prep",
    )(P, P, P, P, mu_rkv, mu_lo, row(w0), w2p, row(a0), a2p, g2p, row(k_k), row(k_a), row(r_k))


RWKV_CHUNK = 64
RWKV_CHUNK_TT = 512


def _rwkv_chunk_kernel(r_ref, lw_ref, k_ref, v_ref, kk_ref, b_ref, y_ref, h_ref):
    @pl.when(pl.program_id(2) == 0)
    def _():
        h_ref[...] = jnp.zeros_like(h_ref)

    C = RWKV_CHUNK
    TT = RWKV_CHUNK_TT
    N = RWKV_HEAD_DIM
    R2 = 2 * C
    bdot = lambda x, y: jnp.dot(x.astype(BF16), y.astype(BF16), preferred_element_type=F32)
    r, lw, k, v, kk, b = (ref[0] for ref in (r_ref, lw_ref, k_ref, v_ref, kk_ref, b_ref))
    i = lax.broadcasted_iota(jnp.int32, (TT, TT), 0)
    j = lax.broadcasted_iota(jnp.int32, (TT, TT), 1)
    tri = jnp.logical_and(i // C == j // C, j <= i).astype(F32)
    cum = jnp.dot(tri, lw, precision=HIGHEST, preferred_element_type=F32)
    lane_lo = lax.broadcasted_iota(jnp.int32, (C, LANES), 1) < N

    def stack(x):
        return jnp.concatenate([jnp.where(lane_lo, x, 0.0), jnp.where(lane_lo, 0.0, x)], axis=0)

    ti = lax.broadcasted_iota(jnp.int32, (R2, R2), 0)
    si = lax.broadcasted_iota(jnp.int32, (R2, R2), 1)
    strict = ti > si
    incl = ti >= si
    eye = (ti == si).astype(F32)
    blk16 = ti // 16 == si // 16
    off32 = jnp.logical_and(ti // 32 == si // 32, jnp.logical_not(blk16))
    off64 = jnp.logical_and(ti // 64 == si // 64, ti // 32 != si // 32)
    ones_c = jnp.ones((C, LANES), F32)

    chunks = range(TT // C)
    each = lambda f, *lists: [f(*args) for args in zip(*lists)]
    rows = [slice(c * C, (c + 1) * C) for c in chunks]
    cu = [cum[rw] for rw in rows]
    cu_last = [cum[(c + 1) * C - 1:(c + 1) * C] for c in chunks]
    e_neg = each(lambda x: jnp.exp(-x), cu)
    e_tail = each(lambda x, xl: jnp.exp(xl - x), cu, cu_last)
    a_t = each(lambda rw, x: stack(-kk[rw] * jnp.exp(x - lw[rw])), rows, cu)
    r_t = each(lambda rw, x: stack(r[rw] * jnp.exp(x)), rows, cu)
    b_t = each(lambda rw, e: stack(b[rw] * e), rows, e_neg)
    k_t = each(lambda rw, e: stack(k[rw] * e), rows, e_neg)
    v_s = each(lambda rw: stack(v[rw]), rows)
    gram = each(lambda a_, r_, b_, k_: lax.dot_general(
        jnp.concatenate([a_, r_], 0).astype(BF16), jnp.concatenate([b_, k_], 0).astype(BF16),
        (((1,), (1,)), ((), ())), preferred_element_type=F32), a_t, r_t, b_t, k_t)
    n_ab = each(lambda g: jnp.where(strict, g[:R2, :R2], 0.0), gram)
    a_ak = each(lambda g: jnp.where(strict, g[:R2, R2:], 0.0), gram)
    a_rb = each(lambda g: jnp.where(incl, g[R2:, :R2], 0.0), gram)
    a_rk = each(lambda g: jnp.where(incl, g[R2:, R2:], 0.0), gram)
    av = each(lambda ak, rk, vv: bdot(jnp.concatenate([ak, rk], 0), vv), a_ak, a_rk, v_s)
    n1 = each(lambda n: jnp.where(blk16, n, 0.0), n_ab)
    n2 = each(lambda x: bdot(x, x), n1)
    n4 = each(lambda x: bdot(x, x), n2)
    n8 = each(lambda x: bdot(x, x), n4)
    xa = each(lambda x1, x2: x1 + x2 + bdot(x1, x2), n1, n2)
    xb = each(lambda x4, x8: x4 + x8 + bdot(x4, x8), n4, n8)
    t_inv = each(lambda p, q: eye + p + q + bdot(p, q), xa, xb)
    for off in (off32, off64):
        mid = each(lambda n, t: bdot(jnp.where(off, n, 0.0), t), n_ab, t_inv)
        t_inv = each(lambda t, m_: t + bdot(t, m_), t_inv, mid)
    tw = each(lambda t, a_, av_: bdot(t, jnp.concatenate([a_, av_[:R2]], axis=1)), t_inv, a_t, av)
    bk_t = each(lambda rw, e: jnp.transpose(jnp.concatenate([stack(b[rw] * e), stack(k[rw] * e)], 0)),
                rows, e_tail)
    pc_col = each(lambda rw: jnp.exp(jnp.dot(jnp.transpose(lw[rw]), ones_c, precision=HIGHEST,
                                             preferred_element_type=F32)), rows)
    w1r = each(lambda t, r_: jnp.concatenate([t[:, :LANES], r_], 0), tw, r_t)

    h = h_ref[...]
    for c in chunks:
        x = bdot(w1r[c], h)
        u = x[:R2] + tw[c][:, LANES:]
        y_bd = x[R2:] + bdot(a_rb[c], u) + av[c][R2:]
        y_ref[0, rows[c], :] = y_bd[:C] + y_bd[C:]
        h = pc_col[c] * h + bdot(bk_t[c], jnp.concatenate([u, v_s[c]], 0))
    h_ref[...] = h


def rwkv_chunked(r, lw, k, v, kk, b):
    B, S, W = r.shape
    TT = RWKV_CHUNK_TT
    spec = pl.BlockSpec((1, TT, LANES), lambda bb, p, i: (bb, i, p))
    return pl.pallas_call(
        _rwkv_chunk_kernel,
        out_shape=jax.ShapeDtypeStruct((B, S, W), F32),
        grid=(B, W // LANES, S // TT),
        in_specs=[spec] * 6,
        out_specs=spec,
        scratch_shapes=[pltpu.VMEM((LANES, LANES), F32)],
        compiler_params=_cparams(("parallel", "parallel", "arbitrary")),
        name="rwkv7_chunked",
    )(r, lw, k, v, kk, b)


def _rwkv_post_kernel(y_ref, gate_ref, bonus_ref, g_ref, b_ref, o_ref):
    seg = _segment_ones(RWKV_WIDTH, RWKV_HEAD_DIM) * (1.0 / RWKV_HEAD_DIM)
    hdot = functools.partial(jnp.dot, precision=HIGHEST, preferred_element_type=F32)
    y = y_ref[0]
    mean = hdot(y, seg)
    d = y - mean
    var = hdot(d * d, seg)
    yn = d * lax.rsqrt(var + RWKV_GN_EPS) * g_ref[...] + b_ref[...]
    o_ref[0] = (yn + bonus_ref[0]) * gate_ref[0]


def rwkv_post(y, gate, bonus, ln_g, ln_b):
    B, S, W = y.shape
    ts = 512
    spec = pl.BlockSpec((1, ts, W), lambda b, i: (b, i, 0))
    full = pl.BlockSpec((1, W), lambda b, i: (0, 0))
    return pl.pallas_call(
        _rwkv_post_kernel,
        out_shape=jax.ShapeDtypeStruct((B, S, W), F32),
        grid=(B, S // ts),
        in_specs=[spec, spec, spec, full, full],
        out_specs=spec,
        compiler_params=_cparams(("parallel", "parallel")),
        name="rwkv7_post",
    )(y, gate, bonus, ln_g.reshape(1, W), ln_b.reshape(1, W))


def _merge_kernel(h_ref, wg0_ref, wg1_ref, wg2_ref, wg3_ref, ya_ref, yb_ref, yc_ref, yd_ref,
                  wa_ref, wb_ref, wc_ref, wd_ref, o_ref):
    h = h_ref[...]
    acc = None
    branches = ((wg0_ref, ya_ref, wa_ref), (wg1_ref, yb_ref, wb_ref), (wg2_ref, yc_ref, wc_ref),
                (wg3_ref, yd_ref, wd_ref))
    for wg_ref, y_ref, w_ref in branches:
        gate = jax.nn.sigmoid(jnp.dot(h, wg_ref[...], preferred_element_type=F32))
        proj = jnp.dot(y_ref[...].astype(BF16), w_ref[...], preferred_element_type=F32)
        acc = gate * proj if acc is None else acc + gate * proj
    o_ref[...] = acc.astype(o_ref.dtype)


def branch_merge(hb, wg, ys, ws):
    T, D = hb.shape
    tm, tn = 512, 512
    nj = D // tn
    gspecs = [pl.BlockSpec((D, tn), functools.partial(lambda j, i, br: (0, br * nj + j), br=br))
              for br in range(N_BRANCHES)]
    yspecs = [pl.BlockSpec((tm, y.shape[1]), lambda j, i: (i, 0)) for y in ys]
    wspecs = [pl.BlockSpec((w.shape[0], tn), lambda j, i: (0, j)) for w in ws]
    return pl.pallas_call(
        _merge_kernel,
        out_shape=jax.ShapeDtypeStruct((T, D), BF16),
        grid=(nj, T // tm),
        in_specs=[pl.BlockSpec((tm, D), lambda j, i: (i, 0))] + gspecs + yspecs + wspecs,
        out_specs=pl.BlockSpec((tm, tn), lambda j, i: (i, j)),
        compiler_params=_cparams(("parallel", "parallel")),
        name="branch_merge",
    )(hb, wg, wg, wg, wg, *ys, *ws)


ROUTE_TM = 256


def _layer_norm(z, g, b):
    mu = jnp.mean(z, -1, keepdims=True)
    d = z - mu
    var = jnp.mean(d * d, -1, keepdims=True)
    return d * lax.rsqrt(var + LN_EPS) * g + b


def _split_bf16(x):
    hi = x.astype(BF16)
    return hi, (x - hi.astype(F32)).astype(BF16)


def _out_ln_router_kernel(m_ref, wo_ref, x_ref, gate_ref, g_ref, b_ref, sc_ref, sh_ref, rwh_ref, rwl_ref, rb_ref,
                          x1_ref, h2_ref, idx_ref, wt_ref, rank_ref, base_ref, cnt_ref, run_ref):
    @pl.when(pl.program_id(0) == 0)
    def _():
        run_ref[...] = jnp.zeros_like(run_ref)

    y = jnp.dot(m_ref[...], wo_ref[...], preferred_element_type=F32)
    x1 = _layer_norm(DEEPNORM_ALPHA * x_ref[...] + gate_ref[0] * y, g_ref[...], b_ref[...])
    x1_ref[...] = x1
    h2 = x1 * (1.0 + sc_ref[0]) + sh_ref[0]
    h2_hi, h2_lo = _split_bf16(h2)
    h2_ref[...] = h2_hi
    logits = (jnp.dot(h2_hi, rwh_ref[...], preferred_element_type=F32)
              + jnp.dot(h2_hi, rwl_ref[...], preferred_element_type=F32)
              + jnp.dot(h2_lo, rwh_ref[...], preferred_element_type=F32)) + rb_ref[...]
    lane = lax.broadcasted_iota(jnp.int32, logits.shape, 1).astype(F32)
    vals, idxs = [], []
    cur = logits
    for _ in range(TOP_K):
        m = jnp.max(cur, -1, keepdims=True)
        ix = jnp.min(jnp.where(cur == m, lane, float(LANES)), -1, keepdims=True)
        vals.append(m)
        idxs.append(ix)
        cur = jnp.where(lane == ix, -jnp.inf, cur)
    es = [jnp.exp(v - vals[0]) for v in vals]
    tot = es[0] + es[1] + es[2] + es[3]
    sel = jnp.zeros_like(logits)
    for kq in range(TOP_K):
        sel = jnp.where(lane == idxs[kq], 1.0, sel)
    tm = logits.shape[0]
    ti = lax.broadcasted_iota(jnp.int32, (tm, tm), 0)
    si = lax.broadcasted_iota(jnp.int32, (tm, tm), 1)
    before = jnp.dot((si < ti).astype(BF16), sel.astype(BF16), preferred_element_type=F32)
    run = run_ref[...]
    pos = before + run
    idx_o = jnp.zeros_like(logits)
    wt_o = jnp.zeros_like(logits)
    rank_o = jnp.zeros_like(logits)
    for kq in range(TOP_K):
        rank_k = jnp.sum(jnp.where(lane == idxs[kq], pos, 0.0), -1, keepdims=True)
        idx_o = jnp.where(lane == kq, idxs[kq], idx_o)
        wt_o = jnp.where(lane == kq, es[kq] / tot, wt_o)
        rank_o = jnp.where(lane == kq, rank_k, rank_o)
    idx_ref[...] = idx_o.astype(jnp.int32)
    wt_ref[...] = wt_o
    rank_ref[...] = rank_o.astype(jnp.int32)
    cnt = jnp.sum(sel, axis=0, keepdims=True)
    base_ref[0] = jnp.broadcast_to(run, (SUBLANES, LANES)).astype(jnp.int32)
    cnt_ref[0] = jnp.broadcast_to(cnt, (SUBLANES, LANES)).astype(jnp.int32)
    run_ref[...] = run + cnt


def out_ln_router(merged, w_out_b, x2d, gate1, ln_g, ln_b, scale2, shift2, router_w, router_b, S):
    T, D = x2d.shape
    tm = ROUTE_TM
    nt = T // tm
    spb = S // tm
    rw = jnp.zeros((D, LANES), F32).at[:, :N_EXPERTS].set(router_w)
    rw_hi, rw_lo = _split_bf16(rw)
    rb = jnp.full((1, LANES), NEG_BIG, F32).at[0, :N_EXPERTS].set(router_b)
    rowblk = lambda w: pl.BlockSpec((tm, w), lambda i: (i, 0))
    full = lambda r, c: pl.BlockSpec((r, c), lambda i: (0, 0))
    perb = pl.BlockSpec((1, 1, D), lambda i: (i // spb, 0, 0))
    tile_spec = pl.BlockSpec((1, SUBLANES, LANES), lambda i: (i, 0, 0))
    tile_shape = jax.ShapeDtypeStruct((nt, SUBLANES, LANES), jnp.int32)
    return pl.pallas_call(
        _out_ln_router_kernel,
        out_shape=(jax.ShapeDtypeStruct((T, D), F32), jax.ShapeDtypeStruct((T, D), BF16),
                   jax.ShapeDtypeStruct((T, LANES), jnp.int32), jax.ShapeDtypeStruct((T, LANES), F32),
                   jax.ShapeDtypeStruct((T, LANES), jnp.int32), tile_shape, tile_shape),
        grid=(nt,),
        in_specs=[rowblk(D), full(D, D), rowblk(D), perb, full(1, D), full(1, D), perb, perb,
                  full(D, LANES), full(D, LANES), full(1, LANES)],
        out_specs=(rowblk(D), rowblk(D), rowblk(LANES), rowblk(LANES), rowblk(LANES), tile_spec, tile_spec),
        scratch_shapes=[pltpu.VMEM((1, LANES), F32)],
        compiler_params=_cparams(("arbitrary",)),
        name="out_proj_ln_router",
    )(merged, w_out_b, x2d, gate1, ln_g.reshape(1, D), ln_b.reshape(1, D), scale2, shift2, rw_hi, rw_lo, rb)


def _expert_kernel(be_ref, nu_ref, x_ref, wgu_ref, bgu_ref, wd_ref, bd_ref, o_ref):
    i = pl.program_id(0)

    @pl.when(i < nu_ref[0])
    def _():
        gu = jnp.dot(x_ref[...], wgu_ref[0], preferred_element_type=F32) + bgu_ref[0]
        glu = jnp.minimum(gu[:, :EXPERT_FF], SWIGLU_LIMIT)
        lin = jnp.clip(gu[:, EXPERT_FF:], -SWIGLU_LIMIT, SWIGLU_LIMIT)
        act = glu * jax.nn.sigmoid(SWIGLU_ALPHA * glu) * (lin + 1.0)
        y = jnp.dot(act.astype(BF16), wd_ref[0], preferred_element_type=F32) + bd_ref[0]
        o_ref[...] = y.astype(o_ref.dtype)

    @pl.when(i >= nu_ref[0])
    def _():
        o_ref[...] = jnp.zeros_like(o_ref)


def expert_ffn(block_e, n_used, xb, wgu, bgu, wd, bd):
    rows, D = xb.shape
    E, _, F2 = wgu.shape
    nblk = rows // MOE_BM
    grid_spec = pltpu.PrefetchScalarGridSpec(
        num_scalar_prefetch=2,
        grid=(nblk,),
        in_specs=[pl.BlockSpec((MOE_BM, D), lambda i, be, nu: (i, 0)),
                  pl.BlockSpec((1, D, F2), lambda i, be, nu: (be[i], 0, 0)),
                  pl.BlockSpec((1, 1, F2), lambda i, be, nu: (be[i], 0, 0)),
                  pl.BlockSpec((1, F2 // 2, D), lambda i, be, nu: (be[i], 0, 0)),
                  pl.BlockSpec((1, 1, D), lambda i, be, nu: (be[i], 0, 0))],
        out_specs=pl.BlockSpec((MOE_BM, D), lambda i, be, nu: (i, 0)),
    )
    return pl.pallas_call(
        _expert_kernel,
        out_shape=jax.ShapeDtypeStruct((rows, D), BF16),
        grid_spec=grid_spec,
        compiler_params=_cparams(("arbitrary",)),
        name="expert_ffn",
    )(block_e, n_used, xb, wgu, bgu.reshape(E, 1, F2), wd, bd.reshape(E, 1, D))


COMBINE_WIN = 64
COMBINE_ALIGN = 16
COMBINE_COLS = 512


def _combine_kernel(lo_ref, nr_ref, meta_ref, wt_ref, x1_ref, gate_ref, g_ref, b_ref, sc_ref, sh_ref, yb_ref,
                    x2_ref, hb_ref, buf_ref, obuf_ref, acc_ref, sem_ref, osem_ref, *, last_row):
    E, R = N_EXPERTS, COMBINE_WIN
    t = pl.program_id(0)
    nt = pl.num_programs(0)
    slot = t % 2

    def window_copy(tile, e, rnd, dst, sem):
        start = jnp.minimum(lo_ref[tile * E + e] + rnd * R, last_row)
        start = pl.multiple_of(start, COMBINE_ALIGN)
        return pltpu.make_async_copy(yb_ref.at[pl.ds(start, R), :], dst.at[pl.ds(e * R, R), :], sem)

    def start_all(tile, rnd, dst, sem):
        for e in range(E):
            window_copy(tile, e, rnd, dst, sem).start()

    def wait_all(tile, rnd, dst, sem):
        for e in range(E):
            window_copy(tile, e, rnd, dst, sem).wait()

    @pl.when(t == 0)
    def _():
        start_all(0, 0, buf_ref.at[0], sem_ref.at[0])

    @pl.when(t + 1 < nt)
    def _():
        start_all(t + 1, 0, buf_ref.at[1 - slot], sem_ref.at[1 - slot])

    meta = meta_ref[...]
    wts = wt_ref[...]
    CW = COMBINE_COLS
    col0 = lax.broadcasted_iota(jnp.int32, (meta.shape[0], CW), 1)

    def accumulate_round(rnd, win_ref, first):
        for cc in range(E * R // CW):
            hi = jnp.zeros(col0.shape, F32)
            lo = jnp.zeros(col0.shape, F32)
            for kq in range(TOP_K):
                e_k = meta[:, kq:kq + 1]
                loc = meta[:, TOP_K + kq:TOP_K + kq + 1] - rnd * R
                hit = jnp.logical_and(col0 == e_k * R + loc - cc * CW, jnp.logical_and(loc >= 0, loc < R))
                w = wts[:, kq:kq + 1]
                w_hi = w.astype(BF16).astype(F32)
                hi = jnp.where(hit, w_hi, hi)
                lo = jnp.where(hit, w - w_hi, lo)
            win = win_ref[pl.ds(cc * CW, CW), :]
            part = (jnp.dot(hi.astype(BF16), win, preferred_element_type=F32)
                    + jnp.dot(lo.astype(BF16), win, preferred_element_type=F32))
            if first and cc == 0:
                acc_ref[...] = part
            else:
                acc_ref[...] += part

    wait_all(t, 0, buf_ref.at[slot], sem_ref.at[slot])
    accumulate_round(0, buf_ref.at[slot], True)

    def extra_round(rnd, carry):
        start_all(t, rnd, obuf_ref, osem_ref.at[0])
        wait_all(t, rnd, obuf_ref, osem_ref.at[0])
        accumulate_round(rnd, obuf_ref, False)
        return carry

    lax.fori_loop(1, nr_ref[t], extra_round, 0)

    x2 = _layer_norm(DEEPNORM_ALPHA * x1_ref[...] + gate_ref[0] * acc_ref[...], g_ref[...], b_ref[...])
    x2_ref[...] = x2
    hb_ref[...] = (x2 * (1.0 + sc_ref[0]) + sh_ref[0]).astype(hb_ref.dtype)


def moe_combine_ln(yb, win_lo, n_rounds, meta, wt, x1, gate2, ln_g, ln_b, scale_next, shift_next, S):
    T, D = x1.shape
    tm = ROUTE_TM
    nt = T // tm
    spb = S // tm
    E, R = N_EXPERTS, COMBINE_WIN
    rowblk = lambda w: pl.BlockSpec((tm, w), lambda i, lo, nr: (i, 0))
    full = pl.BlockSpec((1, D), lambda i, lo, nr: (0, 0))
    perb = pl.BlockSpec((1, 1, D), lambda i, lo, nr: (i // spb, 0, 0))
    grid_spec = pltpu.PrefetchScalarGridSpec(
        num_scalar_prefetch=2,
        grid=(nt,),
        in_specs=[rowblk(LANES), rowblk(LANES), rowblk(D), perb, full, full, perb, perb,
                  pl.BlockSpec(memory_space=pl.ANY)],
        out_specs=(rowblk(D), rowblk(D)),
        scratch_shapes=[pltpu.VMEM((2, E * R, D), BF16), pltpu.VMEM((E * R, D), BF16), pltpu.VMEM((tm, D), F32),
                        pltpu.SemaphoreType.DMA((2,)), pltpu.SemaphoreType.DMA((1,))],
    )
    return pl.pallas_call(
        functools.partial(_combine_kernel, last_row=yb.shape[0] - R),
        out_shape=(jax.ShapeDtypeStruct((T, D), F32), jax.ShapeDtypeStruct((T, D), BF16)),
        grid_spec=grid_spec,
        compiler_params=_cparams(("arbitrary",)),
        name="moe_combine_ln",
    )(win_lo, n_rounds, meta, wt, x1, gate2, ln_g.reshape(1, D), ln_b.reshape(1, D), scale_next, shift_next, yb)


def routed_ffn_ln(h2, idx, wt, rank, tile_base, tile_cnt, wgu, bgu, wd, bd, x1, gate2, ln_g, ln_b,
                  scale_next, shift_next, S):
    T, D = h2.shape
    E, R = N_EXPERTS, COMBINE_WIN
    n_pairs = T * TOP_K
    e4 = idx[:, :TOP_K]
    base = tile_base[:, 0, :E]
    cnt = tile_cnt[:, 0, :E]
    counts = base[-1] + cnt[-1]
    padded = ((counts + MOE_BM - 1) // MOE_BM) * MOE_BM
    pend = jnp.cumsum(padded)
    pstart = pend - padded
    onehot = e4[:, :, None] == jnp.arange(E, dtype=jnp.int32)[None, None, :]
    pick = lambda table: jnp.sum(jnp.where(onehot, table, 0), axis=-1)
    dest = pick(pstart[None, None, :]) + rank[:, :TOP_K]
    nblk = n_pairs // MOE_BM + E + 1
    rows = nblk * MOE_BM
    slot_tok = jnp.zeros((rows,), jnp.int32).at[dest.reshape(-1)].set(
        jnp.arange(n_pairs, dtype=jnp.int32) // TOP_K)
    blk_start = jnp.arange(nblk, dtype=jnp.int32) * MOE_BM
    block_e = jnp.minimum(jnp.sum((pend[None, :] <= blk_start[:, None]).astype(jnp.int32), axis=1), E - 1)
    n_used = (pend[-1] // MOE_BM).astype(jnp.int32).reshape(1)
    xb = h2[slot_tok]
    yb = expert_ffn(block_e, n_used, xb, wgu, bgu, wd, bd)
    lo = pstart[None, :] + base
    win_lo = (lo // COMBINE_ALIGN) * COMBINE_ALIGN
    n_chunks = (lo - win_lo + cnt + R - 1) // R
    n_rounds = jnp.maximum(jnp.max(n_chunks, axis=1), 1).astype(jnp.int32)
    win_tok = jnp.repeat(win_lo, ROUTE_TM, axis=0)[:, None, :]
    loc = dest - pick(win_tok)
    meta = jnp.zeros((T, LANES), jnp.int32).at[:, :TOP_K].set(e4).at[:, TOP_K:2 * TOP_K].set(loc)
    return moe_combine_ln(yb, win_lo.reshape(-1).astype(jnp.int32), n_rounds, meta, wt, x1, gate2, ln_g, ln_b,
                          scale_next, shift_next, S)


def _mix_weights(w_in_l):
    def cols(rng, pad=0):
        part = w_in_l[:, rng[0]:rng[1]]
        if pad:
            part = jnp.pad(part, ((0, 0), (0, pad)))
        return part
    w_mix = jnp.concatenate([
        cols(SRC_POOL), cols(SRC_GLA_V), cols(SRC_GLA_G), cols(SRC_RWKV_RKV), cols(SRC_ATT),
        cols(SRC_GLA_Q), cols(SRC_GLA_K),
        cols(SRC_RWKV_LO, RWKV_LO_PAD - (SRC_RWKV_LO[1] - SRC_RWKV_LO[0])),
        cols(SRC_GLA_A, GLA_A_PAD - (SRC_GLA_A[1] - SRC_GLA_A[0]))], axis=1).astype(BF16)
    w_gates = w_in_l[:, SRC_GATES[0]:].astype(BF16)
    return w_mix, w_gates


def kernel(x, c, ada_w, ada_b, w_in, pool_w, pool_scale, gla_w_alpha, gla_b_alpha, gla_norm_g, rwkv_mu, rwkv_w0, rwkv_w2, rwkv_a0, rwkv_a2, rwkv_g2, rwkv_k_k, rwkv_k_a, rwkv_r_k, rwkv_ln_g, rwkv_ln_b, w_branch_a, w_branch_b, w_branch_c, w_branch_d, w_out, ln1_g, ln1_b, router_w, router_b, w_gate_up, b_gate_up, w_down, b_down, ln2_g, ln2_b):
    B, S, D = x.shape
    T = B * S
    mod = ada_modulation(c, ada_w, ada_b)
    mods = [[mod[l, :, None, i * D:(i + 1) * D] for i in range(6)] for l in range(DEPTH)]
    x2d = x.reshape(T, D)
    hb2d = modulate(x, mods[0][1], mods[0][0]).reshape(T, D)
    for l in range(DEPTH):
        shift1, scale1, gate1, shift2, scale2, gate2 = mods[l]
        shift_next, scale_next = (mods[l + 1][0], mods[l + 1][1]) if l + 1 < DEPTH else (shift1, scale1)
        w_mix, w_gates = _mix_weights(w_in[l])
        P = matmul(hb2d, w_mix, 1024, MIX_TN).reshape(B, S, MIX_WIDTH)
        y_a = pool_mixer(P, pool_w[l], pool_scale[l])
        y_b = dilated_attention(P)
        y_c = gla_mixer(P, gla_w_alpha[l], gla_b_alpha[l], gla_norm_g[l])
        r_, w_, k_, v_, kk_, b_, gate_, bonus_ = rwkv_prep(
            P, rwkv_mu[l], rwkv_w0[l], rwkv_w2[l], rwkv_a0[l], rwkv_a2[l], rwkv_g2[l],
            rwkv_k_k[l], rwkv_k_a[l], rwkv_r_k[l].reshape(-1))
        y_t = rwkv_chunked(r_, w_, k_, v_, kk_, b_)
        y_d = rwkv_post(y_t, gate_, bonus_, rwkv_ln_g[l], rwkv_ln_b[l])
        ys = [y.reshape(T, -1) for y in (y_a, y_b, y_c, y_d)]
        ws = [w[l].astype(BF16) for w in (w_branch_a, w_branch_b, w_branch_c, w_branch_d)]
        merged = branch_merge(hb2d, w_gates, ys, ws)
        x1, h2, idx, wt, rank, tile_base, tile_cnt = out_ln_router(
            merged, w_out[l].astype(BF16), x2d, gate1, ln1_g[l], ln1_b[l], scale2, shift2,
            router_w[l], router_b[l], S)
        x2d, hb2d = routed_ffn_ln(h2, idx, wt, rank, tile_base, tile_cnt, w_gate_up[l].astype(BF16), b_gate_up[l],
                                  w_down[l].astype(BF16), b_down[l], x1, gate2, ln2_g[l], ln2_b[l],
                                  scale_next, shift_next, S)
    return x2d.reshape(B, S, D)
```

```python
import functools

import numpy as np
import jax
import jax.numpy as jnp
from jax import lax
from jax.experimental import pallas as pl
from jax.experimental.pallas import tpu as pltpu

F32 = jnp.float32
BF16 = jnp.bfloat16
HIGHEST = lax.Precision.HIGHEST

D_MODEL = 2048
DEPTH = 2
POOL_WINDOWS = (2, 4, 8, 16)
POOL_GROUP = 128
POOL_WIDTH = 512
ATT_GROUPS = ((128, 1), (512, 4), (2048, 16))
ATT_HEAD_DIM = 64
ATT_HEADS = 12
ATT_WIDTH = 768
ATT_OUT_WIDTH = 256
ATT_BLOCK = 128
ALIBI_SLOPES = tuple(2.0 ** (-8.0 * (h + 1) / ATT_HEADS) for h in range(ATT_HEADS))
GLA_HEADS = 4
GLA_DK = 64
GLA_DV = 128
GLA_KEY_WIDTH = 256
GLA_VAL_WIDTH = 512
GLA_GATE_RANK = 16
GLA_GATE_TEMP = 16.0
GLA_CHUNK = 32
GLA_NORM_EPS = 1e-6
RWKV_HEADS = 8
RWKV_HEAD_DIM = 64
RWKV_WIDTH = 512
RWKV_DECAY_RANK = 32
RWKV_AAA_RANK = 32
RWKV_GATE_RANK = 96
RWKV_GN_EPS = 64e-5
N_BRANCHES = 4
N_EXPERTS = 32
TOP_K = 4
EXPERT_FF = 1024
SWIGLU_LIMIT = 7.0
SWIGLU_ALPHA = 1.702
LN_EPS = 1e-5
DEEPNORM_ALPHA = (2 * DEPTH) ** 0.25

LANES = 128
SUBLANES = 8
VMEM_LIMIT = 56 * 1024 * 1024

SRC_POOL = (0, 512)
SRC_ATT = (512, 2816)
SRC_GLA_Q = (2816, 3072)
SRC_GLA_K = (3072, 3328)
SRC_GLA_V = (3328, 3840)
SRC_GLA_G = (3840, 4352)
SRC_GLA_A = (4352, 4368)
SRC_RWKV_RKV = (4368, 5904)
SRC_RWKV_LO = (5904, 6064)
SRC_GATES = (6064, 14256)
OFF_POOL = 0
OFF_GLA_V = 512
OFF_GLA_G = 1024
OFF_RWKV_RKV = 1536
OFF_ATT_Q = 3072
OFF_ATT_K = OFF_ATT_Q + ATT_WIDTH
OFF_ATT_V = OFF_ATT_K + ATT_WIDTH
OFF_GLA_Q = 5376
OFF_GLA_K = 5632
OFF_RWKV_LO = 5888
RWKV_LO_PAD = 256
OFF_GLA_A = 6144
GLA_A_PAD = 128
MIX_WIDTH = 6272
MIX_TN = 896

MOE_BM = 256
NEG_BIG = -1e30


def _cparams(sem):
    return pltpu.CompilerParams(dimension_semantics=sem, vmem_limit_bytes=VMEM_LIMIT)


def _ada_kernel(c_ref, w_ref, b_ref, o_ref):
    c = c_ref[...]
    s = c * jax.nn.sigmoid(c)
    o_ref[0] = jnp.dot(s, w_ref[0], precision=HIGHEST, preferred_element_type=F32) + b_ref[0]


def ada_modulation(c, ada_w, ada_b):
    L, D, N = ada_w.shape
    B = c.shape[0]
    cp = jnp.zeros((SUBLANES, D), F32).at[:B].set(c)
    tn = 1024
    out = pl.pallas_call(
        _ada_kernel,
        out_shape=jax.ShapeDtypeStruct((L, SUBLANES, N), F32),
        grid=(L, N // tn),
        in_specs=[pl.BlockSpec((SUBLANES, D), lambda l, j: (0, 0)),
                  pl.BlockSpec((1, D, tn), lambda l, j: (l, 0, j)),
                  pl.BlockSpec((1, 1, tn), lambda l, j: (l, 0, j))],
        out_specs=pl.BlockSpec((1, SUBLANES, tn), lambda l, j: (l, 0, j)),
        compiler_params=_cparams(("parallel", "parallel")),
        name="ada_modulation",
    )(cp, ada_w, ada_b.reshape(L, 1, N))
    return out[:, :B]


def _modulate_kernel(x_ref, sc_ref, sh_ref, o_ref):
    o_ref[0] = (x_ref[0] * (1.0 + sc_ref[0]) + sh_ref[0]).astype(o_ref.dtype)


def modulate(x, scale, shift):
    B, S, D = x.shape
    ts = 1024
    return pl.pallas_call(
        _modulate_kernel,
        out_shape=jax.ShapeDtypeStruct((B, S, D), BF16),
        grid=(B, S // ts),
        in_specs=[pl.BlockSpec((1, ts, D), lambda b, i: (b, i, 0)),
                  pl.BlockSpec((1, 1, D), lambda b, i: (b, 0, 0)),
                  pl.BlockSpec((1, 1, D), lambda b, i: (b, 0, 0))],
        out_specs=pl.BlockSpec((1, ts, D), lambda b, i: (b, i, 0)),
        compiler_params=_cparams(("parallel", "parallel")),
        name="modulate",
    )(x, scale, shift)


def _mm_kernel(x_ref, w_ref, o_ref):
    o_ref[...] = jnp.dot(x_ref[...], w_ref[...], preferred_element_type=F32).astype(o_ref.dtype)


def matmul(x, w, tm, tn, out_dtype=F32):
    M, K = x.shape
    N = w.shape[1]
    return pl.pallas_call(
        _mm_kernel,
        out_shape=jax.ShapeDtypeStruct((M, N), out_dtype),
        grid=(N // tn, M // tm),
        in_specs=[pl.BlockSpec((tm, K), lambda j, i: (i, 0)),
                  pl.BlockSpec((K, tn), lambda j, i: (0, j))],
        out_specs=pl.BlockSpec((tm, tn), lambda j, i: (i, j)),
        compiler_params=_cparams(("parallel", "parallel")),
        name="matmul",
    )(x, w)


def _pool_kernel(p_ref, w_ref, sc_ref, o_ref):
    g = pl.program_id(1)
    v = p_ref[0]
    S = v.shape[0]
    row = lax.broadcasted_iota(jnp.int32, v.shape, 0)
    win = jnp.left_shift(2, g)
    s = v
    pooled_sum = v
    for k, sh in enumerate((1, 2, 4, 8)):
        s = s + jnp.where(row >= sh, pltpu.roll(s, sh, 0), 0.0)
        pooled_sum = jnp.where(g >= k, s, pooled_sum)
    cnt = jnp.minimum(row + 1, win).astype(F32)
    diff = pooled_sum / cnt - v
    y = jnp.dot(diff.astype(BF16), w_ref[0].astype(BF16), preferred_element_type=F32)
    o_ref[0] = y * sc_ref[...]


def pool_mixer(P, pool_w, pool_scale):
    B, S, _ = P.shape
    G = len(POOL_WINDOWS)
    return pl.pallas_call(
        _pool_kernel,
        out_shape=jax.ShapeDtypeStruct((B, S, POOL_WIDTH), F32),
        grid=(B, G),
        in_specs=[pl.BlockSpec((1, S, POOL_GROUP), lambda b, g: (b, 0, OFF_POOL // POOL_GROUP + g)),
                  pl.BlockSpec((1, POOL_GROUP, POOL_GROUP), lambda b, g: (g, 0, 0)),
                  pl.BlockSpec((1, POOL_GROUP), lambda b, g: (0, g))],
        out_specs=pl.BlockSpec((1, S, POOL_GROUP), lambda b, g: (b, 0, g)),
        compiler_params=_cparams(("parallel", "parallel")),
        name="pool_mixer",
    )(P, pool_w, pool_scale.reshape(1, POOL_WIDTH))


def _att_kernel(q_ref, kc_ref, kp_ref, vc_ref, vp_ref, num_ref, den_ref, m_ref, *, slopes, dilation):
    hp = pl.program_id(1)
    n = pl.program_id(2)
    i = lax.broadcasted_iota(jnp.int32, (ATT_BLOCK, ATT_BLOCK), 0)
    j = lax.broadcasted_iota(jnp.int32, (ATT_BLOCK, ATT_BLOCK), 1)
    dist_c = i - j
    dist_p = dist_c + ATT_BLOCK
    valid_c = dist_c >= 0
    valid_p = jnp.logical_and(dist_p <= ATT_BLOCK, n > 0)
    lane_lo = lax.broadcasted_iota(jnp.int32, (ATT_BLOCK, LANES), 1) < ATT_HEAD_DIM
    dn = (((1,), (1,)), ((), ()))

    def residue(r, carry):
        rows = slice(None) if dilation == 1 else pl.ds(r, ATT_BLOCK, stride=dilation)
        q2, kc2, kp2, vc2, vp2 = (ref[0, rows, :] for ref in (q_ref, kc_ref, kp_ref, vc_ref, vp_ref))
        nums, dens, ms = [], [], []
        for h in range(2):
            sl = slice(h * ATT_HEAD_DIM, (h + 1) * ATT_HEAD_DIM)
            slope = jnp.where(hp == 0, slopes[h], slopes[2 + h]) * float(dilation)
            q = (q2[:, sl] * (ATT_HEAD_DIM ** -0.5)).astype(BF16)
            s_c = lax.dot_general(q, kc2[:, sl].astype(BF16), dn, preferred_element_type=F32)
            s_p = lax.dot_general(q, kp2[:, sl].astype(BF16), dn, preferred_element_type=F32)
            s_c = jnp.where(valid_c, s_c - slope * dist_c.astype(F32), NEG_BIG)
            s_p = jnp.where(valid_p, s_p - slope * dist_p.astype(F32), NEG_BIG)
            m = jnp.maximum(jnp.max(s_c, -1, keepdims=True), jnp.max(s_p, -1, keepdims=True))
            p_c = jnp.exp(s_c - m)
            p_p = jnp.exp(s_p - m)
            dens.append(jnp.sum(p_c, -1, keepdims=True) + jnp.sum(p_p, -1, keepdims=True))
            nums.append(jnp.dot(p_c.astype(BF16), vc2[:, sl].astype(BF16), preferred_element_type=F32)
                        + jnp.dot(p_p.astype(BF16), vp2[:, sl].astype(BF16), preferred_element_type=F32))
            ms.append(m)
        num_ref[0, rows, :] = jnp.concatenate(nums, axis=-1)
        den_ref[0, rows, :] = jnp.where(lane_lo, dens[0], dens[1])
        m_ref[0, rows, :] = jnp.where(lane_lo, ms[0], ms[1])
        return carry

    if dilation == 1:
        residue(0, 0)
    else:
        lax.fori_loop(0, dilation, residue, 0)


def att_group(P, g, dilation):
    B, S, NP = P.shape
    rows = ATT_BLOCK * dilation
    nb = S // rows
    qo, ko, vo = (o // LANES + g * 2 for o in (OFF_ATT_Q, OFF_ATT_K, OFF_ATT_V))
    blk = (1, rows, LANES)
    cur = lambda off: pl.BlockSpec(blk, lambda b, hp, n: (b, n, off + hp))
    prev = lambda off: pl.BlockSpec(blk, lambda b, hp, n: (b, jnp.maximum(n - 1, 0), off + hp))
    ospec = pl.BlockSpec(blk, lambda b, hp, n: (b, n, hp))
    oshape = jax.ShapeDtypeStruct((B, S, ATT_OUT_WIDTH), F32)
    slopes = ALIBI_SLOPES[g * 4:(g + 1) * 4]
    return pl.pallas_call(
        functools.partial(_att_kernel, slopes=slopes, dilation=dilation),
        out_shape=(oshape, oshape, oshape),
        grid=(B, 2, nb),
        in_specs=[cur(qo), cur(ko), prev(ko), cur(vo), prev(vo)],
        out_specs=(ospec, ospec, ospec),
        compiler_params=_cparams(("parallel", "parallel", "arbitrary")),
        name=f"dilated_attention_g{g}",
    )(P, P, P, P, P)


def _att_merge_kernel(*refs):
    o_ref = refs[-1]
    nums, dens, ms = refs[0:3], refs[3:6], refs[6:9]
    mx = jnp.maximum(jnp.maximum(ms[0][0], ms[1][0]), ms[2][0])
    num = jnp.zeros_like(mx)
    den = jnp.zeros_like(mx)
    for g in range(3):
        e = jnp.exp(ms[g][0] - mx)
        num = num + nums[g][0] * e
        den = den + dens[g][0] * e
    o_ref[0] = num / den


def dilated_attention(P):
    B, S, _ = P.shape
    parts = [att_group(P, g, d) for g, (_, d) in enumerate(ATT_GROUPS)]
    args = [p[0] for p in parts] + [p[1] for p in parts] + [p[2] for p in parts]
    ts = 1024
    spec = pl.BlockSpec((1, ts, ATT_OUT_WIDTH), lambda b, i: (b, i, 0))
    return pl.pallas_call(
        _att_merge_kernel,
        out_shape=jax.ShapeDtypeStruct((B, S, ATT_OUT_WIDTH), F32),
        grid=(B, S // ts),
        in_specs=[spec] * 9,
        out_specs=spec,
        compiler_params=_cparams(("parallel", "parallel")),
        name="dilated_attention_merge",
    )(*args)


GLA_TS = 256


def _gla_kernel(q_ref, k_ref, v_ref, g_ref, a_ref, wa_ref, ba_ref, ng_ref, o_ref, st_ref):
    @pl.when(pl.program_id(1) == 0)
    def _():
        st_ref[...] = jnp.zeros_like(st_ref)

    TS = GLA_TS
    nchunk = TS // GLA_CHUNK
    logit = jnp.dot(a_ref[0], wa_ref[...], precision=HIGHEST, preferred_element_type=F32) + ba_ref[...]
    log_a = (jnp.minimum(logit, 0.0) - jnp.log1p(jnp.exp(-jnp.abs(logit)))) / GLA_GATE_TEMP
    i = lax.broadcasted_iota(jnp.int32, (TS, TS), 0)
    j = lax.broadcasted_iota(jnp.int32, (TS, TS), 1)
    same_chunk_causal = jnp.logical_and(i // GLA_CHUNK == j // GLA_CHUNK, j <= i)
    tri = same_chunk_causal.astype(F32)
    bcum = jnp.dot(tri, log_a, precision=HIGHEST, preferred_element_type=F32)
    eb = jnp.exp(bcum)
    qd = q_ref[0] * (GLA_DK ** -0.5) * eb
    kd = k_ref[0] * jnp.exp(-bcum)
    k_all = k_ref[0]
    v_all = v_ref[0]
    dn_nt = (((1,), (1,)), ((), ()))
    dn_tn = (((0,), (0,)), ((), ()))
    for h in range(GLA_HEADS):
        ks = slice(h * GLA_DK, (h + 1) * GLA_DK)
        vs = slice(h * GLA_DV, (h + 1) * GLA_DV)
        qh = qd[:, ks].astype(BF16)
        vh = v_all[:, vs].astype(BF16)
        scores = lax.dot_general(qh, kd[:, ks].astype(BF16), dn_nt, preferred_element_type=F32)
        scores = jnp.where(same_chunk_causal, scores, 0.0)
        o_intra = jnp.dot(scores.astype(BF16), vh, preferred_element_type=F32)
        st = st_ref[h]
        outs = []
        for c in range(nchunk):
            rs = slice(c * GLA_CHUNK, (c + 1) * GLA_CHUNK)
            last = (c + 1) * GLA_CHUNK - 1
            o_c = o_intra[rs] + lax.dot_general(qh[rs], st.astype(BF16), dn_nt, preferred_element_type=F32)
            outs.append(o_c)
            b_last = bcum[last:last + 1, ks]
            k_tail = (k_all[rs, ks] * jnp.exp(b_last - bcum[rs, ks])).astype(BF16)
            st = st * jnp.exp(b_last) + lax.dot_general(vh[rs], k_tail, dn_tn, preferred_element_type=F32)
        st_ref[h] = st
        o = jnp.concatenate(outs, axis=0)
        o = o * lax.rsqrt(jnp.mean(o * o, -1, keepdims=True) + GLA_NORM_EPS)
        gg = g_ref[0][:, vs]
        o_ref[0, :, vs] = o * ng_ref[:, vs] * (gg * jax.nn.sigmoid(gg))


def gla_mixer(P, w_alpha, b_alpha, norm_g):
    B, S, _ = P.shape
    TS = GLA_TS
    wa = jnp.zeros((GLA_A_PAD, GLA_KEY_WIDTH), F32).at[:GLA_GATE_RANK].set(w_alpha)
    col = lambda off, w: pl.BlockSpec((1, TS, w), lambda b, i: (b, i, off // w))
    full = lambda r, c: pl.BlockSpec((r, c), lambda b, i: (0, 0))
    return pl.pallas_call(
        _gla_kernel,
        out_shape=jax.ShapeDtypeStruct((B, S, GLA_VAL_WIDTH), F32),
        grid=(B, S // TS),
        in_specs=[col(OFF_GLA_Q, GLA_KEY_WIDTH), col(OFF_GLA_K, GLA_KEY_WIDTH),
                  col(OFF_GLA_V, GLA_VAL_WIDTH), col(OFF_GLA_G, GLA_VAL_WIDTH),
                  col(OFF_GLA_A, GLA_A_PAD),
                  full(GLA_A_PAD, GLA_KEY_WIDTH), full(1, GLA_KEY_WIDTH), full(1, GLA_VAL_WIDTH)],
        out_specs=pl.BlockSpec((1, TS, GLA_VAL_WIDTH), lambda b, i: (b, i, 0)),
        scratch_shapes=[pltpu.VMEM((GLA_HEADS, GLA_DV, GLA_DK), F32)],
        compiler_params=_cparams(("parallel", "arbitrary")),
        name="gla_mixer",
    )(P, P, P, P, P, wa, b_alpha.reshape(1, -1), norm_g.reshape(1, -1))


RWKV_PREP_TS = 512
RWKV_SHIFT_LO = RWKV_DECAY_RANK + RWKV_AAA_RANK + RWKV_GATE_RANK


def _segment_ones(n, seg):
    i = lax.broadcasted_iota(jnp.int32, (n, n), 0)
    j = lax.broadcasted_iota(jnp.int32, (n, n), 1)
    return (i // seg == j // seg).astype(F32)


def _rwkv_prep_kernel(rkv_ref, lo_ref, rkv_prev_ref, lo_prev_ref, mu_rkv_ref, mu_lo_ref, w0_ref, w2_ref,
                      a0_ref, a2_ref, g2_ref, kk_ref, ka_ref, rk_ref,
                      r_out, w_out, k_out, v_out, kk_out, b_out, gate_out, bonus_out):
    first = pl.program_id(1) == 0

    def shifted(cur, prev_ref, mu):
        prev_row = jnp.where(first, 0.0, prev_ref[0][SUBLANES - 1:SUBLANES, :])
        row = lax.broadcasted_iota(jnp.int32, cur.shape, 0)
        prev = jnp.where(row == 0, prev_row, pltpu.roll(cur, 1, 0))
        return cur + (prev - cur) * mu

    xs = shifted(rkv_ref[0], rkv_prev_ref, mu_rkv_ref[...])
    lo = shifted(lo_ref[0], lo_prev_ref, mu_lo_ref[...])
    W = RWKV_WIDTH
    r, k, v = xs[:, 0:W], xs[:, W:2 * W], xs[:, 2 * W:3 * W]
    hdot = functools.partial(jnp.dot, precision=HIGHEST, preferred_element_type=F32)
    z = w0_ref[...] + hdot(jnp.tanh(lo), w2_ref[...])
    log_decay = -jax.nn.sigmoid(z) * float(np.exp(-0.5))
    a = jax.nn.sigmoid(a0_ref[...] + hdot(lo, a2_ref[...]))
    gate = hdot(jax.nn.sigmoid(lo), g2_ref[...])
    seg = _segment_ones(W, RWKV_HEAD_DIM)
    kk = k * kk_ref[...]
    kk_norm = jnp.sqrt(hdot(kk * kk, seg))
    kk = kk / jnp.maximum(kk_norm, 1e-12)
    k2 = k * (1.0 + (a - 1.0) * ka_ref[...])
    bonus = hdot(r * k2 * rk_ref[...], seg) * v
    r_out[0] = r
    w_out[0] = log_decay
    k_out[0] = k2
    v_out[0] = v
    kk_out[0] = kk
    b_out[0] = kk * a
    gate_out[0] = gate
    bonus_out[0] = bonus


def rwkv_prep(P, mu, w0, w2, a0, a2, g2, k_k, k_a, r_k):
    B, S, _ = P.shape
    TS = RWKV_PREP_TS
    W = RWKV_WIDTH
    W3 = 3 * W
    n_rkv = SRC_RWKV_RKV[1] - SRC_RWKV_RKV[0]
    mu_rkv = mu[:n_rkv].reshape(1, W3)
    mu_lo = jnp.zeros((1, RWKV_LO_PAD), F32).at[0, :RWKV_SHIFT_LO].set(mu[n_rkv:])
    w2p = jnp.zeros((RWKV_LO_PAD, W), F32).at[0:RWKV_DECAY_RANK].set(w2)
    a2p = jnp.zeros((RWKV_LO_PAD, W), F32).at[RWKV_DECAY_RANK:RWKV_DECAY_RANK + RWKV_AAA_RANK].set(a2)
    g2p = jnp.zeros((RWKV_LO_PAD, W), F32).at[RWKV_DECAY_RANK + RWKV_AAA_RANK:RWKV_SHIFT_LO].set(g2)
    row = lambda a: a.reshape(1, -1)
    full = lambda r, c: pl.BlockSpec((r, c), lambda b, i: (0, 0))
    tpb = TS // SUBLANES
    prev_map = lambda off, w: pl.BlockSpec((1, SUBLANES, w), lambda b, i: (b, jnp.maximum(i * tpb - 1, 0), off // w))
    oshape = jax.ShapeDtypeStruct((B, S, W), F32)
    ospec = pl.BlockSpec((1, TS, W), lambda b, i: (b, i, 0))
    return pl.pallas_call(
        _rwkv_prep_kernel,
        out_shape=(oshape,) * 8,
        grid=(B, S // TS),
        in_specs=[pl.BlockSpec((1, TS, W3), lambda b, i: (b, i, OFF_RWKV_RKV // W3)),
                  pl.BlockSpec((1, TS, RWKV_LO_PAD), lambda b, i: (b, i, OFF_RWKV_LO // RWKV_LO_PAD)),
                  prev_map(OFF_RWKV_RKV, W3), prev_map(OFF_RWKV_LO, RWKV_LO_PAD),
                  full(1, W3), full(1, RWKV_LO_PAD), full(1, W), full(RWKV_LO_PAD, W),
                  full(1, W), full(RWKV_LO_PAD, W), full(RWKV_LO_PAD, W),
                  full(1, W), full(1, W), full(1, W)],
        out_specs=(ospec,) * 8,
        compiler_params=_cparams(("parallel", "arbitrary")),
        name="rwkv7_prep",
    )(P, P, P, P, mu_rkv, mu_lo, row(w0), w2p, row(a0), a2p, g2p, row(k_k), row(k_a), row(r_k))


RWKV_CHUNK = 64
RWKV_CHUNK_TT = 512


def _rwkv_chunk_kernel(r_ref, lw_ref, k_ref, v_ref, kk_ref, b_ref, y_ref, h_ref):
    @pl.when(pl.program_id(2) == 0)
    def _():
        h_ref[...] = jnp.zeros_like(h_ref)

    C = RWKV_CHUNK
    TT = RWKV_CHUNK_TT
    N = RWKV_HEAD_DIM
    R2 = 2 * C
    bdot = lambda x, y: jnp.dot(x.astype(BF16), y.astype(BF16), preferred_element_type=F32)
    r, lw, k, v, kk, b = (ref[0] for ref in (r_ref, lw_ref, k_ref, v_ref, kk_ref, b_ref))
    i = lax.broadcasted_iota(jnp.int32, (TT, TT), 0)
    j = lax.broadcasted_iota(jnp.int32, (TT, TT), 1)
    tri = jnp.logical_and(i // C == j // C, j <= i).astype(F32)
    cum = jnp.dot(tri, lw, precision=HIGHEST, preferred_element_type=F32)
    lane_lo = lax.broadcasted_iota(jnp.int32, (C, LANES), 1) < N

    def stack(x):
        return jnp.concatenate([jnp.where(lane_lo, x, 0.0), jnp.where(lane_lo, 0.0, x)], axis=0)

    ti = lax.broadcasted_iota(jnp.int32, (R2, R2), 0)
    si = lax.broadcasted_iota(jnp.int32, (R2, R2), 1)
    strict = ti > si
    incl = ti >= si
    eye = (ti == si).astype(F32)
    blk16 = ti // 16 == si // 16
    off32 = jnp.logical_and(ti // 32 == si // 32, jnp.logical_not(blk16))
    off64 = jnp.logical_and(ti // 64 == si // 64, ti // 32 != si // 32)
    ones_c = jnp.ones((C, LANES), F32)

    chunks = range(TT // C)
    each = lambda f, *lists: [f(*args) for args in zip(*lists)]
    rows = [slice(c * C, (c + 1) * C) for c in chunks]
    cu = [cum[rw] for rw in rows]
    cu_last = [cum[(c + 1) * C - 1:(c + 1) * C] for c in chunks]
    e_neg = each(lambda x: jnp.exp(-x), cu)
    e_tail = each(lambda x, xl: jnp.exp(xl - x), cu, cu_last)
    a_t = each(lambda rw, x: stack(-kk[rw] * jnp.exp(x - lw[rw])), rows, cu)
    r_t = each(lambda rw, x: stack(r[rw] * jnp.exp(x)), rows, cu)
    b_t = each(lambda rw, e: stack(b[rw] * e), rows, e_neg)
    k_t = each(lambda rw, e: stack(k[rw] * e), rows, e_neg)
    v_s = each(lambda rw: stack(v[rw]), rows)
    gram = each(lambda a_, r_, b_, k_: lax.dot_general(
        jnp.concatenate([a_, r_], 0).astype(BF16), jnp.concatenate([b_, k_], 0).astype(BF16),
        (((1,), (1,)), ((), ())), preferred_element_type=F32), a_t, r_t, b_t, k_t)
    n_ab = each(lambda g: jnp.where(strict, g[:R2, :R2], 0.0), gram)
    a_ak = each(lambda g: jnp.where(strict, g[:R2, R2:], 0.0), gram)
    a_rb = each(lambda g: jnp.where(incl, g[R2:, :R2], 0.0), gram)
    a_rk = each(lambda g: jnp.where(incl, g[R2:, R2:], 0.0), gram)
    av = each(lambda ak, rk, vv: bdot(jnp.concatenate([ak, rk], 0), vv), a_ak, a_rk, v_s)
    n1 = each(lambda n: jnp.where(blk16, n, 0.0), n_ab)
    n2 = each(lambda x: bdot(x, x), n1)
    n4 = each(lambda x: bdot(x, x), n2)
    n8 = each(lambda x: bdot(x, x), n4)
    xa = each(lambda x1, x2: x1 + x2 + bdot(x1, x2), n1, n2)
    xb = each(lambda x4, x8: x4 + x8 + bdot(x4, x8), n4, n8)
    t_inv = each(lambda p, q: eye + p + q + bdot(p, q), xa, xb)
    for off in (off32, off64):
        mid = each(lambda n, t: bdot(jnp.where(off, n, 0.0), t), n_ab, t_inv)
        t_inv = each(lambda t, m_: t + bdot(t, m_), t_inv, mid)
    tw = each(lambda t, a_, av_: bdot(t, jnp.concatenate([a_, av_[:R2]], axis=1)), t_inv, a_t, av)
    bk_t = each(lambda rw, e: jnp.transpose(jnp.concatenate([stack(b[rw] * e), stack(k[rw] * e)], 0)),
                rows, e_tail)
    pc_col = each(lambda rw: jnp.exp(jnp.dot(jnp.transpose(lw[rw]), ones_c, precision=HIGHEST,
                                             preferred_element_type=F32)), rows)
    w1r = each(lambda t, r_: jnp.concatenate([t[:, :LANES], r_], 0), tw, r_t)

    h = h_ref[...]
    for c in chunks:
        x = bdot(w1r[c], h)
        u = x[:R2] + tw[c][:, LANES:]
        y_bd = x[R2:] + bdot(a_rb[c], u) + av[c][R2:]
        y_ref[0, rows[c], :] = y_bd[:C] + y_bd[C:]
        h = pc_col[c] * h + bdot(bk_t[c], jnp.concatenate([u, v_s[c]], 0))
    h_ref[...] = h


def rwkv_chunked(r, lw, k, v, kk, b):
    B, S, W = r.shape
    TT = RWKV_CHUNK_TT
    spec = pl.BlockSpec((1, TT, LANES), lambda bb, p, i: (bb, i, p))
    return pl.pallas_call(
        _rwkv_chunk_kernel,
        out_shape=jax.ShapeDtypeStruct((B, S, W), F32),
        grid=(B, W // LANES, S // TT),
        in_specs=[spec] * 6,
        out_specs=spec,
        scratch_shapes=[pltpu.VMEM((LANES, LANES), F32)],
        compiler_params=_cparams(("parallel", "parallel", "arbitrary")),
        name="rwkv7_chunked",
    )(r, lw, k, v, kk, b)


def _rwkv_post_kernel(y_ref, gate_ref, bonus_ref, g_ref, b_ref, o_ref):
    seg = _segment_ones(RWKV_WIDTH, RWKV_HEAD_DIM) * (1.0 / RWKV_HEAD_DIM)
    hdot = functools.partial(jnp.dot, precision=HIGHEST, preferred_element_type=F32)
    y = y_ref[0]
    mean = hdot(y, seg)
    d = y - mean
    var = hdot(d * d, seg)
    yn = d * lax.rsqrt(var + RWKV_GN_EPS) * g_ref[...] + b_ref[...]
    o_ref[0] = (yn + bonus_ref[0]) * gate_ref[0]


def rwkv_post(y, gate, bonus, ln_g, ln_b):
    B, S, W = y.shape
    ts = 512
    spec = pl.BlockSpec((1, ts, W), lambda b, i: (b, i, 0))
    full = pl.BlockSpec((1, W), lambda b, i: (0, 0))
    return pl.pallas_call(
        _rwkv_post_kernel,
        out_shape=jax.ShapeDtypeStruct((B, S, W), F32),
        grid=(B, S // ts),
        in_specs=[spec, spec, spec, full, full],
        out_specs=spec,
        compiler_params=_cparams(("parallel", "parallel")),
        name="rwkv7_post",
    )(y, gate, bonus, ln_g.reshape(1, W), ln_b.reshape(1, W))


def _merge_kernel(h_ref, wg0_ref, wg1_ref, wg2_ref, wg3_ref, ya_ref, yb_ref, yc_ref, yd_ref,
                  wa_ref, wb_ref, wc_ref, wd_ref, o_ref):
    h = h_ref[...]
    acc = None
    branches = ((wg0_ref, ya_ref, wa_ref), (wg1_ref, yb_ref, wb_ref), (wg2_ref, yc_ref, wc_ref),
                (wg3_ref, yd_ref, wd_ref))
    for wg_ref, y_ref, w_ref in branches:
        gate = jax.nn.sigmoid(jnp.dot(h, wg_ref[...], preferred_element_type=F32))
        proj = jnp.dot(y_ref[...].astype(BF16), w_ref[...], preferred_element_type=F32)
        acc = gate * proj if acc is None else acc + gate * proj
    o_ref[...] = acc.astype(o_ref.dtype)


def branch_merge(hb, wg, ys, ws):
    T, D = hb.shape
    tm, tn = 512, 512
    nj = D // tn
    gspecs = [pl.BlockSpec((D, tn), functools.partial(lambda j, i, br: (0, br * nj + j), br=br))
              for br in range(N_BRANCHES)]
    yspecs = [pl.BlockSpec((tm, y.shape[1]), lambda j, i: (i, 0)) for y in ys]
    wspecs = [pl.BlockSpec((w.shape[0], tn), lambda j, i: (0, j)) for w in ws]
    return pl.pallas_call(
        _merge_kernel,
        out_shape=jax.ShapeDtypeStruct((T, D), BF16),
        grid=(nj, T // tm),
        in_specs=[pl.BlockSpec((tm, D), lambda j, i: (i, 0))] + gspecs + yspecs + wspecs,
        out_specs=pl.BlockSpec((tm, tn), lambda j, i: (i, j)),
        compiler_params=_cparams(("parallel", "parallel")),
        name="branch_merge",
    )(hb, wg, wg, wg, wg, *ys, *ws)


ROUTE_TM = 256


def _layer_norm(z, g, b):
    mu = jnp.mean(z, -1, keepdims=True)
    d = z - mu
    var = jnp.mean(d * d, -1, keepdims=True)
    return d * lax.rsqrt(var + LN_EPS) * g + b


def _split_bf16(x):
    hi = x.astype(BF16)
    return hi, (x - hi.astype(F32)).astype(BF16)


def _out_ln_router_kernel(m_ref, wo_ref, x_ref, gate_ref, g_ref, b_ref, sc_ref, sh_ref, rwh_ref, rwl_ref, rb_ref,
                          x1_ref, h2_ref, idx_ref, wt_ref, rank_ref, cnt_ref):
    y = jnp.dot(m_ref[...], wo_ref[...], preferred_element_type=F32)
    x1 = _layer_norm(DEEPNORM_ALPHA * x_ref[...] + gate_ref[0] * y, g_ref[...], b_ref[...])
    x1_ref[...] = x1
    h2 = x1 * (1.0 + sc_ref[0]) + sh_ref[0]
    h2_hi, h2_lo = _split_bf16(h2)
    h2_ref[...] = h2_hi
    logits = (jnp.dot(h2_hi, rwh_ref[...], preferred_element_type=F32)
              + jnp.dot(h2_hi, rwl_ref[...], preferred_element_type=F32)
              + jnp.dot(h2_lo, rwh_ref[...], preferred_element_type=F32)) + rb_ref[...]
    lane = lax.broadcasted_iota(jnp.int32, logits.shape, 1).astype(F32)
    vals, idxs = [], []
    cur = logits
    for _ in range(TOP_K):
        m = jnp.max(cur, -1, keepdims=True)
        ix = jnp.min(jnp.where(cur == m, lane, float(LANES)), -1, keepdims=True)
        vals.append(m)
        idxs.append(ix)
        cur = jnp.where(lane == ix, -jnp.inf, cur)
    es = [jnp.exp(v - vals[0]) for v in vals]
    tot = es[0] + es[1] + es[2] + es[3]
    sel = jnp.zeros_like(logits)
    for kq in range(TOP_K):
        sel = jnp.where(lane == idxs[kq], 1.0, sel)
    tm = logits.shape[0]
    ti = lax.broadcasted_iota(jnp.int32, (tm, tm), 0)
    si = lax.broadcasted_iota(jnp.int32, (tm, tm), 1)
    before = jnp.dot((si < ti).astype(BF16), sel.astype(BF16), preferred_element_type=F32)
    idx_o = jnp.zeros_like(logits)
    wt_o = jnp.zeros_like(logits)
    rank_o = jnp.zeros_like(logits)
    for kq in range(TOP_K):
        rank_k = jnp.sum(jnp.where(lane == idxs[kq], before, 0.0), -1, keepdims=True)
        idx_o = jnp.where(lane == kq, idxs[kq], idx_o)
        wt_o = jnp.where(lane == kq, es[kq] / tot, wt_o)
        rank_o = jnp.where(lane == kq, rank_k, rank_o)
    idx_ref[...] = idx_o.astype(jnp.int32)
    wt_ref[...] = wt_o
    rank_ref[...] = rank_o.astype(jnp.int32)
    cnt = jnp.sum(sel, axis=0, keepdims=True)
    cnt_ref[0] = jnp.broadcast_to(cnt, (SUBLANES, LANES)).astype(jnp.int32)


def out_ln_router(merged, w_out_b, x2d, gate1, ln_g, ln_b, scale2, shift2, router_w, router_b, S):
    T, D = x2d.shape
    tm = ROUTE_TM
    nt = T // tm
    spb = S // tm
    rw = jnp.zeros((D, LANES), F32).at[:, :N_EXPERTS].set(router_w)
    rw_hi, rw_lo = _split_bf16(rw)
    rb = jnp.full((1, LANES), NEG_BIG, F32).at[0, :N_EXPERTS].set(router_b)
    rowblk = lambda w: pl.BlockSpec((tm, w), lambda i: (i, 0))
    full = lambda r, c: pl.BlockSpec((r, c), lambda i: (0, 0))
    perb = pl.BlockSpec((1, 1, D), lambda i: (i // spb, 0, 0))
    tile_spec = pl.BlockSpec((1, SUBLANES, LANES), lambda i: (i, 0, 0))
    tile_shape = jax.ShapeDtypeStruct((nt, SUBLANES, LANES), jnp.int32)
    return pl.pallas_call(
        _out_ln_router_kernel,
        out_shape=(jax.ShapeDtypeStruct((T, D), F32), jax.ShapeDtypeStruct((T, D), BF16),
                   jax.ShapeDtypeStruct((T, LANES), jnp.int32), jax.ShapeDtypeStruct((T, LANES), F32),
                   jax.ShapeDtypeStruct((T, LANES), jnp.int32), tile_shape),
        grid=(nt,),
        in_specs=[rowblk(D), full(D, D), rowblk(D), perb, full(1, D), full(1, D), perb, perb,
                  full(D, LANES), full(D, LANES), full(1, LANES)],
        out_specs=(rowblk(D), rowblk(D), rowblk(LANES), rowblk(LANES), rowblk(LANES), tile_spec),
        compiler_params=_cparams(("parallel",)),
        name="out_proj_ln_router",
    )(merged, w_out_b, x2d, gate1, ln_g.reshape(1, D), ln_b.reshape(1, D), scale2, shift2, rw_hi, rw_lo, rb)


PIECE = 16
TILE_ROWS = 1536
BLOCK_PIECES = MOE_BM // PIECE


def _dispatch_kernel(lrow_ref, h_ref, o_ref):
    lrow = lrow_ref[0]
    row = lax.broadcasted_iota(jnp.int32, (TILE_ROWS, lrow.shape[1]), 0)
    hit = row == lrow[0:1, :]
    for kq in range(1, TOP_K):
        hit = jnp.logical_or(hit, row == lrow[kq:kq + 1, :])
    onehot = jnp.where(hit, 1.0, 0.0).astype(BF16)
    o_ref[...] = jnp.dot(onehot, h_ref[...], preferred_element_type=F32).astype(o_ref.dtype)


def moe_dispatch(h2, lrow_t):
    T, D = h2.shape
    nt = T // ROUTE_TM
    return pl.pallas_call(
        _dispatch_kernel,
        out_shape=jax.ShapeDtypeStruct((nt * TILE_ROWS, D), BF16),
        grid=(nt,),
        in_specs=[pl.BlockSpec((1, SUBLANES, ROUTE_TM), lambda i: (i, 0, 0)),
                  pl.BlockSpec((ROUTE_TM, D), lambda i: (i, 0))],
        out_specs=pl.BlockSpec((TILE_ROWS, D), lambda i: (i, 0)),
        compiler_params=_cparams(("parallel",)),
        name="moe_dispatch",
    )(lrow_t, h2)


def _expert_kernel(be_ref, nu_ref, pr_ref, x_hbm, wgu_ref, bgu_ref, wd_ref, bd_ref, yin_hbm, y_hbm,
                   xs_ref, ys_ref, isem, osem):
    del yin_hbm
    i = pl.program_id(0)
    nu = nu_ref[0]
    slot = i % 2

    def piece_row(blk, k):
        return pr_ref[blk * BLOCK_PIECES + k]

    def in_copy(blk, k, s):
        src = pl.multiple_of(jnp.maximum(piece_row(blk, k), 0), PIECE)
        return pltpu.make_async_copy(x_hbm.at[pl.ds(src, PIECE), :], xs_ref.at[s, pl.ds(k * PIECE, PIECE), :],
                                     isem.at[s])

    def out_copy(blk, k, s):
        dst = pl.multiple_of(jnp.maximum(piece_row(blk, k), 0), PIECE)
        return pltpu.make_async_copy(ys_ref.at[s, pl.ds(k * PIECE, PIECE), :], y_hbm.at[pl.ds(dst, PIECE), :],
                                     osem.at[s])

    def for_pieces(blk, fn):
        for k in range(BLOCK_PIECES):
            @pl.when(piece_row(blk, k) >= 0)
            def _():
                fn(k)

    @pl.when(i == 0)
    def _():
        xs_ref[...] = jnp.zeros_like(xs_ref)
        for_pieces(0, lambda k: in_copy(0, k, 0).start())

    @pl.when(i + 1 < nu)
    def _():
        for_pieces(i + 1, lambda k: in_copy(i + 1, k, 1 - slot).start())

    @pl.when(i < nu)
    def _():
        for_pieces(i, lambda k: in_copy(i, k, slot).wait())
        gu = jnp.dot(xs_ref[slot], wgu_ref[0], preferred_element_type=F32) + bgu_ref[0]
        glu = jnp.minimum(gu[:, :EXPERT_FF], SWIGLU_LIMIT)
        lin = jnp.clip(gu[:, EXPERT_FF:], -SWIGLU_LIMIT, SWIGLU_LIMIT)
        act = glu * jax.nn.sigmoid(SWIGLU_ALPHA * glu) * (lin + 1.0)

        @pl.when(i >= 2)
        def _():
            for_pieces(i - 2, lambda k: out_copy(i - 2, k, slot).wait())

        y = jnp.dot(act.astype(BF16), wd_ref[0], preferred_element_type=F32) + bd_ref[0]
        ys_ref[slot] = y.astype(ys_ref.dtype)
        for_pieces(i, lambda k: out_copy(i, k, slot).start())

    @pl.when(i == nu - 1)
    def _():
        @pl.when(i >= 1)
        def _():
            for_pieces(i - 1, lambda k: out_copy(i - 1, k, 1 - slot).wait())
        for_pieces(i, lambda k: out_copy(i, k, slot).wait())


def expert_ffn(block_e, n_used, piece_rows, xbt, wgu, bgu, wd, bd):
    rows, D = xbt.shape
    E, _, F2 = wgu.shape
    nblk = block_e.shape[0]
    grid_spec = pltpu.PrefetchScalarGridSpec(
        num_scalar_prefetch=3,
        grid=(nblk,),
        in_specs=[pl.BlockSpec(memory_space=pl.ANY),
                  pl.BlockSpec((1, D, F2), lambda i, be, nu, pr: (be[i], 0, 0)),
                  pl.BlockSpec((1, 1, F2), lambda i, be, nu, pr: (be[i], 0, 0)),
                  pl.BlockSpec((1, F2 // 2, D), lambda i, be, nu, pr: (be[i], 0, 0)),
                  pl.BlockSpec((1, 1, D), lambda i, be, nu, pr: (be[i], 0, 0)),
                  pl.BlockSpec(memory_space=pl.ANY)],
        out_specs=pl.BlockSpec(memory_space=pl.ANY),
        scratch_shapes=[pltpu.VMEM((2, MOE_BM, D), BF16), pltpu.VMEM((2, MOE_BM, D), BF16),
                        pltpu.SemaphoreType.DMA((2,)), pltpu.SemaphoreType.DMA((2,))],
    )
    return pl.pallas_call(
        _expert_kernel,
        out_shape=jax.ShapeDtypeStruct((rows, D), BF16),
        grid_spec=grid_spec,
        input_output_aliases={8: 0},
        compiler_params=_cparams(("arbitrary",)),
        name="expert_ffn",
    )(block_e, n_used, piece_rows, xbt, wgu, bgu.reshape(E, 1, F2), wd, bd.reshape(E, 1, D),
      jnp.zeros((rows, D), BF16))


COMBINE_COLS = 512


def _combine_kernel(meta_ref, wt_ref, x1_ref, gate_ref, g_ref, b_ref, sc_ref, sh_ref, y_ref,
                    x2_ref, hb_ref, acc_ref):
    meta = meta_ref[...]
    wts = wt_ref[...]
    CW = COMBINE_COLS
    col0 = lax.broadcasted_iota(jnp.int32, (meta.shape[0], CW), 1)
    for cc in range(TILE_ROWS // CW):
        hi = jnp.zeros(col0.shape, F32)
        lo = jnp.zeros(col0.shape, F32)
        for kq in range(TOP_K):
            hit = col0 == meta[:, kq:kq + 1] - cc * CW
            w = wts[:, kq:kq + 1]
            w_hi = w.astype(BF16).astype(F32)
            hi = jnp.where(hit, w_hi, hi)
            lo = jnp.where(hit, w - w_hi, lo)
        rows = y_ref[pl.ds(cc * CW, CW), :]
        part = (jnp.dot(hi.astype(BF16), rows, preferred_element_type=F32)
                + jnp.dot(lo.astype(BF16), rows, preferred_element_type=F32))
        if cc == 0:
            acc_ref[...] = part
        else:
            acc_ref[...] += part

    x2 = _layer_norm(DEEPNORM_ALPHA * x1_ref[...] + gate_ref[0] * acc_ref[...], g_ref[...], b_ref[...])
    x2_ref[...] = x2
    hb_ref[...] = (x2 * (1.0 + sc_ref[0]) + sh_ref[0]).astype(hb_ref.dtype)


def moe_combine_ln(ybt, meta, wt, x1, gate2, ln_g, ln_b, scale_next, shift_next, S):
    T, D = x1.shape
    tm = ROUTE_TM
    spb = S // tm
    rowblk = lambda w: pl.BlockSpec((tm, w), lambda i: (i, 0))
    full = pl.BlockSpec((1, D), lambda i: (0, 0))
    perb = pl.BlockSpec((1, 1, D), lambda i: (i // spb, 0, 0))
    return pl.pallas_call(
        _combine_kernel,
        out_shape=(jax.ShapeDtypeStruct((T, D), F32), jax.ShapeDtypeStruct((T, D), BF16)),
        grid=(T // tm,),
        in_specs=[rowblk(LANES), rowblk(LANES), rowblk(D), perb, full, full, perb, perb,
                  pl.BlockSpec((TILE_ROWS, D), lambda i: (i, 0))],
        out_specs=(rowblk(D), rowblk(D)),
        scratch_shapes=[pltpu.VMEM((tm, D), F32)],
        compiler_params=_cparams(("parallel",)),
        name="moe_combine_ln",
    )(meta, wt, x1, gate2, ln_g.reshape(1, D), ln_b.reshape(1, D), scale_next, shift_next, ybt)


def routed_ffn_ln(h2, idx, wt, rank, tile_cnt, wgu, bgu, wd, bd, x1, gate2, ln_g, ln_b,
                  scale_next, shift_next, S):
    T, D = h2.shape
    E = N_EXPERTS
    nt = T // ROUTE_TM
    i32 = jnp.int32
    e4 = idx[:, :TOP_K]
    cnt = tile_cnt[:, 0, :E]
    npc = (cnt + PIECE - 1) // PIECE
    seg_off = (jnp.cumsum(npc, axis=1) - npc) * PIECE
    onehot = e4[:, :, None] == jnp.arange(E, dtype=i32)[None, None, :]
    off_tok = jnp.repeat(seg_off, ROUTE_TM, axis=0)[:, None, :]
    lrow = jnp.sum(jnp.where(onehot, off_tok, 0), axis=-1) + rank[:, :TOP_K]
    per_e = jnp.sum(npc, axis=0)
    per_e_pad = ((per_e + BLOCK_PIECES - 1) // BLOCK_PIECES) * BLOCK_PIECES
    e_end = jnp.cumsum(per_e_pad)
    e_start = e_end - per_e_pad
    t_end = jnp.cumsum(npc, axis=0).T
    t_start = t_end - npc.T
    max_pieces = nt * ((ROUTE_TM * TOP_K + E * (PIECE - 1)) // PIECE) + E * (BLOCK_PIECES - 1)
    nblk = -(-max_pieces // BLOCK_PIECES)
    q = jnp.arange(nblk * BLOCK_PIECES, dtype=i32)
    e_q = jnp.minimum(jnp.sum((e_end[None, :] <= q[:, None]).astype(i32), axis=1), E - 1)
    sel_e = e_q[:, None] == jnp.arange(E, dtype=i32)[None, :]
    of_expert = lambda table: jnp.sum(jnp.where(sel_e, table[None, :], 0), axis=1)
    w_q = q - of_expert(e_start)
    real = w_q < of_expert(per_e)
    rows_of = lambda table: jnp.sum(jnp.where(sel_e[:, :, None], table[None], 0), axis=1)
    t_q = jnp.minimum(jnp.sum((rows_of(t_end) <= w_q[:, None]).astype(i32), axis=1), nt - 1)
    sel_t = t_q[:, None] == jnp.arange(nt, dtype=i32)[None, :]
    of_tile = lambda table: jnp.sum(jnp.where(sel_t, rows_of(table), 0), axis=1)
    k_q = w_q - of_tile(t_start)
    piece_rows = jnp.where(real, t_q * TILE_ROWS + of_tile(seg_off.T) + k_q * PIECE, -1).astype(i32)
    block_e = e_q[::BLOCK_PIECES]
    n_used = (e_end[-1] // BLOCK_PIECES).astype(i32).reshape(1)

    lrow_t = jnp.full((nt, SUBLANES, ROUTE_TM), -1, i32).at[:, :TOP_K, :].set(
        lrow.reshape(nt, ROUTE_TM, TOP_K).transpose(0, 2, 1))
    xbt = moe_dispatch(h2, lrow_t)
    ybt = expert_ffn(block_e, n_used, piece_rows, xbt, wgu, bgu, wd, bd)
    meta = jnp.full((T, LANES), -1, i32).at[:, :TOP_K].set(lrow)
    return moe_combine_ln(ybt, meta, wt, x1, gate2, ln_g, ln_b, scale_next, shift_next, S)


def _mix_weights(w_in_l):
    def cols(rng, pad=0):
        part = w_in_l[:, rng[0]:rng[1]]
        if pad:
            part = jnp.pad(part, ((0, 0), (0, pad)))
        return part
    w_mix = jnp.concatenate([
        cols(SRC_POOL), cols(SRC_GLA_V), cols(SRC_GLA_G), cols(SRC_RWKV_RKV), cols(SRC_ATT),
        cols(SRC_GLA_Q), cols(SRC_GLA_K),
        cols(SRC_RWKV_LO, RWKV_LO_PAD - (SRC_RWKV_LO[1] - SRC_RWKV_LO[0])),
        cols(SRC_GLA_A, GLA_A_PAD - (SRC_GLA_A[1] - SRC_GLA_A[0]))], axis=1).astype(BF16)
    w_gates = w_in_l[:, SRC_GATES[0]:].astype(BF16)
    return w_mix, w_gates


def kernel(x, c, ada_w, ada_b, w_in, pool_w, pool_scale, gla_w_alpha, gla_b_alpha, gla_norm_g, rwkv_mu, rwkv_w0, rwkv_w2, rwkv_a0, rwkv_a2, rwkv_g2, rwkv_k_k, rwkv_k_a, rwkv_r_k, rwkv_ln_g, rwkv_ln_b, w_branch_a, w_branch_b, w_branch_c, w_branch_d, w_out, ln1_g, ln1_b, router_w, router_b, w_gate_up, b_gate_up, w_down, b_down, ln2_g, ln2_b):
    B, S, D = x.shape
    T = B * S
    mod = ada_modulation(c, ada_w, ada_b)
    mods = [[mod[l, :, None, i * D:(i + 1) * D] for i in range(6)] for l in range(DEPTH)]
    x2d = x.reshape(T, D)
    hb2d = modulate(x, mods[0][1], mods[0][0]).reshape(T, D)
    for l in range(DEPTH):
        shift1, scale1, gate1, shift2, scale2, gate2 = mods[l]
        shift_next, scale_next = (mods[l + 1][0], mods[l + 1][1]) if l + 1 < DEPTH else (shift1, scale1)
        w_mix, w_gates = _mix_weights(w_in[l])
        P = matmul(hb2d, w_mix, 1024, MIX_TN).reshape(B, S, MIX_WIDTH)
        y_a = pool_mixer(P, pool_w[l], pool_scale[l])
        y_b = dilated_attention(P)
        y_c = gla_mixer(P, gla_w_alpha[l], gla_b_alpha[l], gla_norm_g[l])
        r_, w_, k_, v_, kk_, b_, gate_, bonus_ = rwkv_prep(
            P, rwkv_mu[l], rwkv_w0[l], rwkv_w2[l], rwkv_a0[l], rwkv_a2[l], rwkv_g2[l],
            rwkv_k_k[l], rwkv_k_a[l], rwkv_r_k[l].reshape(-1))
        y_t = rwkv_chunked(r_, w_, k_, v_, kk_, b_)
        y_d = rwkv_post(y_t, gate_, bonus_, rwkv_ln_g[l], rwkv_ln_b[l])
        ys = [y.reshape(T, -1) for y in (y_a, y_b, y_c, y_d)]
        ws = [w[l].astype(BF16) for w in (w_branch_a, w_branch_b, w_branch_c, w_branch_d)]
        merged = branch_merge(hb2d, w_gates, ys, ws)
        x1, h2, idx, wt, rank, tile_cnt = out_ln_router(
            merged, w_out[l].astype(BF16), x2d, gate1, ln1_g[l], ln1_b[l], scale2, shift2,
            router_w[l], router_b[l], S)
        x2d, hb2d = routed_ffn_ln(h2, idx, wt, rank, tile_cnt, w_gate_up[l].astype(BF16), b_gate_up[l],
                                  w_down[l].astype(BF16), b_down[l], x1, gate2, ln2_g[l], ln2_b[l],
                                  scale_next, shift_next, S)
    return x2d.reshape(B, S, D)
```

```python
import functools

import numpy as np
import jax
import jax.numpy as jnp
from jax import lax
from jax.experimental import pallas as pl
from jax.experimental.pallas import tpu as pltpu

F32 = jnp.float32
BF16 = jnp.bfloat16
HIGHEST = lax.Precision.HIGHEST

D_MODEL = 2048
DEPTH = 2
POOL_WINDOWS = (2, 4, 8, 16)
POOL_GROUP = 128
POOL_WIDTH = 512
ATT_GROUPS = ((128, 1), (512, 4), (2048, 16))
ATT_HEAD_DIM = 64
ATT_HEADS = 12
ATT_WIDTH = 768
ATT_OUT_WIDTH = 256
ATT_BLOCK = 128
ALIBI_SLOPES = tuple(2.0 ** (-8.0 * (h + 1) / ATT_HEADS) for h in range(ATT_HEADS))
GLA_HEADS = 4
GLA_DK = 64
GLA_DV = 128
GLA_KEY_WIDTH = 256
GLA_VAL_WIDTH = 512
GLA_GATE_RANK = 16
GLA_GATE_TEMP = 16.0
GLA_CHUNK = 32
GLA_NORM_EPS = 1e-6
RWKV_HEADS = 8
RWKV_HEAD_DIM = 64
RWKV_WIDTH = 512
RWKV_DECAY_RANK = 32
RWKV_AAA_RANK = 32
RWKV_GATE_RANK = 96
RWKV_GN_EPS = 64e-5
N_BRANCHES = 4
N_EXPERTS = 32
TOP_K = 4
EXPERT_FF = 1024
SWIGLU_LIMIT = 7.0
SWIGLU_ALPHA = 1.702
LN_EPS = 1e-5
DEEPNORM_ALPHA = (2 * DEPTH) ** 0.25

LANES = 128
SUBLANES = 8
VMEM_LIMIT = 56 * 1024 * 1024

SRC_POOL = (0, 512)
SRC_ATT = (512, 2816)
SRC_GLA_Q = (2816, 3072)
SRC_GLA_K = (3072, 3328)
SRC_GLA_V = (3328, 3840)
SRC_GLA_G = (3840, 4352)
SRC_GLA_A = (4352, 4368)
SRC_RWKV_RKV = (4368, 5904)
SRC_RWKV_LO = (5904, 6064)
SRC_GATES = (6064, 14256)
OFF_POOL = 0
OFF_GLA_V = 512
OFF_GLA_G = 1024
OFF_RWKV_RKV = 1536
OFF_ATT_Q = 3072
OFF_ATT_K = OFF_ATT_Q + ATT_WIDTH
OFF_ATT_V = OFF_ATT_K + ATT_WIDTH
OFF_GLA_Q = 5376
OFF_GLA_K = 5632
OFF_RWKV_LO = 5888
RWKV_LO_PAD = 256
OFF_GLA_A = 6144
GLA_A_PAD = 128
MIX_WIDTH = 6272
MIX_TN = 896

MOE_BM = 256
NEG_BIG = -1e30


def _cparams(sem):
    return pltpu.CompilerParams(dimension_semantics=sem, vmem_limit_bytes=VMEM_LIMIT)


def _ada_kernel(c_ref, w_ref, b_ref, o_ref):
    c = c_ref[...]
    s = c * jax.nn.sigmoid(c)
    o_ref[0] = jnp.dot(s, w_ref[0], precision=HIGHEST, preferred_element_type=F32) + b_ref[0]


def ada_modulation(c, ada_w, ada_b):
    L, D, N = ada_w.shape
    B = c.shape[0]
    cp = jnp.zeros((SUBLANES, D), F32).at[:B].set(c)
    tn = 1024
    out = pl.pallas_call(
        _ada_kernel,
        out_shape=jax.ShapeDtypeStruct((L, SUBLANES, N), F32),
        grid=(L, N // tn),
        in_specs=[pl.BlockSpec((SUBLANES, D), lambda l, j: (0, 0)),
                  pl.BlockSpec((1, D, tn), lambda l, j: (l, 0, j)),
                  pl.BlockSpec((1, 1, tn), lambda l, j: (l, 0, j))],
        out_specs=pl.BlockSpec((1, SUBLANES, tn), lambda l, j: (l, 0, j)),
        compiler_params=_cparams(("parallel", "parallel")),
        name="ada_modulation",
    )(cp, ada_w, ada_b.reshape(L, 1, N))
    return out[:, :B]


def _modulate_kernel(x_ref, sc_ref, sh_ref, o_ref):
    o_ref[0] = (x_ref[0] * (1.0 + sc_ref[0]) + sh_ref[0]).astype(o_ref.dtype)


def modulate(x, scale, shift):
    B, S, D = x.shape
    ts = 1024
    return pl.pallas_call(
        _modulate_kernel,
        out_shape=jax.ShapeDtypeStruct((B, S, D), BF16),
        grid=(B, S // ts),
        in_specs=[pl.BlockSpec((1, ts, D), lambda b, i: (b, i, 0)),
                  pl.BlockSpec((1, 1, D), lambda b, i: (b, 0, 0)),
                  pl.BlockSpec((1, 1, D), lambda b, i: (b, 0, 0))],
        out_specs=pl.BlockSpec((1, ts, D), lambda b, i: (b, i, 0)),
        compiler_params=_cparams(("parallel", "parallel")),
        name="modulate",
    )(x, scale, shift)


def _mm_kernel(x_ref, w_ref, o_ref):
    o_ref[...] = jnp.dot(x_ref[...], w_ref[...], preferred_element_type=F32).astype(o_ref.dtype)


def matmul(x, w, tm, tn, out_dtype=F32):
    M, K = x.shape
    N = w.shape[1]
    return pl.pallas_call(
        _mm_kernel,
        out_shape=jax.ShapeDtypeStruct((M, N), out_dtype),
        grid=(N // tn, M // tm),
        in_specs=[pl.BlockSpec((tm, K), lambda j, i: (i, 0)),
                  pl.BlockSpec((K, tn), lambda j, i: (0, j))],
        out_specs=pl.BlockSpec((tm, tn), lambda j, i: (i, j)),
        compiler_params=_cparams(("parallel", "parallel")),
        name="matmul",
    )(x, w)


def _pool_kernel(p_ref, w_ref, sc_ref, o_ref):
    g = pl.program_id(1)
    v = p_ref[0]
    S = v.shape[0]
    row = lax.broadcasted_iota(jnp.int32, v.shape, 0)
    win = jnp.left_shift(2, g)
    s = v
    pooled_sum = v
    for k, sh in enumerate((1, 2, 4, 8)):
        s = s + jnp.where(row >= sh, pltpu.roll(s, sh, 0), 0.0)
        pooled_sum = jnp.where(g >= k, s, pooled_sum)
    cnt = jnp.minimum(row + 1, win).astype(F32)
    diff = pooled_sum / cnt - v
    y = jnp.dot(diff.astype(BF16), w_ref[0].astype(BF16), preferred_element_type=F32)
    o_ref[0] = y * sc_ref[...]


def pool_mixer(P, pool_w, pool_scale):
    B, S, _ = P.shape
    G = len(POOL_WINDOWS)
    return pl.pallas_call(
        _pool_kernel,
        out_shape=jax.ShapeDtypeStruct((B, S, POOL_WIDTH), F32),
        grid=(B, G),
        in_specs=[pl.BlockSpec((1, S, POOL_GROUP), lambda b, g: (b, 0, OFF_POOL // POOL_GROUP + g)),
                  pl.BlockSpec((1, POOL_GROUP, POOL_GROUP), lambda b, g: (g, 0, 0)),
                  pl.BlockSpec((1, POOL_GROUP), lambda b, g: (0, g))],
        out_specs=pl.BlockSpec((1, S, POOL_GROUP), lambda b, g: (b, 0, g)),
        compiler_params=_cparams(("parallel", "parallel")),
        name="pool_mixer",
    )(P, pool_w, pool_scale.reshape(1, POOL_WIDTH))


def _att_kernel(q_ref, kc_ref, kp_ref, vc_ref, vp_ref, num_ref, den_ref, m_ref, *, slopes, dilation):
    hp = pl.program_id(1)
    n = pl.program_id(2)
    i = lax.broadcasted_iota(jnp.int32, (ATT_BLOCK, ATT_BLOCK), 0)
    j = lax.broadcasted_iota(jnp.int32, (ATT_BLOCK, ATT_BLOCK), 1)
    dist_c = i - j
    dist_p = dist_c + ATT_BLOCK
    valid_c = dist_c >= 0
    valid_p = jnp.logical_and(dist_p <= ATT_BLOCK, n > 0)
    lane_lo = lax.broadcasted_iota(jnp.int32, (ATT_BLOCK, LANES), 1) < ATT_HEAD_DIM
    dn = (((1,), (1,)), ((), ()))

    def residue(r, carry):
        rows = slice(None) if dilation == 1 else pl.ds(r, ATT_BLOCK, stride=dilation)
        q2, kc2, kp2, vc2, vp2 = (ref[0, rows, :] for ref in (q_ref, kc_ref, kp_ref, vc_ref, vp_ref))
        nums, dens, ms = [], [], []
        for h in range(2):
            sl = slice(h * ATT_HEAD_DIM, (h + 1) * ATT_HEAD_DIM)
            slope = jnp.where(hp == 0, slopes[h], slopes[2 + h]) * float(dilation)
            q = (q2[:, sl] * (ATT_HEAD_DIM ** -0.5)).astype(BF16)
            s_c = lax.dot_general(q, kc2[:, sl].astype(BF16), dn, preferred_element_type=F32)
            s_p = lax.dot_general(q, kp2[:, sl].astype(BF16), dn, preferred_element_type=F32)
            s_c = jnp.where(valid_c, s_c - slope * dist_c.astype(F32), NEG_BIG)
            s_p = jnp.where(valid_p, s_p - slope * dist_p.astype(F32), NEG_BIG)
            m = jnp.maximum(jnp.max(s_c, -1, keepdims=True), jnp.max(s_p, -1, keepdims=True))
            p_c = jnp.exp(s_c - m)
            p_p = jnp.exp(s_p - m)
            dens.append(jnp.sum(p_c, -1, keepdims=True) + jnp.sum(p_p, -1, keepdims=True))
            nums.append(jnp.dot(p_c.astype(BF16), vc2[:, sl].astype(BF16), preferred_element_type=F32)
                        + jnp.dot(p_p.astype(BF16), vp2[:, sl].astype(BF16), preferred_element_type=F32))
            ms.append(m)
        num_ref[0, rows, :] = jnp.concatenate(nums, axis=-1)
        den_ref[0, rows, :] = jnp.where(lane_lo, dens[0], dens[1])
        m_ref[0, rows, :] = jnp.where(lane_lo, ms[0], ms[1])
        return carry

    if dilation == 1:
        residue(0, 0)
    else:
        lax.fori_loop(0, dilation, residue, 0)


def att_group(P, g, dilation):
    B, S, NP = P.shape
    rows = ATT_BLOCK * dilation
    nb = S // rows
    qo, ko, vo = (o // LANES + g * 2 for o in (OFF_ATT_Q, OFF_ATT_K, OFF_ATT_V))
    blk = (1, rows, LANES)
    cur = lambda off: pl.BlockSpec(blk, lambda b, hp, n: (b, n, off + hp))
    prev = lambda off: pl.BlockSpec(blk, lambda b, hp, n: (b, jnp.maximum(n - 1, 0), off + hp))
    ospec = pl.BlockSpec(blk, lambda b, hp, n: (b, n, hp))
    oshape = jax.ShapeDtypeStruct((B, S, ATT_OUT_WIDTH), F32)
    slopes = ALIBI_SLOPES[g * 4:(g + 1) * 4]
    return pl.pallas_call(
        functools.partial(_att_kernel, slopes=slopes, dilation=dilation),
        out_shape=(oshape, oshape, oshape),
        grid=(B, 2, nb),
        in_specs=[cur(qo), cur(ko), prev(ko), cur(vo), prev(vo)],
        out_specs=(ospec, ospec, ospec),
        compiler_params=_cparams(("parallel", "parallel", "arbitrary")),
        name=f"dilated_attention_g{g}",
    )(P, P, P, P, P)


def _att_merge_kernel(*refs):
    o_ref = refs[-1]
    nums, dens, ms = refs[0:3], refs[3:6], refs[6:9]
    mx = jnp.maximum(jnp.maximum(ms[0][0], ms[1][0]), ms[2][0])
    num = jnp.zeros_like(mx)
    den = jnp.zeros_like(mx)
    for g in range(3):
        e = jnp.exp(ms[g][0] - mx)
        num = num + nums[g][0] * e
        den = den + dens[g][0] * e
    o_ref[0] = num / den


def dilated_attention(P):
    B, S, _ = P.shape
    parts = [att_group(P, g, d) for g, (_, d) in enumerate(ATT_GROUPS)]
    args = [p[0] for p in parts] + [p[1] for p in parts] + [p[2] for p in parts]
    ts = 1024
    spec = pl.BlockSpec((1, ts, ATT_OUT_WIDTH), lambda b, i: (b, i, 0))
    return pl.pallas_call(
        _att_merge_kernel,
        out_shape=jax.ShapeDtypeStruct((B, S, ATT_OUT_WIDTH), F32),
        grid=(B, S // ts),
        in_specs=[spec] * 9,
        out_specs=spec,
        compiler_params=_cparams(("parallel", "parallel")),
        name="dilated_attention_merge",
    )(*args)


GLA_TS = 256


def _gla_kernel(q_ref, k_ref, v_ref, g_ref, a_ref, wa_ref, ba_ref, ng_ref, o_ref, st_ref):
    @pl.when(pl.program_id(1) == 0)
    def _():
        st_ref[...] = jnp.zeros_like(st_ref)

    TS = GLA_TS
    nchunk = TS // GLA_CHUNK
    logit = jnp.dot(a_ref[0], wa_ref[...], precision=HIGHEST, preferred_element_type=F32) + ba_ref[...]
    log_a = (jnp.minimum(logit, 0.0) - jnp.log1p(jnp.exp(-jnp.abs(logit)))) / GLA_GATE_TEMP
    i = lax.broadcasted_iota(jnp.int32, (TS, TS), 0)
    j = lax.broadcasted_iota(jnp.int32, (TS, TS), 1)
    same_chunk_causal = jnp.logical_and(i // GLA_CHUNK == j // GLA_CHUNK, j <= i)
    tri = same_chunk_causal.astype(F32)
    bcum = jnp.dot(tri, log_a, precision=HIGHEST, preferred_element_type=F32)
    eb = jnp.exp(bcum)
    qd = q_ref[0] * (GLA_DK ** -0.5) * eb
    kd = k_ref[0] * jnp.exp(-bcum)
    k_all = k_ref[0]
    v_all = v_ref[0]
    dn_nt = (((1,), (1,)), ((), ()))
    dn_tn = (((0,), (0,)), ((), ()))
    for h in range(GLA_HEADS):
        ks = slice(h * GLA_DK, (h + 1) * GLA_DK)
        vs = slice(h * GLA_DV, (h + 1) * GLA_DV)
        qh = qd[:, ks].astype(BF16)
        vh = v_all[:, vs].astype(BF16)
        scores = lax.dot_general(qh, kd[:, ks].astype(BF16), dn_nt, preferred_element_type=F32)
        scores = jnp.where(same_chunk_causal, scores, 0.0)
        o_intra = jnp.dot(scores.astype(BF16), vh, preferred_element_type=F32)
        st = st_ref[h]
        outs = []
        for c in range(nchunk):
            rs = slice(c * GLA_CHUNK, (c + 1) * GLA_CHUNK)
            last = (c + 1) * GLA_CHUNK - 1
            o_c = o_intra[rs] + lax.dot_general(qh[rs], st.astype(BF16), dn_nt, preferred_element_type=F32)
            outs.append(o_c)
            b_last = bcum[last:last + 1, ks]
            k_tail = (k_all[rs, ks] * jnp.exp(b_last - bcum[rs, ks])).astype(BF16)
            st = st * jnp.exp(b_last) + lax.dot_general(vh[rs], k_tail, dn_tn, preferred_element_type=F32)
        st_ref[h] = st
        o = jnp.concatenate(outs, axis=0)
        o = o * lax.rsqrt(jnp.mean(o * o, -1, keepdims=True) + GLA_NORM_EPS)
        gg = g_ref[0][:, vs]
        o_ref[0, :, vs] = o * ng_ref[:, vs] * (gg * jax.nn.sigmoid(gg))


def gla_mixer(P, w_alpha, b_alpha, norm_g):
    B, S, _ = P.shape
    TS = GLA_TS
    wa = jnp.zeros((GLA_A_PAD, GLA_KEY_WIDTH), F32).at[:GLA_GATE_RANK].set(w_alpha)
    col = lambda off, w: pl.BlockSpec((1, TS, w), lambda b, i: (b, i, off // w))
    full = lambda r, c: pl.BlockSpec((r, c), lambda b, i: (0, 0))
    return pl.pallas_call(
        _gla_kernel,
        out_shape=jax.ShapeDtypeStruct((B, S, GLA_VAL_WIDTH), F32),
        grid=(B, S // TS),
        in_specs=[col(OFF_GLA_Q, GLA_KEY_WIDTH), col(OFF_GLA_K, GLA_KEY_WIDTH),
                  col(OFF_GLA_V, GLA_VAL_WIDTH), col(OFF_GLA_G, GLA_VAL_WIDTH),
                  col(OFF_GLA_A, GLA_A_PAD),
                  full(GLA_A_PAD, GLA_KEY_WIDTH), full(1, GLA_KEY_WIDTH), full(1, GLA_VAL_WIDTH)],
        out_specs=pl.BlockSpec((1, TS, GLA_VAL_WIDTH), lambda b, i: (b, i, 0)),
        scratch_shapes=[pltpu.VMEM((GLA_HEADS, GLA_DV, GLA_DK), F32)],
        compiler_params=_cparams(("parallel", "arbitrary")),
        name="gla_mixer",
    )(P, P, P, P, P, wa, b_alpha.reshape(1, -1), norm_g.reshape(1, -1))


RWKV_PREP_TS = 512
RWKV_SHIFT_LO = RWKV_DECAY_RANK + RWKV_AAA_RANK + RWKV_GATE_RANK


def _split_bf16(x):
    hi = x.astype(BF16)
    return hi, (x - hi.astype(F32)).astype(BF16)


def _segment_sum(x, seg):
    n = x.shape[-1]
    i = lax.broadcasted_iota(jnp.int32, (n, n), 0)
    j = lax.broadcasted_iota(jnp.int32, (n, n), 1)
    ones = jnp.where(i // seg == j // seg, 1.0, 0.0).astype(BF16)
    hi, lo = _split_bf16(x)
    return jnp.dot(hi, ones, preferred_element_type=F32) + jnp.dot(lo, ones, preferred_element_type=F32)


def _rwkv_prep_kernel(rkv_ref, lo_ref, rkv_prev_ref, lo_prev_ref, mu_rkv_ref, mu_lo_ref, w0_ref, w2_ref,
                      a0_ref, a2_ref, g2_ref, kk_ref, ka_ref, rk_ref,
                      r_out, w_out, k_out, v_out, kk_out, b_out, gate_out, bonus_out):
    first = pl.program_id(1) == 0

    def shifted(cur, prev_ref, mu):
        prev_row = jnp.where(first, 0.0, prev_ref[0][SUBLANES - 1:SUBLANES, :])
        row = lax.broadcasted_iota(jnp.int32, cur.shape, 0)
        prev = jnp.where(row == 0, prev_row, pltpu.roll(cur, 1, 0))
        return cur + (prev - cur) * mu

    xs = shifted(rkv_ref[0], rkv_prev_ref, mu_rkv_ref[...])
    lo = shifted(lo_ref[0], lo_prev_ref, mu_lo_ref[...])
    W = RWKV_WIDTH
    r, k, v = xs[:, 0:W], xs[:, W:2 * W], xs[:, 2 * W:3 * W]
    hdot = functools.partial(jnp.dot, precision=HIGHEST, preferred_element_type=F32)
    z = w0_ref[...] + hdot(jnp.tanh(lo), w2_ref[...])
    log_decay = -jax.nn.sigmoid(z) * float(np.exp(-0.5))
    a = jax.nn.sigmoid(a0_ref[...] + hdot(lo, a2_ref[...]))
    gate = hdot(jax.nn.sigmoid(lo), g2_ref[...])
    kk = k * kk_ref[...]
    kk_norm = jnp.sqrt(_segment_sum(kk * kk, RWKV_HEAD_DIM))
    kk = kk / jnp.maximum(kk_norm, 1e-12)
    k2 = k * (1.0 + (a - 1.0) * ka_ref[...])
    bonus = _segment_sum(r * k2 * rk_ref[...], RWKV_HEAD_DIM) * v
    r_out[0] = r
    w_out[0] = log_decay
    k_out[0] = k2
    v_out[0] = v
    kk_out[0] = kk
    b_out[0] = kk * a
    gate_out[0] = gate
    bonus_out[0] = bonus


def rwkv_prep(P, mu, w0, w2, a0, a2, g2, k_k, k_a, r_k):
    B, S, _ = P.shape
    TS = RWKV_PREP_TS
    W = RWKV_WIDTH
    W3 = 3 * W
    n_rkv = SRC_RWKV_RKV[1] - SRC_RWKV_RKV[0]
    mu_rkv = mu[:n_rkv].reshape(1, W3)
    mu_lo = jnp.zeros((1, RWKV_LO_PAD), F32).at[0, :RWKV_SHIFT_LO].set(mu[n_rkv:])
    w2p = jnp.zeros((RWKV_LO_PAD, W), F32).at[0:RWKV_DECAY_RANK].set(w2)
    a2p = jnp.zeros((RWKV_LO_PAD, W), F32).at[RWKV_DECAY_RANK:RWKV_DECAY_RANK + RWKV_AAA_RANK].set(a2)
    g2p = jnp.zeros((RWKV_LO_PAD, W), F32).at[RWKV_DECAY_RANK + RWKV_AAA_RANK:RWKV_SHIFT_LO].set(g2)
    row = lambda a: a.reshape(1, -1)
    full = lambda r, c: pl.BlockSpec((r, c), lambda b, i: (0, 0))
    tpb = TS // SUBLANES
    prev_map = lambda off, w: pl.BlockSpec((1, SUBLANES, w), lambda b, i: (b, jnp.maximum(i * tpb - 1, 0), off // w))
    oshape = jax.ShapeDtypeStruct((B, S, W), F32)
    ospec = pl.BlockSpec((1, TS, W), lambda b, i: (b, i, 0))
    return pl.pallas_call(
        _rwkv_prep_kernel,
        out_shape=(oshape,) * 8,
        grid=(B, S // TS),
        in_specs=[pl.BlockSpec((1, TS, W3), lambda b, i: (b, i, OFF_RWKV_RKV // W3)),
                  pl.BlockSpec((1, TS, RWKV_LO_PAD), lambda b, i: (b, i, OFF_RWKV_LO // RWKV_LO_PAD)),
                  prev_map(OFF_RWKV_RKV, W3), prev_map(OFF_RWKV_LO, RWKV_LO_PAD),
                  full(1, W3), full(1, RWKV_LO_PAD), full(1, W), full(RWKV_LO_PAD, W),
                  full(1, W), full(RWKV_LO_PAD, W), full(RWKV_LO_PAD, W),
                  full(1, W), full(1, W), full(1, W)],
        out_specs=(ospec,) * 8,
        compiler_params=_cparams(("parallel", "arbitrary")),
        name="rwkv7_prep",
    )(P, P, P, P, mu_rkv, mu_lo, row(w0), w2p, row(a0), a2p, g2p, row(k_k), row(k_a), row(r_k))


RWKV_CHUNK = 64
RWKV_CHUNK_TT = 512


def _rwkv_chunk_kernel(r_ref, lw_ref, k_ref, v_ref, kk_ref, b_ref, y_ref, h_ref):
    @pl.when(pl.program_id(2) == 0)
    def _():
        h_ref[...] = jnp.zeros_like(h_ref)

    C = RWKV_CHUNK
    TT = RWKV_CHUNK_TT
    N = RWKV_HEAD_DIM
    R2 = 2 * C
    bdot = lambda x, y: jnp.dot(x.astype(BF16), y.astype(BF16), preferred_element_type=F32)
    r, lw, k, v, kk, b = (ref[0] for ref in (r_ref, lw_ref, k_ref, v_ref, kk_ref, b_ref))
    i = lax.broadcasted_iota(jnp.int32, (TT, TT), 0)
    j = lax.broadcasted_iota(jnp.int32, (TT, TT), 1)
    tri = jnp.logical_and(i // C == j // C, j <= i).astype(F32)
    cum = jnp.dot(tri, lw, precision=HIGHEST, preferred_element_type=F32)
    lane_lo = lax.broadcasted_iota(jnp.int32, (C, LANES), 1) < N

    def stack(x):
        return jnp.concatenate([jnp.where(lane_lo, x, 0.0), jnp.where(lane_lo, 0.0, x)], axis=0)

    ti = lax.broadcasted_iota(jnp.int32, (R2, R2), 0)
    si = lax.broadcasted_iota(jnp.int32, (R2, R2), 1)
    strict = ti > si
    incl = ti >= si
    eye = (ti == si).astype(F32)
    blk16 = ti // 16 == si // 16
    off32 = jnp.logical_and(ti // 32 == si // 32, jnp.logical_not(blk16))
    off64 = jnp.logical_and(ti // 64 == si // 64, ti // 32 != si // 32)
    ones_c = jnp.ones((C, LANES), F32)

    chunks = range(TT // C)
    each = lambda f, *lists: [f(*args) for args in zip(*lists)]
    rows = [slice(c * C, (c + 1) * C) for c in chunks]
    cu = [cum[rw] for rw in rows]
    cu_last = [cum[(c + 1) * C - 1:(c + 1) * C] for c in chunks]
    e_neg = each(lambda x: jnp.exp(-x), cu)
    e_tail = each(lambda x, xl: jnp.exp(xl - x), cu, cu_last)
    a_t = each(lambda rw, x: stack(-kk[rw] * jnp.exp(x - lw[rw])), rows, cu)
    r_t = each(lambda rw, x: stack(r[rw] * jnp.exp(x)), rows, cu)
    b_t = each(lambda rw, e: stack(b[rw] * e), rows, e_neg)
    k_t = each(lambda rw, e: stack(k[rw] * e), rows, e_neg)
    v_s = each(lambda rw: stack(v[rw]), rows)
    gram = each(lambda a_, r_, b_, k_: lax.dot_general(
        jnp.concatenate([a_, r_], 0).astype(BF16), jnp.concatenate([b_, k_], 0).astype(BF16),
        (((1,), (1,)), ((), ())), preferred_element_type=F32), a_t, r_t, b_t, k_t)
    n_ab = each(lambda g: jnp.where(strict, g[:R2, :R2], 0.0), gram)
    a_ak = each(lambda g: jnp.where(strict, g[:R2, R2:], 0.0), gram)
    a_rb = each(lambda g: jnp.where(incl, g[R2:, :R2], 0.0), gram)
    a_rk = each(lambda g: jnp.where(incl, g[R2:, R2:], 0.0), gram)
    av = each(lambda ak, rk, vv: bdot(jnp.concatenate([ak, rk], 0), vv), a_ak, a_rk, v_s)
    n1 = each(lambda n: jnp.where(blk16, n, 0.0), n_ab)
    n2 = each(lambda x: bdot(x, x), n1)
    n4 = each(lambda x: bdot(x, x), n2)
    n8 = each(lambda x: bdot(x, x), n4)
    xa = each(lambda x1, x2: x1 + x2 + bdot(x1, x2), n1, n2)
    xb = each(lambda x4, x8: x4 + x8 + bdot(x4, x8), n4, n8)
    t_inv = each(lambda p, q: eye + p + q + bdot(p, q), xa, xb)
    for off in (off32, off64):
        mid = each(lambda n, t: bdot(jnp.where(off, n, 0.0), t), n_ab, t_inv)
        t_inv = each(lambda t, m_: t + bdot(t, m_), t_inv, mid)
    tw = each(lambda t, a_, av_: bdot(t, jnp.concatenate([a_, av_[:R2]], axis=1)), t_inv, a_t, av)
    bk_t = each(lambda rw, e: jnp.transpose(jnp.concatenate([stack(b[rw] * e), stack(k[rw] * e)], 0)),
                rows, e_tail)
    pc_col = each(lambda rw: jnp.exp(jnp.dot(jnp.transpose(lw[rw]), ones_c, precision=HIGHEST,
                                             preferred_element_type=F32)), rows)
    w1r = each(lambda t, r_: jnp.concatenate([t[:, :LANES], r_], 0), tw, r_t)

    h = h_ref[...]
    for c in chunks:
        x = bdot(w1r[c], h)
        u = x[:R2] + tw[c][:, LANES:]
        y_bd = x[R2:] + bdot(a_rb[c], u) + av[c][R2:]
        y_ref[0, rows[c], :] = y_bd[:C] + y_bd[C:]
        h = pc_col[c] * h + bdot(bk_t[c], jnp.concatenate([u, v_s[c]], 0))
    h_ref[...] = h


def rwkv_chunked(r, lw, k, v, kk, b):
    B, S, W = r.shape
    TT = RWKV_CHUNK_TT
    spec = pl.BlockSpec((1, TT, LANES), lambda bb, p, i: (bb, i, p))
    return pl.pallas_call(
        _rwkv_chunk_kernel,
        out_shape=jax.ShapeDtypeStruct((B, S, W), F32),
        grid=(B, W // LANES, S // TT),
        in_specs=[spec] * 6,
        out_specs=spec,
        scratch_shapes=[pltpu.VMEM((LANES, LANES), F32)],
        compiler_params=_cparams(("parallel", "parallel", "arbitrary")),
        name="rwkv7_chunked",
    )(r, lw, k, v, kk, b)


def _rwkv_post_kernel(y_ref, gate_ref, bonus_ref, g_ref, b_ref, o_ref):
    y = y_ref[0]
    mean = _segment_sum(y, RWKV_HEAD_DIM) * (1.0 / RWKV_HEAD_DIM)
    d = y - mean
    var = _segment_sum(d * d, RWKV_HEAD_DIM) * (1.0 / RWKV_HEAD_DIM)
    yn = d * lax.rsqrt(var + RWKV_GN_EPS) * g_ref[...] + b_ref[...]
    o_ref[0] = (yn + bonus_ref[0]) * gate_ref[0]


def rwkv_post(y, gate, bonus, ln_g, ln_b):
    B, S, W = y.shape
    ts = 512
    spec = pl.BlockSpec((1, ts, W), lambda b, i: (b, i, 0))
    full = pl.BlockSpec((1, W), lambda b, i: (0, 0))
    return pl.pallas_call(
        _rwkv_post_kernel,
        out_shape=jax.ShapeDtypeStruct((B, S, W), F32),
        grid=(B, S // ts),
        in_specs=[spec, spec, spec, full, full],
        out_specs=spec,
        compiler_params=_cparams(("parallel", "parallel")),
        name="rwkv7_post",
    )(y, gate, bonus, ln_g.reshape(1, W), ln_b.reshape(1, W))


def _merge_kernel(h_ref, wg0_ref, wg1_ref, wg2_ref, wg3_ref, ya_ref, yb_ref, yc_ref, yd_ref,
                  wa_ref, wb_ref, wc_ref, wd_ref, o_ref):
    h = h_ref[...]
    acc = None
    branches = ((wg0_ref, ya_ref, wa_ref), (wg1_ref, yb_ref, wb_ref), (wg2_ref, yc_ref, wc_ref),
                (wg3_ref, yd_ref, wd_ref))
    for wg_ref, y_ref, w_ref in branches:
        gate = jax.nn.sigmoid(jnp.dot(h, wg_ref[...], preferred_element_type=F32))
        proj = jnp.dot(y_ref[...].astype(BF16), w_ref[...], preferred_element_type=F32)
        acc = gate * proj if acc is None else acc + gate * proj
    o_ref[...] = acc.astype(o_ref.dtype)


def branch_merge(hb, wg, ys, ws):
    T, D = hb.shape
    tm, tn = 512, 512
    nj = D // tn
    gspecs = [pl.BlockSpec((D, tn), functools.partial(lambda j, i, br: (0, br * nj + j), br=br))
              for br in range(N_BRANCHES)]
    yspecs = [pl.BlockSpec((tm, y.shape[1]), lambda j, i: (i, 0)) for y in ys]
    wspecs = [pl.BlockSpec((w.shape[0], tn), lambda j, i: (0, j)) for w in ws]
    return pl.pallas_call(
        _merge_kernel,
        out_shape=jax.ShapeDtypeStruct((T, D), BF16),
        grid=(nj, T // tm),
        in_specs=[pl.BlockSpec((tm, D), lambda j, i: (i, 0))] + gspecs + yspecs + wspecs,
        out_specs=pl.BlockSpec((tm, tn), lambda j, i: (i, j)),
        compiler_params=_cparams(("parallel", "parallel")),
        name="branch_merge",
    )(hb, wg, wg, wg, wg, *ys, *ws)


ROUTE_TM = 256


def _layer_norm(z, g, b):
    mu = jnp.mean(z, -1, keepdims=True)
    d = z - mu
    var = jnp.mean(d * d, -1, keepdims=True)
    return d * lax.rsqrt(var + LN_EPS) * g + b


def _out_ln_router_kernel(m_ref, wo_ref, x_ref, gate_ref, g_ref, b_ref, sc_ref, sh_ref, rwh_ref, rwl_ref, rb_ref,
                          x1_ref, h2_ref, idx_ref, wt_ref, rank_ref, cnt_ref):
    y = jnp.dot(m_ref[...], wo_ref[...], preferred_element_type=F32)
    x1 = _layer_norm(DEEPNORM_ALPHA * x_ref[...] + gate_ref[0] * y, g_ref[...], b_ref[...])
    x1_ref[...] = x1
    h2 = x1 * (1.0 + sc_ref[0]) + sh_ref[0]
    h2_hi, h2_lo = _split_bf16(h2)
    h2_ref[...] = h2_hi
    logits = (jnp.dot(h2_hi, rwh_ref[...], preferred_element_type=F32)
              + jnp.dot(h2_hi, rwl_ref[...], preferred_element_type=F32)
              + jnp.dot(h2_lo, rwh_ref[...], preferred_element_type=F32)) + rb_ref[...]
    lane = lax.broadcasted_iota(jnp.int32, logits.shape, 1).astype(F32)
    vals, idxs = [], []
    cur = logits
    for _ in range(TOP_K):
        m = jnp.max(cur, -1, keepdims=True)
        ix = jnp.min(jnp.where(cur == m, lane, float(LANES)), -1, keepdims=True)
        vals.append(m)
        idxs.append(ix)
        cur = jnp.where(lane == ix, -jnp.inf, cur)
    es = [jnp.exp(v - vals[0]) for v in vals]
    tot = es[0] + es[1] + es[2] + es[3]
    sel = jnp.zeros_like(logits)
    for kq in range(TOP_K):
        sel = jnp.where(lane == idxs[kq], 1.0, sel)
    tm = logits.shape[0]
    ti = lax.broadcasted_iota(jnp.int32, (tm, tm), 0)
    si = lax.broadcasted_iota(jnp.int32, (tm, tm), 1)
    before = jnp.dot((si < ti).astype(BF16), sel.astype(BF16), preferred_element_type=F32)
    idx_o = jnp.zeros_like(logits)
    wt_o = jnp.zeros_like(logits)
    rank_o = jnp.zeros_like(logits)
    for kq in range(TOP_K):
        rank_k = jnp.sum(jnp.where(lane == idxs[kq], before, 0.0), -1, keepdims=True)
        idx_o = jnp.where(lane == kq, idxs[kq], idx_o)
        wt_o = jnp.where(lane == kq, es[kq] / tot, wt_o)
        rank_o = jnp.where(lane == kq, rank_k, rank_o)
    idx_ref[...] = idx_o.astype(jnp.int32)
    wt_ref[...] = wt_o
    rank_ref[...] = rank_o.astype(jnp.int32)
    cnt = jnp.sum(sel, axis=0, keepdims=True)
    cnt_ref[0] = jnp.broadcast_to(cnt, (SUBLANES, LANES)).astype(jnp.int32)


def out_ln_router(merged, w_out_b, x2d, gate1, ln_g, ln_b, scale2, shift2, router_w, router_b, S):
    T, D = x2d.shape
    tm = ROUTE_TM
    nt = T // tm
    spb = S // tm
    rw = jnp.zeros((D, LANES), F32).at[:, :N_EXPERTS].set(router_w)
    rw_hi, rw_lo = _split_bf16(rw)
    rb = jnp.full((1, LANES), NEG_BIG, F32).at[0, :N_EXPERTS].set(router_b)
    rowblk = lambda w: pl.BlockSpec((tm, w), lambda i: (i, 0))
    full = lambda r, c: pl.BlockSpec((r, c), lambda i: (0, 0))
    perb = pl.BlockSpec((1, 1, D), lambda i: (i // spb, 0, 0))
    tile_spec = pl.BlockSpec((1, SUBLANES, LANES), lambda i: (i, 0, 0))
    tile_shape = jax.ShapeDtypeStruct((nt, SUBLANES, LANES), jnp.int32)
    return pl.pallas_call(
        _out_ln_router_kernel,
        out_shape=(jax.ShapeDtypeStruct((T, D), F32), jax.ShapeDtypeStruct((T, D), BF16),
                   jax.ShapeDtypeStruct((T, LANES), jnp.int32), jax.ShapeDtypeStruct((T, LANES), F32),
                   jax.ShapeDtypeStruct((T, LANES), jnp.int32), tile_shape),
        grid=(nt,),
        in_specs=[rowblk(D), full(D, D), rowblk(D), perb, full(1, D), full(1, D), perb, perb,
                  full(D, LANES), full(D, LANES), full(1, LANES)],
        out_specs=(rowblk(D), rowblk(D), rowblk(LANES), rowblk(LANES), rowblk(LANES), tile_spec),
        compiler_params=_cparams(("parallel",)),
        name="out_proj_ln_router",
    )(merged, w_out_b, x2d, gate1, ln_g.reshape(1, D), ln_b.reshape(1, D), scale2, shift2, rw_hi, rw_lo, rb)


PIECE = 16
TILE_ROWS = 1536
BLOCK_PIECES = MOE_BM // PIECE


def _dispatch_kernel(lrow_ref, h_ref, o_ref):
    lrow = lrow_ref[0]
    row = lax.broadcasted_iota(jnp.int32, (TILE_ROWS, lrow.shape[1]), 0)
    hit = row == lrow[0:1, :]
    for kq in range(1, TOP_K):
        hit = jnp.logical_or(hit, row == lrow[kq:kq + 1, :])
    onehot = jnp.where(hit, 1.0, 0.0).astype(BF16)
    o_ref[...] = jnp.dot(onehot, h_ref[...], preferred_element_type=F32).astype(o_ref.dtype)


def moe_dispatch(h2, lrow_t):
    T, D = h2.shape
    nt = T // ROUTE_TM
    return pl.pallas_call(
        _dispatch_kernel,
        out_shape=jax.ShapeDtypeStruct(((nt + 1) * TILE_ROWS, D), BF16),
        grid=(nt + 1,),
        in_specs=[pl.BlockSpec((1, SUBLANES, ROUTE_TM), lambda i: (i, 0, 0)),
                  pl.BlockSpec((ROUTE_TM, D), lambda i: (jnp.minimum(i, nt - 1), 0))],
        out_specs=pl.BlockSpec((TILE_ROWS, D), lambda i: (i, 0)),
        compiler_params=_cparams(("parallel",)),
        name="moe_dispatch",
    )(lrow_t, h2)


W_CHUNK_ROWS = 512
W_STAGES = 3
GU_CHUNKS = D_MODEL // W_CHUNK_ROWS
DN_CHUNKS = EXPERT_FF // W_CHUNK_ROWS


def _expert_kernel(be_ref, nu_ref, src_ref, dst_ref, par_ref, nxt_ref, x_hbm, wgu_hbm, bgu_ref, wd_hbm, bd_ref,
                   y_hbm, xs_ref, ys_ref, wgu_b, wd_b, stg_ref, isem, osem, wsem, *, layer):
    i = pl.program_id(0)
    nu = nu_ref[0]
    slot = i % 2
    n_chunks = GU_CHUNKS + DN_CHUNKS

    def in_copy(blk, k, s):
        src = pl.multiple_of(src_ref[blk * BLOCK_PIECES + k], PIECE)
        return pltpu.make_async_copy(x_hbm.at[pl.ds(src, PIECE), :], xs_ref.at[s, pl.ds(k * PIECE, PIECE), :],
                                     isem.at[s])

    def out_copy(blk, k, s):
        dst = pl.multiple_of(dst_ref[blk * BLOCK_PIECES + k], PIECE)
        return pltpu.make_async_copy(ys_ref.at[s, pl.ds(k * PIECE, PIECE), :], y_hbm.at[pl.ds(dst, PIECE), :],
                                     osem.at[s])

    def all_pieces(fn):
        for k in range(BLOCK_PIECES):
            fn(k)

    def w_copy(e, c):
        s = c % W_STAGES
        if c < GU_CHUNKS:
            src = wgu_hbm.at[layer, e, pl.ds(c * W_CHUNK_ROWS, W_CHUNK_ROWS), :]
        else:
            src = wd_hbm.at[layer, e, pl.ds((c - GU_CHUNKS) * W_CHUNK_ROWS, W_CHUNK_ROWS), :]
        return pltpu.make_async_copy(src, stg_ref.at[s], wsem.at[s])

    def w_convert(c, p):
        w = stg_ref[c % W_STAGES].astype(BF16)
        if c < GU_CHUNKS:
            wgu_b[p, pl.ds(c * W_CHUNK_ROWS, W_CHUNK_ROWS), :] = w
        else:
            wd_b[p, pl.ds((c - GU_CHUNKS) * W_CHUNK_ROWS, W_CHUNK_ROWS), :] = w

    def w_finish(e, p, chunks):
        for c in chunks:
            w_copy(e, c).wait()
            w_convert(c, p)
            if c + W_STAGES < n_chunks:
                w_copy(e, c + W_STAGES).start()

    @pl.when(i == 0)
    def _():
        all_pieces(lambda k: in_copy(0, k, 0).start())
        e0 = be_ref[0]
        for c in range(W_STAGES):
            w_copy(e0, c).start()
        w_finish(e0, par_ref[0], range(n_chunks))

    @pl.when(i + 1 < nu)
    def _():
        all_pieces(lambda k: in_copy(i + 1, k, 1 - slot).start())

    @pl.when(i < nu)
    def _():
        p = par_ref[i]
        e_next = nxt_ref[i]
        prefetch = e_next >= 0

        @pl.when(prefetch)
        def _():
            for c in range(W_STAGES):
                w_copy(e_next, c).start()

        all_pieces(lambda k: in_copy(i, k, slot).wait())
        gu = jnp.dot(xs_ref[slot], wgu_b[p], preferred_element_type=F32) + bgu_ref[0]

        @pl.when(prefetch)
        def _():
            w_finish(e_next, 1 - p, range(W_STAGES))

        glu = jnp.minimum(gu[:, :EXPERT_FF], SWIGLU_LIMIT)
        lin = jnp.clip(gu[:, EXPERT_FF:], -SWIGLU_LIMIT, SWIGLU_LIMIT)
        act = glu * jax.nn.sigmoid(SWIGLU_ALPHA * glu) * (lin + 1.0)

        @pl.when(i >= 2)
        def _():
            all_pieces(lambda k: out_copy(i - 2, k, slot).wait())

        y = jnp.dot(act.astype(BF16), wd_b[p], preferred_element_type=F32) + bd_ref[0]
        ys_ref[slot] = y.astype(ys_ref.dtype)
        all_pieces(lambda k: out_copy(i, k, slot).start())

        @pl.when(prefetch)
        def _():
            w_finish(e_next, 1 - p, range(W_STAGES, n_chunks))

    @pl.when(i == nu - 1)
    def _():
        @pl.when(i >= 1)
        def _():
            all_pieces(lambda k: out_copy(i - 1, k, 1 - slot).wait())
        all_pieces(lambda k: out_copy(i, k, slot).wait())


def expert_ffn(block_e, n_used, piece_src, piece_dst, w_parity, next_e, xbt, wgu, bgu, wd, bd, layer):
    rows, D = xbt.shape
    _, E, _, F2 = wgu.shape
    nblk = block_e.shape[0]
    bias = lambda w: pl.BlockSpec((1, 1, w), lambda i, be, *_: (be[i], 0, 0))
    grid_spec = pltpu.PrefetchScalarGridSpec(
        num_scalar_prefetch=6,
        grid=(nblk,),
        in_specs=[pl.BlockSpec(memory_space=pl.ANY), pl.BlockSpec(memory_space=pl.ANY), bias(F2),
                  pl.BlockSpec(memory_space=pl.ANY), bias(D)],
        out_specs=pl.BlockSpec(memory_space=pl.ANY),
        scratch_shapes=[pltpu.VMEM((2, MOE_BM, D), BF16), pltpu.VMEM((2, MOE_BM, D), BF16),
                        pltpu.VMEM((2, D, F2), BF16), pltpu.VMEM((2, F2 // 2, D), BF16),
                        pltpu.VMEM((W_STAGES, W_CHUNK_ROWS, D), F32),
                        pltpu.SemaphoreType.DMA((2,)), pltpu.SemaphoreType.DMA((2,)),
                        pltpu.SemaphoreType.DMA((W_STAGES,))],
    )
    return pl.pallas_call(
        functools.partial(_expert_kernel, layer=layer),
        out_shape=jax.ShapeDtypeStruct((rows, D), BF16),
        grid_spec=grid_spec,
        input_output_aliases={6: 0},
        compiler_params=_cparams(("arbitrary",)),
        name="expert_ffn",
    )(block_e, n_used, piece_src, piece_dst, w_parity, next_e, xbt, wgu, bgu.reshape(E, 1, F2), wd,
      bd.reshape(E, 1, D))


COMBINE_COLS = 512


def _combine_kernel(meta_ref, wt_ref, x1_ref, gate_ref, g_ref, b_ref, sc_ref, sh_ref, y_ref,
                    x2_ref, hb_ref, acc_ref):
    meta = meta_ref[...]
    wts = wt_ref[...]
    CW = COMBINE_COLS
    col0 = lax.broadcasted_iota(jnp.int32, (meta.shape[0], CW), 1)
    for cc in range(TILE_ROWS // CW):
        hi = jnp.zeros(col0.shape, F32)
        lo = jnp.zeros(col0.shape, F32)
        for kq in range(TOP_K):
            hit = col0 == meta[:, kq:kq + 1] - cc * CW
            w = wts[:, kq:kq + 1]
            w_hi = w.astype(BF16).astype(F32)
            hi = jnp.where(hit, w_hi, hi)
            lo = jnp.where(hit, w - w_hi, lo)
        rows = y_ref[pl.ds(cc * CW, CW), :]
        part = (jnp.dot(hi.astype(BF16), rows, preferred_element_type=F32)
                + jnp.dot(lo.astype(BF16), rows, preferred_element_type=F32))
        if cc == 0:
            acc_ref[...] = part
        else:
            acc_ref[...] += part

    x2 = _layer_norm(DEEPNORM_ALPHA * x1_ref[...] + gate_ref[0] * acc_ref[...], g_ref[...], b_ref[...])
    x2_ref[...] = x2
    hb_ref[...] = (x2 * (1.0 + sc_ref[0]) + sh_ref[0]).astype(hb_ref.dtype)


def moe_combine_ln(ybt, meta, wt, x1, gate2, ln_g, ln_b, scale_next, shift_next, S):
    T, D = x1.shape
    tm = ROUTE_TM
    spb = S // tm
    rowblk = lambda w: pl.BlockSpec((tm, w), lambda i: (i, 0))
    full = pl.BlockSpec((1, D), lambda i: (0, 0))
    perb = pl.BlockSpec((1, 1, D), lambda i: (i // spb, 0, 0))
    return pl.pallas_call(
        _combine_kernel,
        out_shape=(jax.ShapeDtypeStruct((T, D), F32), jax.ShapeDtypeStruct((T, D), BF16)),
        grid=(T // tm,),
        in_specs=[rowblk(LANES), rowblk(LANES), rowblk(D), perb, full, full, perb, perb,
                  pl.BlockSpec((TILE_ROWS, D), lambda i: (i, 0))],
        out_specs=(rowblk(D), rowblk(D)),
        scratch_shapes=[pltpu.VMEM((tm, D), F32)],
        compiler_params=_cparams(("parallel",)),
        name="moe_combine_ln",
    )(meta, wt, x1, gate2, ln_g.reshape(1, D), ln_b.reshape(1, D), scale_next, shift_next, ybt)


def routed_ffn_ln(h2, idx, wt, rank, tile_cnt, wgu, bgu, wd, bd, layer, x1, gate2, ln_g, ln_b,
                  scale_next, shift_next, S):
    T, D = h2.shape
    E = N_EXPERTS
    nt = T // ROUTE_TM
    i32 = jnp.int32
    e4 = idx[:, :TOP_K]
    cnt = tile_cnt[:, 0, :E]
    npc = (cnt + PIECE - 1) // PIECE
    seg_off = (jnp.cumsum(npc, axis=1) - npc) * PIECE
    onehot = e4[:, :, None] == jnp.arange(E, dtype=i32)[None, None, :]
    off_tok = jnp.repeat(seg_off, ROUTE_TM, axis=0)[:, None, :]
    lrow = jnp.sum(jnp.where(onehot, off_tok, 0), axis=-1) + rank[:, :TOP_K]
    per_e = jnp.sum(npc, axis=0)
    per_e_pad = ((per_e + BLOCK_PIECES - 1) // BLOCK_PIECES) * BLOCK_PIECES
    e_end = jnp.cumsum(per_e_pad)
    e_start = e_end - per_e_pad
    t_end = jnp.cumsum(npc, axis=0).T
    t_start = t_end - npc.T
    max_pieces = nt * ((ROUTE_TM * TOP_K + E * (PIECE - 1)) // PIECE) + E * (BLOCK_PIECES - 1)
    nblk = -(-max_pieces // BLOCK_PIECES)
    q = jnp.arange(nblk * BLOCK_PIECES, dtype=i32)
    e_q = jnp.minimum(jnp.sum((e_end[None, :] <= q[:, None]).astype(i32), axis=1), E - 1)
    sel_e = e_q[:, None] == jnp.arange(E, dtype=i32)[None, :]
    of_expert = lambda table: jnp.sum(jnp.where(sel_e, table[None, :], 0), axis=1)
    w_q = q - of_expert(e_start)
    real = w_q < of_expert(per_e)
    rows_of = lambda table: jnp.sum(jnp.where(sel_e[:, :, None], table[None], 0), axis=1)
    t_q = jnp.minimum(jnp.sum((rows_of(t_end) <= w_q[:, None]).astype(i32), axis=1), nt - 1)
    sel_t = t_q[:, None] == jnp.arange(nt, dtype=i32)[None, :]
    of_tile = lambda table: jnp.sum(jnp.where(sel_t, rows_of(table), 0), axis=1)
    k_q = w_q - of_tile(t_start)
    row_q = t_q * TILE_ROWS + of_tile(seg_off.T) + k_q * PIECE
    scratch_q = nt * TILE_ROWS + (((q // BLOCK_PIECES) % 2) * BLOCK_PIECES + q % BLOCK_PIECES) * PIECE
    piece_src = jnp.where(real, row_q, 0).astype(i32)
    piece_dst = jnp.where(real, row_q, scratch_q).astype(i32)
    block_e = e_q[::BLOCK_PIECES]
    n_used = (e_end[-1] // BLOCK_PIECES).astype(i32)
    blk = jnp.arange(nblk, dtype=i32)
    nxt = jnp.concatenate([block_e[1:], block_e[-1:]])
    switch_next = jnp.logical_and(nxt != block_e, blk + 1 < n_used)
    next_e = jnp.where(switch_next, nxt, -1).astype(i32)
    w_parity = (jnp.cumsum(switch_next.astype(i32)) - switch_next.astype(i32)) % 2

    lrow_t = jnp.full((nt + 1, SUBLANES, ROUTE_TM), -1, i32).at[:nt, :TOP_K, :].set(
        lrow.reshape(nt, ROUTE_TM, TOP_K).transpose(0, 2, 1))
    xbt = moe_dispatch(h2, lrow_t)
    ybt = expert_ffn(block_e, n_used.reshape(1), piece_src, piece_dst, w_parity, next_e, xbt, wgu, bgu, wd, bd,
                     layer)
    meta = jnp.full((T, LANES), -1, i32).at[:, :TOP_K].set(lrow)
    return moe_combine_ln(ybt, meta, wt, x1, gate2, ln_g, ln_b, scale_next, shift_next, S)


def _mix_weights(w_in_l):
    def cols(rng, pad=0):
        part = w_in_l[:, rng[0]:rng[1]]
        if pad:
            part = jnp.pad(part, ((0, 0), (0, pad)))
        return part
    w_mix = jnp.concatenate([
        cols(SRC_POOL), cols(SRC_GLA_V), cols(SRC_GLA_G), cols(SRC_RWKV_RKV), cols(SRC_ATT),
        cols(SRC_GLA_Q), cols(SRC_GLA_K),
        cols(SRC_RWKV_LO, RWKV_LO_PAD - (SRC_RWKV_LO[1] - SRC_RWKV_LO[0])),
        cols(SRC_GLA_A, GLA_A_PAD - (SRC_GLA_A[1] - SRC_GLA_A[0]))], axis=1).astype(BF16)
    w_gates = w_in_l[:, SRC_GATES[0]:].astype(BF16)
    return w_mix, w_gates


def kernel(x, c, ada_w, ada_b, w_in, pool_w, pool_scale, gla_w_alpha, gla_b_alpha, gla_norm_g, rwkv_mu, rwkv_w0, rwkv_w2, rwkv_a0, rwkv_a2, rwkv_g2, rwkv_k_k, rwkv_k_a, rwkv_r_k, rwkv_ln_g, rwkv_ln_b, w_branch_a, w_branch_b, w_branch_c, w_branch_d, w_out, ln1_g, ln1_b, router_w, router_b, w_gate_up, b_gate_up, w_down, b_down, ln2_g, ln2_b):
    B, S, D = x.shape
    T = B * S
    mod = ada_modulation(c, ada_w, ada_b)
    mods = [[mod[l, :, None, i * D:(i + 1) * D] for i in range(6)] for l in range(DEPTH)]
    x2d = x.reshape(T, D)
    hb2d = modulate(x, mods[0][1], mods[0][0]).reshape(T, D)
    for l in range(DEPTH):
        shift1, scale1, gate1, shift2, scale2, gate2 = mods[l]
        shift_next, scale_next = (mods[l + 1][0], mods[l + 1][1]) if l + 1 < DEPTH else (shift1, scale1)
        w_mix, w_gates = _mix_weights(w_in[l])
        P = matmul(hb2d, w_mix, 1024, MIX_TN).reshape(B, S, MIX_WIDTH)
        y_a = pool_mixer(P, pool_w[l], pool_scale[l])
        y_b = dilated_attention(P)
        y_c = gla_mixer(P, gla_w_alpha[l], gla_b_alpha[l], gla_norm_g[l])
        r_, w_, k_, v_, kk_, b_, gate_, bonus_ = rwkv_prep(
            P, rwkv_mu[l], rwkv_w0[l], rwkv_w2[l], rwkv_a0[l], rwkv_a2[l], rwkv_g2[l],
            rwkv_k_k[l], rwkv_k_a[l], rwkv_r_k[l].reshape(-1))
        y_t = rwkv_chunked(r_, w_, k_, v_, kk_, b_)
        y_d = rwkv_post(y_t, gate_, bonus_, rwkv_ln_g[l], rwkv_ln_b[l])
        ys = [y.reshape(T, -1) for y in (y_a, y_b, y_c, y_d)]
        ws = [w[l].astype(BF16) for w in (w_branch_a, w_branch_b, w_branch_c, w_branch_d)]
        merged = branch_merge(hb2d, w_gates, ys, ws)
        x1, h2, idx, wt, rank, tile_cnt = out_ln_router(
            merged, w_out[l].astype(BF16), x2d, gate1, ln1_g[l], ln1_b[l], scale2, shift2,
            router_w[l], router_b[l], S)
        x2d, hb2d = routed_ffn_ln(h2, idx, wt, rank, tile_cnt, w_gate_up, b_gate_up[l], w_down, b_down[l], l,
                                  x1, gate2, ln2_g[l], ln2_b[l], scale_next, shift_next, S)
    return x2d.reshape(B, S, D)
```

```python
import functools

import numpy as np
import jax
import jax.numpy as jnp
from jax import lax
from jax.experimental import pallas as pl
from jax.experimental.pallas import tpu as pltpu

F32 = jnp.float32
BF16 = jnp.bfloat16
HIGHEST = lax.Precision.HIGHEST

D_MODEL = 2048
DEPTH = 2
POOL_WINDOWS = (2, 4, 8, 16)
POOL_GROUP = 128
POOL_WIDTH = 512
ATT_GROUPS = ((128, 1), (512, 4), (2048, 16))
ATT_HEAD_DIM = 64
ATT_HEADS = 12
ATT_WIDTH = 768
ATT_OUT_WIDTH = 256
ATT_BLOCK = 128
ALIBI_SLOPES = tuple(2.0 ** (-8.0 * (h + 1) / ATT_HEADS) for h in range(ATT_HEADS))
GLA_HEADS = 4
GLA_DK = 64
GLA_DV = 128
GLA_KEY_WIDTH = 256
GLA_VAL_WIDTH = 512
GLA_GATE_RANK = 16
GLA_GATE_TEMP = 16.0
GLA_CHUNK = 32
GLA_NORM_EPS = 1e-6
RWKV_HEADS = 8
RWKV_HEAD_DIM = 64
RWKV_WIDTH = 512
RWKV_DECAY_RANK = 32
RWKV_AAA_RANK = 32
RWKV_GATE_RANK = 96
RWKV_GN_EPS = 64e-5
N_BRANCHES = 4
N_EXPERTS = 32
TOP_K = 4
EXPERT_FF = 1024
SWIGLU_LIMIT = 7.0
SWIGLU_ALPHA = 1.702
LN_EPS = 1e-5
DEEPNORM_ALPHA = (2 * DEPTH) ** 0.25

LANES = 128
SUBLANES = 8
VMEM_LIMIT = 56 * 1024 * 1024

SRC_POOL = (0, 512)
SRC_ATT = (512, 2816)
SRC_GLA_Q = (2816, 3072)
SRC_GLA_K = (3072, 3328)
SRC_GLA_V = (3328, 3840)
SRC_GLA_G = (3840, 4352)
SRC_GLA_A = (4352, 4368)
SRC_RWKV_RKV = (4368, 5904)
SRC_RWKV_LO = (5904, 6064)
SRC_GATES = (6064, 14256)
OFF_POOL = 0
OFF_GLA_V = 512
OFF_GLA_G = 1024
OFF_RWKV_RKV = 1536
OFF_ATT_Q = 3072
OFF_ATT_K = OFF_ATT_Q + ATT_WIDTH
OFF_ATT_V = OFF_ATT_K + ATT_WIDTH
OFF_GLA_Q = 5376
OFF_GLA_K = 5632
OFF_RWKV_LO = 5888
RWKV_LO_PAD = 256
OFF_GLA_A = 6144
GLA_A_PAD = 128
MIX_WIDTH = 6272
MIX_TN = 896

MOE_BM = 256
NEG_BIG = -1e30


def _cparams(sem):
    return pltpu.CompilerParams(dimension_semantics=sem, vmem_limit_bytes=VMEM_LIMIT)


def _ada_kernel(c_ref, w_ref, b_ref, o_ref):
    c = c_ref[...]
    s = c * jax.nn.sigmoid(c)
    o_ref[0] = jnp.dot(s, w_ref[0], precision=HIGHEST, preferred_element_type=F32) + b_ref[0]


def ada_modulation(c, ada_w, ada_b):
    L, D, N = ada_w.shape
    B = c.shape[0]
    cp = jnp.zeros((SUBLANES, D), F32).at[:B].set(c)
    tn = 1024
    out = pl.pallas_call(
        _ada_kernel,
        out_shape=jax.ShapeDtypeStruct((L, SUBLANES, N), F32),
        grid=(L, N // tn),
        in_specs=[pl.BlockSpec((SUBLANES, D), lambda l, j: (0, 0)),
                  pl.BlockSpec((1, D, tn), lambda l, j: (l, 0, j)),
                  pl.BlockSpec((1, 1, tn), lambda l, j: (l, 0, j))],
        out_specs=pl.BlockSpec((1, SUBLANES, tn), lambda l, j: (l, 0, j)),
        compiler_params=_cparams(("parallel", "parallel")),
        name="ada_modulation",
    )(cp, ada_w, ada_b.reshape(L, 1, N))
    return out[:, :B]


def _modulate_kernel(x_ref, sc_ref, sh_ref, o_ref):
    o_ref[0] = (x_ref[0] * (1.0 + sc_ref[0]) + sh_ref[0]).astype(o_ref.dtype)


def modulate(x, scale, shift):
    B, S, D = x.shape
    ts = 1024
    return pl.pallas_call(
        _modulate_kernel,
        out_shape=jax.ShapeDtypeStruct((B, S, D), BF16),
        grid=(B, S // ts),
        in_specs=[pl.BlockSpec((1, ts, D), lambda b, i: (b, i, 0)),
                  pl.BlockSpec((1, 1, D), lambda b, i: (b, 0, 0)),
                  pl.BlockSpec((1, 1, D), lambda b, i: (b, 0, 0))],
        out_specs=pl.BlockSpec((1, ts, D), lambda b, i: (b, i, 0)),
        compiler_params=_cparams(("parallel", "parallel")),
        name="modulate",
    )(x, scale, shift)


def _mm_kernel(x_ref, w_ref, o_ref):
    o_ref[...] = jnp.dot(x_ref[...], w_ref[...], preferred_element_type=F32).astype(o_ref.dtype)


def matmul(x, w, tm, tn, out_dtype=F32):
    M, K = x.shape
    N = w.shape[1]
    return pl.pallas_call(
        _mm_kernel,
        out_shape=jax.ShapeDtypeStruct((M, N), out_dtype),
        grid=(N // tn, M // tm),
        in_specs=[pl.BlockSpec((tm, K), lambda j, i: (i, 0)),
                  pl.BlockSpec((K, tn), lambda j, i: (0, j))],
        out_specs=pl.BlockSpec((tm, tn), lambda j, i: (i, j)),
        compiler_params=_cparams(("parallel", "parallel")),
        name="matmul",
    )(x, w)


def _pool_kernel(p_ref, w_ref, sc_ref, o_ref):
    g = pl.program_id(1)
    v = p_ref[0]
    S = v.shape[0]
    row = lax.broadcasted_iota(jnp.int32, v.shape, 0)
    win = jnp.left_shift(2, g)
    s = v
    pooled_sum = v
    for k, sh in enumerate((1, 2, 4, 8)):
        s = s + jnp.where(row >= sh, pltpu.roll(s, sh, 0), 0.0)
        pooled_sum = jnp.where(g >= k, s, pooled_sum)
    cnt = jnp.minimum(row + 1, win).astype(F32)
    diff = pooled_sum / cnt - v
    y = jnp.dot(diff.astype(BF16), w_ref[0].astype(BF16), preferred_element_type=F32)
    o_ref[0] = y * sc_ref[...]


def pool_mixer(P, pool_w, pool_scale):
    B, S, _ = P.shape
    G = len(POOL_WINDOWS)
    return pl.pallas_call(
        _pool_kernel,
        out_shape=jax.ShapeDtypeStruct((B, S, POOL_WIDTH), F32),
        grid=(B, G),
        in_specs=[pl.BlockSpec((1, S, POOL_GROUP), lambda b, g: (b, 0, OFF_POOL // POOL_GROUP + g)),
                  pl.BlockSpec((1, POOL_GROUP, POOL_GROUP), lambda b, g: (g, 0, 0)),
                  pl.BlockSpec((1, POOL_GROUP), lambda b, g: (0, g))],
        out_specs=pl.BlockSpec((1, S, POOL_GROUP), lambda b, g: (b, 0, g)),
        compiler_params=_cparams(("parallel", "parallel")),
        name="pool_mixer",
    )(P, pool_w, pool_scale.reshape(1, POOL_WIDTH))


ATT_GROUP_HEADS = 4
ATT_RESIDUE_UNROLL = 4


def _att_kernel(q_ref, kc_ref, kp_ref, vc_ref, vp_ref, num_ref, den_ref, m_ref, *, slopes, dilation, heads):
    if heads == ATT_GROUP_HEADS:
        n = pl.program_id(1)
    else:
        pair = pl.program_id(1)
        n = pl.program_id(2)
        slopes = [jnp.where(pair == 0, slopes[h], slopes[heads + h]) for h in range(heads)]
    i = lax.broadcasted_iota(jnp.int32, (ATT_BLOCK, ATT_BLOCK), 0)
    j = lax.broadcasted_iota(jnp.int32, (ATT_BLOCK, ATT_BLOCK), 1)
    dist_c = i - j
    dist_p = dist_c + ATT_BLOCK
    valid_c = dist_c >= 0
    valid_p = jnp.logical_and(dist_p <= ATT_BLOCK, n > 0)
    bias_c = [-(s * dilation) * dist_c.astype(F32) for s in slopes]
    bias_p = [-(s * dilation) * dist_p.astype(F32) for s in slopes]
    head_of_lane = lax.broadcasted_iota(jnp.int32, (ATT_BLOCK, heads * ATT_HEAD_DIM), 1) // ATT_HEAD_DIM
    dn = (((1,), (1,)), ((), ()))

    def per_head_lanes(cols):
        out = cols[-1]
        for h in range(heads - 2, -1, -1):
            out = jnp.where(head_of_lane == h, cols[h], out)
        return out

    def residue(r):
        rows = slice(None) if dilation == 1 else pl.ds(r, ATT_BLOCK, stride=dilation)
        q4, kc4, kp4, vc4, vp4 = (ref[0, rows, :] for ref in (q_ref, kc_ref, kp_ref, vc_ref, vp_ref))
        nums, dens, ms = [], [], []
        for h in range(heads):
            sl = slice(h * ATT_HEAD_DIM, (h + 1) * ATT_HEAD_DIM)
            q = (q4[:, sl] * (ATT_HEAD_DIM ** -0.5)).astype(BF16)
            s_c = lax.dot_general(q, kc4[:, sl].astype(BF16), dn, preferred_element_type=F32)
            s_p = lax.dot_general(q, kp4[:, sl].astype(BF16), dn, preferred_element_type=F32)
            s_c = jnp.where(valid_c, s_c + bias_c[h], NEG_BIG)
            s_p = jnp.where(valid_p, s_p + bias_p[h], NEG_BIG)
            m = jnp.maximum(jnp.max(s_c, -1, keepdims=True), jnp.max(s_p, -1, keepdims=True))
            p_c = jnp.exp(s_c - m)
            p_p = jnp.exp(s_p - m)
            dens.append(jnp.sum(p_c, -1, keepdims=True) + jnp.sum(p_p, -1, keepdims=True))
            nums.append(jnp.dot(p_c.astype(BF16), vc4[:, sl].astype(BF16), preferred_element_type=F32)
                        + jnp.dot(p_p.astype(BF16), vp4[:, sl].astype(BF16), preferred_element_type=F32))
            ms.append(m)
        num_ref[0, rows, :] = jnp.concatenate(nums, axis=-1)
        den_ref[0, rows, :] = per_head_lanes(dens)
        m_ref[0, rows, :] = per_head_lanes(ms)

    if dilation == 1:
        residue(0)
    else:
        def body(it, carry):
            for u in range(ATT_RESIDUE_UNROLL):
                residue(it * ATT_RESIDUE_UNROLL + u)
            return carry

        lax.fori_loop(0, dilation // ATT_RESIDUE_UNROLL, body, 0)


def att_group(P, g, dilation):
    B, S, NP = P.shape
    rows = ATT_BLOCK * dilation
    nb = S // rows
    heads = ATT_GROUP_HEADS if dilation == 1 else 2
    width = heads * ATT_HEAD_DIM
    npairs = ATT_GROUP_HEADS // heads
    qo, ko, vo = (o // width + g * npairs for o in (OFF_ATT_Q, OFF_ATT_K, OFF_ATT_V))
    blk = (1, rows, width)
    if npairs == 1:
        grid, sem = (B, nb), ("parallel", "arbitrary")
        cur = lambda off: pl.BlockSpec(blk, lambda b, n: (b, n, off))
        prev = lambda off: pl.BlockSpec(blk, lambda b, n: (b, jnp.maximum(n - 1, 0), off))
        ospec = pl.BlockSpec(blk, lambda b, n: (b, n, 0))
    else:
        grid, sem = (B, npairs, nb), ("parallel", "parallel", "arbitrary")
        cur = lambda off: pl.BlockSpec(blk, lambda b, hp, n: (b, n, off + hp))
        prev = lambda off: pl.BlockSpec(blk, lambda b, hp, n: (b, jnp.maximum(n - 1, 0), off + hp))
        ospec = pl.BlockSpec(blk, lambda b, hp, n: (b, n, hp))
    oshape = jax.ShapeDtypeStruct((B, S, ATT_OUT_WIDTH), F32)
    slopes = ALIBI_SLOPES[g * ATT_GROUP_HEADS:(g + 1) * ATT_GROUP_HEADS]
    return pl.pallas_call(
        functools.partial(_att_kernel, slopes=slopes, dilation=dilation, heads=heads),
        out_shape=(oshape, oshape, oshape),
        grid=grid,
        in_specs=[cur(qo), cur(ko), prev(ko), cur(vo), prev(vo)],
        out_specs=(ospec, ospec, ospec),
        compiler_params=_cparams(sem),
        name=f"dilated_attention_g{g}",
    )(P, P, P, P, P)


def _att_merge_kernel(*refs):
    o_ref = refs[-1]
    nums, dens, ms = refs[0:3], refs[3:6], refs[6:9]
    mx = jnp.maximum(jnp.maximum(ms[0][0], ms[1][0]), ms[2][0])
    num = jnp.zeros_like(mx)
    den = jnp.zeros_like(mx)
    for g in range(3):
        e = jnp.exp(ms[g][0] - mx)
        num = num + nums[g][0] * e
        den = den + dens[g][0] * e
    o_ref[0] = num / den


def dilated_attention(P):
    B, S, _ = P.shape
    parts = [att_group(P, g, d) for g, (_, d) in enumerate(ATT_GROUPS)]
    args = [p[0] for p in parts] + [p[1] for p in parts] + [p[2] for p in parts]
    ts = 1024
    spec = pl.BlockSpec((1, ts, ATT_OUT_WIDTH), lambda b, i: (b, i, 0))
    return pl.pallas_call(
        _att_merge_kernel,
        out_shape=jax.ShapeDtypeStruct((B, S, ATT_OUT_WIDTH), F32),
        grid=(B, S // ts),
        in_specs=[spec] * 9,
        out_specs=spec,
        compiler_params=_cparams(("parallel", "parallel")),
        name="dilated_attention_merge",
    )(*args)


GLA_TS = 256


def _gla_kernel(q_ref, k_ref, v_ref, g_ref, a_ref, wa_ref, ba_ref, ng_ref, o_ref, st_ref):
    @pl.when(pl.program_id(1) == 0)
    def _():
        st_ref[...] = jnp.zeros_like(st_ref)

    TS = GLA_TS
    nchunk = TS // GLA_CHUNK
    logit = jnp.dot(a_ref[0], wa_ref[...], precision=HIGHEST, preferred_element_type=F32) + ba_ref[...]
    log_a = (jnp.minimum(logit, 0.0) - jnp.log1p(jnp.exp(-jnp.abs(logit)))) / GLA_GATE_TEMP
    i = lax.broadcasted_iota(jnp.int32, (TS, TS), 0)
    j = lax.broadcasted_iota(jnp.int32, (TS, TS), 1)
    same_chunk_causal = jnp.logical_and(i // GLA_CHUNK == j // GLA_CHUNK, j <= i)
    tri = same_chunk_causal.astype(F32)
    bcum = jnp.dot(tri, log_a, precision=HIGHEST, preferred_element_type=F32)
    eb = jnp.exp(bcum)
    qd = q_ref[0] * (GLA_DK ** -0.5) * eb
    kd = k_ref[0] * jnp.exp(-bcum)
    k_all = k_ref[0]
    v_all = v_ref[0]
    dn_nt = (((1,), (1,)), ((), ()))
    dn_tn = (((0,), (0,)), ((), ()))
    for h in range(GLA_HEADS):
        ks = slice(h * GLA_DK, (h + 1) * GLA_DK)
        vs = slice(h * GLA_DV, (h + 1) * GLA_DV)
        qh = qd[:, ks].astype(BF16)
        vh = v_all[:, vs].astype(BF16)
        scores = lax.dot_general(qh, kd[:, ks].astype(BF16), dn_nt, preferred_element_type=F32)
        scores = jnp.where(same_chunk_causal, scores, 0.0)
        o_intra = jnp.dot(scores.astype(BF16), vh, preferred_element_type=F32)
        st = st_ref[h]
        outs = []
        for c in range(nchunk):
            rs = slice(c * GLA_CHUNK, (c + 1) * GLA_CHUNK)
            last = (c + 1) * GLA_CHUNK - 1
            o_c = o_intra[rs] + lax.dot_general(qh[rs], st.astype(BF16), dn_nt, preferred_element_type=F32)
            outs.append(o_c)
            b_last = bcum[last:last + 1, ks]
            k_tail = (k_all[rs, ks] * jnp.exp(b_last - bcum[rs, ks])).astype(BF16)
            st = st * jnp.exp(b_last) + lax.dot_general(vh[rs], k_tail, dn_tn, preferred_element_type=F32)
        st_ref[h] = st
        o = jnp.concatenate(outs, axis=0)
        o = o * lax.rsqrt(jnp.mean(o * o, -1, keepdims=True) + GLA_NORM_EPS)
        gg = g_ref[0][:, vs]
        o_ref[0, :, vs] = o * ng_ref[:, vs] * (gg * jax.nn.sigmoid(gg))


def gla_mixer(P, w_alpha, b_alpha, norm_g):
    B, S, _ = P.shape
    TS = GLA_TS
    wa = jnp.zeros((GLA_A_PAD, GLA_KEY_WIDTH), F32).at[:GLA_GATE_RANK].set(w_alpha)
    col = lambda off, w: pl.BlockSpec((1, TS, w), lambda b, i: (b, i, off // w))
    full = lambda r, c: pl.BlockSpec((r, c), lambda b, i: (0, 0))
    return pl.pallas_call(
        _gla_kernel,
        out_shape=jax.ShapeDtypeStruct((B, S, GLA_VAL_WIDTH), F32),
        grid=(B, S // TS),
        in_specs=[col(OFF_GLA_Q, GLA_KEY_WIDTH), col(OFF_GLA_K, GLA_KEY_WIDTH),
                  col(OFF_GLA_V, GLA_VAL_WIDTH), col(OFF_GLA_G, GLA_VAL_WIDTH),
                  col(OFF_GLA_A, GLA_A_PAD),
                  full(GLA_A_PAD, GLA_KEY_WIDTH), full(1, GLA_KEY_WIDTH), full(1, GLA_VAL_WIDTH)],
        out_specs=pl.BlockSpec((1, TS, GLA_VAL_WIDTH), lambda b, i: (b, i, 0)),
        scratch_shapes=[pltpu.VMEM((GLA_HEADS, GLA_DV, GLA_DK), F32)],
        compiler_params=_cparams(("parallel", "arbitrary")),
        name="gla_mixer",
    )(P, P, P, P, P, wa, b_alpha.reshape(1, -1), norm_g.reshape(1, -1))


RWKV_PREP_TS = 512
RWKV_SHIFT_LO = RWKV_DECAY_RANK + RWKV_AAA_RANK + RWKV_GATE_RANK


def _split_bf16(x):
    hi = x.astype(BF16)
    return hi, (x - hi.astype(F32)).astype(BF16)


def _segment_sum(x, seg):
    n = x.shape[-1]
    i = lax.broadcasted_iota(jnp.int32, (n, n), 0)
    j = lax.broadcasted_iota(jnp.int32, (n, n), 1)
    ones = jnp.where(i // seg == j // seg, 1.0, 0.0).astype(BF16)
    hi, lo = _split_bf16(x)
    return jnp.dot(hi, ones, preferred_element_type=F32) + jnp.dot(lo, ones, preferred_element_type=F32)


def _rwkv_prep_kernel(rkv_ref, lo_ref, rkv_prev_ref, lo_prev_ref, mu_rkv_ref, mu_lo_ref, w0_ref, w2_ref,
                      a0_ref, a2_ref, g2_ref, kk_ref, ka_ref, rk_ref,
                      r_out, w_out, k_out, v_out, kk_out, b_out, gate_out, bonus_out):
    first = pl.program_id(1) == 0

    def shifted(cur, prev_ref, mu):
        prev_row = jnp.where(first, 0.0, prev_ref[0][SUBLANES - 1:SUBLANES, :])
        row = lax.broadcasted_iota(jnp.int32, cur.shape, 0)
        prev = jnp.where(row == 0, prev_row, pltpu.roll(cur, 1, 0))
        return cur + (prev - cur) * mu

    xs = shifted(rkv_ref[0], rkv_prev_ref, mu_rkv_ref[...])
    lo = shifted(lo_ref[0], lo_prev_ref, mu_lo_ref[...])
    W = RWKV_WIDTH
    r, k, v = xs[:, 0:W], xs[:, W:2 * W], xs[:, 2 * W:3 * W]
    hdot = functools.partial(jnp.dot, precision=HIGHEST, preferred_element_type=F32)
    z = w0_ref[...] + hdot(jnp.tanh(lo), w2_ref[...])
    log_decay = -jax.nn.sigmoid(z) * float(np.exp(-0.5))
    a = jax.nn.sigmoid(a0_ref[...] + hdot(lo, a2_ref[...]))
    gate = hdot(jax.nn.sigmoid(lo), g2_ref[...])
    kk = k * kk_ref[...]
    kk_norm = jnp.sqrt(_segment_sum(kk * kk, RWKV_HEAD_DIM))
    kk = kk / jnp.maximum(kk_norm, 1e-12)
    k2 = k * (1.0 + (a - 1.0) * ka_ref[...])
    bonus = _segment_sum(r * k2 * rk_ref[...], RWKV_HEAD_DIM) * v
    r_out[0] = r
    w_out[0] = log_decay
    k_out[0] = k2
    v_out[0] = v
    kk_out[0] = kk
    b_out[0] = kk * a
    gate_out[0] = gate
    bonus_out[0] = bonus


def rwkv_prep(P, mu, w0, w2, a0, a2, g2, k_k, k_a, r_k):
    B, S, _ = P.shape
    TS = RWKV_PREP_TS
    W = RWKV_WIDTH
    W3 = 3 * W
    n_rkv = SRC_RWKV_RKV[1] - SRC_RWKV_RKV[0]
    mu_rkv = mu[:n_rkv].reshape(1, W3)
    mu_lo = jnp.zeros((1, RWKV_LO_PAD), F32).at[0, :RWKV_SHIFT_LO].set(mu[n_rkv:])
    w2p = jnp.zeros((RWKV_LO_PAD, W), F32).at[0:RWKV_DECAY_RANK].set(w2)
    a2p = jnp.zeros((RWKV_LO_PAD, W), F32).at[RWKV_DECAY_RANK:RWKV_DECAY_RANK + RWKV_AAA_RANK].set(a2)
    g2p = jnp.zeros((RWKV_LO_PAD, W), F32).at[RWKV_DECAY_RANK + RWKV_AAA_RANK:RWKV_SHIFT_LO].set(g2)
    row = lambda a: a.reshape(1, -1)
    full = lambda r, c: pl.BlockSpec((r, c), lambda b, i: (0, 0))
    tpb = TS // SUBLANES
    prev_map = lambda off, w: pl.BlockSpec((1, SUBLANES, w), lambda b, i: (b, jnp.maximum(i * tpb - 1, 0), off // w))
    oshape = jax.ShapeDtypeStruct((B, S, W), F32)
    ospec = pl.BlockSpec((1, TS, W), lambda b, i: (b, i, 0))
    return pl.pallas_call(
        _rwkv_prep_kernel,
        out_shape=(oshape,) * 8,
        grid=(B, S // TS),
        in_specs=[pl.BlockSpec((1, TS, W3), lambda b, i: (b, i, OFF_RWKV_RKV // W3)),
                  pl.BlockSpec((1, TS, RWKV_LO_PAD), lambda b, i: (b, i, OFF_RWKV_LO // RWKV_LO_PAD)),
                  prev_map(OFF_RWKV_RKV, W3), prev_map(OFF_RWKV_LO, RWKV_LO_PAD),
                  full(1, W3), full(1, RWKV_LO_PAD), full(1, W), full(RWKV_LO_PAD, W),
                  full(1, W), full(RWKV_LO_PAD, W), full(RWKV_LO_PAD, W),
                  full(1, W), full(1, W), full(1, W)],
        out_specs=(ospec,) * 8,
        compiler_params=_cparams(("parallel", "arbitrary")),
        name="rwkv7_prep",
    )(P, P, P, P, mu_rkv, mu_lo, row(w0), w2p, row(a0), a2p, g2p, row(k_k), row(k_a), row(r_k))


RWKV_CHUNK = 64
RWKV_CHUNK_TT = 512
RWKV_PAIRS_PER_STEP = 2


def _rwkv_chunk_kernel(r_ref, lw_ref, k_ref, v_ref, kk_ref, b_ref, y_ref, h_ref):
    @pl.when(pl.program_id(2) == 0)
    def _():
        h_ref[...] = jnp.zeros_like(h_ref)

    C = RWKV_CHUNK
    TT = RWKV_CHUNK_TT
    N = RWKV_HEAD_DIM
    R2 = 2 * C
    bdot = lambda x, y: jnp.dot(x.astype(BF16), y.astype(BF16), preferred_element_type=F32)
    r, lw, k, v, kk, b = (ref[0] for ref in (r_ref, lw_ref, k_ref, v_ref, kk_ref, b_ref))
    i = lax.broadcasted_iota(jnp.int32, (TT, TT), 0)
    j = lax.broadcasted_iota(jnp.int32, (TT, TT), 1)
    tri = jnp.logical_and(i // C == j // C, j <= i).astype(F32)
    cum = jnp.dot(tri, lw, precision=HIGHEST, preferred_element_type=F32)
    lane_lo = lax.broadcasted_iota(jnp.int32, (C, LANES), 1) < N

    def stack(x):
        return jnp.concatenate([jnp.where(lane_lo, x, 0.0), jnp.where(lane_lo, 0.0, x)], axis=0)

    ti = lax.broadcasted_iota(jnp.int32, (R2, R2), 0)
    si = lax.broadcasted_iota(jnp.int32, (R2, R2), 1)
    strict = ti > si
    incl = ti >= si
    eye = (ti == si).astype(F32)
    blk16 = ti // 16 == si // 16
    off32 = jnp.logical_and(ti // 32 == si // 32, jnp.logical_not(blk16))
    off64 = jnp.logical_and(ti // 64 == si // 64, ti // 32 != si // 32)
    ones_c = jnp.ones((C, LANES), F32)

    items = [(c, p) for c in range(TT // C) for p in range(RWKV_PAIRS_PER_STEP)]
    each = lambda f, *lists: [f(*args) for args in zip(*lists)]
    lanes_of = lambda p: slice(p * LANES, (p + 1) * LANES)
    rows = [(slice(c * C, (c + 1) * C), lanes_of(p)) for c, p in items]
    cu = [cum[rw] for rw in rows]
    cu_last = [cum[(c + 1) * C - 1:(c + 1) * C, lanes_of(p)] for c, p in items]
    e_neg = each(lambda x: jnp.exp(-x), cu)
    e_tail = each(lambda x, xl: jnp.exp(xl - x), cu, cu_last)
    a_t = each(lambda rw, x: stack(-kk[rw] * jnp.exp(x - lw[rw])), rows, cu)
    r_t = each(lambda rw, x: stack(r[rw] * jnp.exp(x)), rows, cu)
    b_t = each(lambda rw, e: stack(b[rw] * e), rows, e_neg)
    k_t = each(lambda rw, e: stack(k[rw] * e), rows, e_neg)
    v_s = each(lambda rw: stack(v[rw]), rows)
    gram = each(lambda a_, r_, b_, k_: lax.dot_general(
        jnp.concatenate([a_, r_], 0).astype(BF16), jnp.concatenate([b_, k_], 0).astype(BF16),
        (((1,), (1,)), ((), ())), preferred_element_type=F32), a_t, r_t, b_t, k_t)
    n_ab = each(lambda g: jnp.where(strict, g[:R2, :R2], 0.0), gram)
    a_ak = each(lambda g: jnp.where(strict, g[:R2, R2:], 0.0), gram)
    a_rb = each(lambda g: jnp.where(incl, g[R2:, :R2], 0.0), gram)
    a_rk = each(lambda g: jnp.where(incl, g[R2:, R2:], 0.0), gram)
    av = each(lambda ak, rk, vv: bdot(jnp.concatenate([ak, rk], 0), vv), a_ak, a_rk, v_s)
    n1 = each(lambda n: jnp.where(blk16, n, 0.0), n_ab)
    n2 = each(lambda x: bdot(x, x), n1)
    n4 = each(lambda x: bdot(x, x), n2)
    n8 = each(lambda x: bdot(x, x), n4)
    xa = each(lambda x1, x2: x1 + x2 + bdot(x1, x2), n1, n2)
    xb = each(lambda x4, x8: x4 + x8 + bdot(x4, x8), n4, n8)
    t_inv = each(lambda p, q: eye + p + q + bdot(p, q), xa, xb)
    for off in (off32, off64):
        mid = each(lambda n, t: bdot(jnp.where(off, n, 0.0), t), n_ab, t_inv)
        t_inv = each(lambda t, m_: t + bdot(t, m_), t_inv, mid)
    tw = each(lambda t, a_, av_: bdot(t, jnp.concatenate([a_, av_[:R2]], axis=1)), t_inv, a_t, av)
    bk_t = each(lambda rw, e: jnp.transpose(jnp.concatenate([stack(b[rw] * e), stack(k[rw] * e)], 0)),
                rows, e_tail)
    pc_col = each(lambda rw: jnp.exp(jnp.dot(jnp.transpose(lw[rw]), ones_c, precision=HIGHEST,
                                             preferred_element_type=F32)), rows)
    w1r = each(lambda t, r_: jnp.concatenate([t[:, :LANES], r_], 0), tw, r_t)

    h = [h_ref[p] for p in range(RWKV_PAIRS_PER_STEP)]
    for it, (c, p) in enumerate(items):
        x = bdot(w1r[it], h[p])
        u = x[:R2] + tw[it][:, LANES:]
        y_bd = x[R2:] + bdot(a_rb[it], u) + av[it][R2:]
        y_ref[(0,) + rows[it]] = y_bd[:C] + y_bd[C:]
        h[p] = pc_col[it] * h[p] + bdot(bk_t[it], jnp.concatenate([u, v_s[it]], 0))
    for p in range(RWKV_PAIRS_PER_STEP):
        h_ref[p] = h[p]


def rwkv_chunked(r, lw, k, v, kk, b):
    B, S, W = r.shape
    TT = RWKV_CHUNK_TT
    lanes = RWKV_PAIRS_PER_STEP * LANES
    spec = pl.BlockSpec((1, TT, lanes), lambda bb, p, i: (bb, i, p))
    return pl.pallas_call(
        _rwkv_chunk_kernel,
        out_shape=jax.ShapeDtypeStruct((B, S, W), F32),
        grid=(B, W // lanes, S // TT),
        in_specs=[spec] * 6,
        out_specs=spec,
        scratch_shapes=[pltpu.VMEM((RWKV_PAIRS_PER_STEP, LANES, LANES), F32)],
        compiler_params=_cparams(("parallel", "parallel", "arbitrary")),
        name="rwkv7_chunked",
    )(r, lw, k, v, kk, b)


def _rwkv_post_kernel(y_ref, gate_ref, bonus_ref, g_ref, b_ref, o_ref):
    y = y_ref[0]
    mean = _segment_sum(y, RWKV_HEAD_DIM) * (1.0 / RWKV_HEAD_DIM)
    d = y - mean
    var = _segment_sum(d * d, RWKV_HEAD_DIM) * (1.0 / RWKV_HEAD_DIM)
    yn = d * lax.rsqrt(var + RWKV_GN_EPS) * g_ref[...] + b_ref[...]
    o_ref[0] = (yn + bonus_ref[0]) * gate_ref[0]


def rwkv_post(y, gate, bonus, ln_g, ln_b):
    B, S, W = y.shape
    ts = 512
    spec = pl.BlockSpec((1, ts, W), lambda b, i: (b, i, 0))
    full = pl.BlockSpec((1, W), lambda b, i: (0, 0))
    return pl.pallas_call(
        _rwkv_post_kernel,
        out_shape=jax.ShapeDtypeStruct((B, S, W), F32),
        grid=(B, S // ts),
        in_specs=[spec, spec, spec, full, full],
        out_specs=spec,
        compiler_params=_cparams(("parallel", "parallel")),
        name="rwkv7_post",
    )(y, gate, bonus, ln_g.reshape(1, W), ln_b.reshape(1, W))


def _merge_kernel(h_ref, wg0_ref, wg1_ref, wg2_ref, wg3_ref, ya_ref, yb_ref, yc_ref, yd_ref,
                  wa_ref, wb_ref, wc_ref, wd_ref, o_ref):
    h = h_ref[...]
    acc = None
    branches = ((wg0_ref, ya_ref, wa_ref), (wg1_ref, yb_ref, wb_ref), (wg2_ref, yc_ref, wc_ref),
                (wg3_ref, yd_ref, wd_ref))
    for wg_ref, y_ref, w_ref in branches:
        gate = jax.nn.sigmoid(jnp.dot(h, wg_ref[...], preferred_element_type=F32))
        proj = jnp.dot(y_ref[...].astype(BF16), w_ref[...], preferred_element_type=F32)
        acc = gate * proj if acc is None else acc + gate * proj
    o_ref[...] = acc.astype(o_ref.dtype)


def branch_merge(hb, wg, ys, ws):
    T, D = hb.shape
    tm, tn = 512, 512
    nj = D // tn
    gspecs = [pl.BlockSpec((D, tn), functools.partial(lambda j, i, br: (0, br * nj + j), br=br))
              for br in range(N_BRANCHES)]
    yspecs = [pl.BlockSpec((tm, y.shape[1]), lambda j, i: (i, 0)) for y in ys]
    wspecs = [pl.BlockSpec((w.shape[0], tn), lambda j, i: (0, j)) for w in ws]
    return pl.pallas_call(
        _merge_kernel,
        out_shape=jax.ShapeDtypeStruct((T, D), BF16),
        grid=(nj, T // tm),
        in_specs=[pl.BlockSpec((tm, D), lambda j, i: (i, 0))] + gspecs + yspecs + wspecs,
        out_specs=pl.BlockSpec((tm, tn), lambda j, i: (i, j)),
        compiler_params=_cparams(("parallel", "parallel")),
        name="branch_merge",
    )(hb, wg, wg, wg, wg, *ys, *ws)


ROUTE_TM = 256


def _layer_norm(z, g, b):
    mu = jnp.mean(z, -1, keepdims=True)
    d = z - mu
    var = jnp.mean(d * d, -1, keepdims=True)
    return d * lax.rsqrt(var + LN_EPS) * g + b


def _out_ln_router_kernel(m_ref, wo_ref, x_ref, gate_ref, g_ref, b_ref, sc_ref, sh_ref, rwh_ref, rwl_ref, rb_ref,
                          x1_ref, h2_ref, idx_ref, wt_ref, rank_ref, cnt_ref):
    y = jnp.dot(m_ref[...], wo_ref[...], preferred_element_type=F32)
    x1 = _layer_norm(DEEPNORM_ALPHA * x_ref[...] + gate_ref[0] * y, g_ref[...], b_ref[...])
    x1_ref[...] = x1
    h2 = x1 * (1.0 + sc_ref[0]) + sh_ref[0]
    h2_hi, h2_lo = _split_bf16(h2)
    h2_ref[...] = h2_hi
    logits = (jnp.dot(h2_hi, rwh_ref[...], preferred_element_type=F32)
              + jnp.dot(h2_hi, rwl_ref[...], preferred_element_type=F32)
              + jnp.dot(h2_lo, rwh_ref[...], preferred_element_type=F32)) + rb_ref[...]
    lane = lax.broadcasted_iota(jnp.int32, logits.shape, 1).astype(F32)
    vals, idxs = [], []
    cur = logits
    for _ in range(TOP_K):
        m = jnp.max(cur, -1, keepdims=True)
        ix = jnp.min(jnp.where(cur == m, lane, float(LANES)), -1, keepdims=True)
        vals.append(m)
        idxs.append(ix)
        cur = jnp.where(lane == ix, -jnp.inf, cur)
    es = [jnp.exp(v - vals[0]) for v in vals]
    tot = es[0] + es[1] + es[2] + es[3]
    sel = jnp.zeros_like(logits)
    for kq in range(TOP_K):
        sel = jnp.where(lane == idxs[kq], 1.0, sel)
    tm = logits.shape[0]
    ti = lax.broadcasted_iota(jnp.int32, (tm, tm), 0)
    si = lax.broadcasted_iota(jnp.int32, (tm, tm), 1)
    before = jnp.dot((si < ti).astype(BF16), sel.astype(BF16), preferred_element_type=F32)
    idx_o = jnp.zeros_like(logits)
    wt_o = jnp.zeros_like(logits)
    rank_o = jnp.zeros_like(logits)
    for kq in range(TOP_K):
        rank_k = jnp.sum(jnp.where(lane == idxs[kq], before, 0.0), -1, keepdims=True)
        idx_o = jnp.where(lane == kq, idxs[kq], idx_o)
        wt_o = jnp.where(lane == kq, es[kq] / tot, wt_o)
        rank_o = jnp.where(lane == kq, rank_k, rank_o)
    idx_ref[...] = idx_o.astype(jnp.int32)
    wt_ref[...] = wt_o
    rank_ref[...] = rank_o.astype(jnp.int32)
    cnt = jnp.sum(sel, axis=0, keepdims=True)
    cnt_ref[0] = jnp.broadcast_to(cnt, (SUBLANES, LANES)).astype(jnp.int32)


def out_ln_router(merged, w_out_b, x2d, gate1, ln_g, ln_b, scale2, shift2, router_w, router_b, S):
    T, D = x2d.shape
    tm = ROUTE_TM
    nt = T // tm
    spb = S // tm
    rw = jnp.zeros((D, LANES), F32).at[:, :N_EXPERTS].set(router_w)
    rw_hi, rw_lo = _split_bf16(rw)
    rb = jnp.full((1, LANES), NEG_BIG, F32).at[0, :N_EXPERTS].set(router_b)
    rowblk = lambda w: pl.BlockSpec((tm, w), lambda i: (i, 0))
    full = lambda r, c: pl.BlockSpec((r, c), lambda i: (0, 0))
    perb = pl.BlockSpec((1, 1, D), lambda i: (i // spb, 0, 0))
    tile_spec = pl.BlockSpec((1, SUBLANES, LANES), lambda i: (i, 0, 0))
    tile_shape = jax.ShapeDtypeStruct((nt, SUBLANES, LANES), jnp.int32)
    return pl.pallas_call(
        _out_ln_router_kernel,
        out_shape=(jax.ShapeDtypeStruct((T, D), F32), jax.ShapeDtypeStruct((T, D), BF16),
                   jax.ShapeDtypeStruct((T, LANES), jnp.int32), jax.ShapeDtypeStruct((T, LANES), F32),
                   jax.ShapeDtypeStruct((T, LANES), jnp.int32), tile_shape),
        grid=(nt,),
        in_specs=[rowblk(D), full(D, D), rowblk(D), perb, full(1, D), full(1, D), perb, perb,
                  full(D, LANES), full(D, LANES), full(1, LANES)],
        out_specs=(rowblk(D), rowblk(D), rowblk(LANES), rowblk(LANES), rowblk(LANES), tile_spec),
        compiler_params=_cparams(("parallel",)),
        name="out_proj_ln_router",
    )(merged, w_out_b, x2d, gate1, ln_g.reshape(1, D), ln_b.reshape(1, D), scale2, shift2, rw_hi, rw_lo, rb)


PIECE = 16
TILE_ROWS = 1536
BLOCK_PIECES = MOE_BM // PIECE


def _dispatch_kernel(lrow_ref, h_ref, o_ref):
    lrow = lrow_ref[0]
    row = lax.broadcasted_iota(jnp.int32, (TILE_ROWS, lrow.shape[1]), 0)
    hit = row == lrow[0:1, :]
    for kq in range(1, TOP_K):
        hit = jnp.logical_or(hit, row == lrow[kq:kq + 1, :])
    onehot = jnp.where(hit, 1.0, 0.0).astype(BF16)
    o_ref[...] = jnp.dot(onehot, h_ref[...], preferred_element_type=F32).astype(o_ref.dtype)


def moe_dispatch(h2, lrow_t):
    T, D = h2.shape
    nt = T // ROUTE_TM
    return pl.pallas_call(
        _dispatch_kernel,
        out_shape=jax.ShapeDtypeStruct(((nt + 1) * TILE_ROWS, D), BF16),
        grid=(nt + 1,),
        in_specs=[pl.BlockSpec((1, SUBLANES, ROUTE_TM), lambda i: (i, 0, 0)),
                  pl.BlockSpec((ROUTE_TM, D), lambda i: (jnp.minimum(i, nt - 1), 0))],
        out_specs=pl.BlockSpec((TILE_ROWS, D), lambda i: (i, 0)),
        compiler_params=_cparams(("parallel",)),
        name="moe_dispatch",
    )(lrow_t, h2)


W_CHUNK_ROWS = 512
W_STAGES = 3
W_NONE, W_FIRST_HALF, W_SECOND_HALF, W_ALL = 0, 1, 2, 3
GU_CHUNKS = D_MODEL // W_CHUNK_ROWS
DN_CHUNKS = EXPERT_FF // W_CHUNK_ROWS


def _expert_kernel(be_ref, nu_ref, src_ref, dst_ref, par_ref, nxt_ref, mode_ref, x_hbm, wgu_hbm, bgu_ref, wd_hbm, bd_ref,
                   y_hbm, xs_ref, ys_ref, wgu_b, wd_b, stg_ref, isem, osem, wsem, *, layer):
    i = pl.program_id(0)
    nu = nu_ref[0]
    slot = i % 2
    n_chunks = GU_CHUNKS + DN_CHUNKS

    def in_copy(blk, k, s):
        src = pl.multiple_of(src_ref[blk * BLOCK_PIECES + k], PIECE)
        return pltpu.make_async_copy(x_hbm.at[pl.ds(src, PIECE), :], xs_ref.at[s, pl.ds(k * PIECE, PIECE), :],
                                     isem.at[s])

    def out_copy(blk, k, s):
        dst = pl.multiple_of(dst_ref[blk * BLOCK_PIECES + k], PIECE)
        return pltpu.make_async_copy(ys_ref.at[s, pl.ds(k * PIECE, PIECE), :], y_hbm.at[pl.ds(dst, PIECE), :],
                                     osem.at[s])

    def all_pieces(fn):
        for k in range(BLOCK_PIECES):
            fn(k)

    def w_copy(e, c):
        s = c % W_STAGES
        if c < GU_CHUNKS:
            src = wgu_hbm.at[layer, e, pl.ds(c * W_CHUNK_ROWS, W_CHUNK_ROWS), :]
        else:
            src = wd_hbm.at[layer, e, pl.ds((c - GU_CHUNKS) * W_CHUNK_ROWS, W_CHUNK_ROWS), :]
        return pltpu.make_async_copy(src, stg_ref.at[s], wsem.at[s])

    def w_convert(c, p):
        w = stg_ref[c % W_STAGES].astype(BF16)
        if c < GU_CHUNKS:
            wgu_b[p, pl.ds(c * W_CHUNK_ROWS, W_CHUNK_ROWS), :] = w
        else:
            wd_b[p, pl.ds((c - GU_CHUNKS) * W_CHUNK_ROWS, W_CHUNK_ROWS), :] = w

    def w_finish(e, p, chunks, refill):
        for c in chunks:
            w_copy(e, c).wait()
            w_convert(c, p)
            if refill and c + W_STAGES < n_chunks:
                w_copy(e, c + W_STAGES).start()

    @pl.when(i == 0)
    def _():
        all_pieces(lambda k: in_copy(0, k, 0).start())
        e0 = be_ref[0]
        for c in range(W_STAGES):
            w_copy(e0, c).start()
        w_finish(e0, par_ref[0], range(n_chunks), True)

    @pl.when(i + 1 < nu)
    def _():
        all_pieces(lambda k: in_copy(i + 1, k, 1 - slot).start())

    @pl.when(i < nu)
    def _():
        p = par_ref[i]
        e_next = nxt_ref[i]
        mode = mode_ref[i]
        first_half = range(W_STAGES)
        second_half = range(W_STAGES, n_chunks)

        @pl.when(jnp.logical_or(mode == W_FIRST_HALF, mode == W_ALL))
        def _():
            for c in first_half:
                w_copy(e_next, c).start()

        @pl.when(mode == W_SECOND_HALF)
        def _():
            for c in second_half:
                w_copy(e_next, c).start()

        all_pieces(lambda k: in_copy(i, k, slot).wait())
        gu = jnp.dot(xs_ref[slot], wgu_b[p], preferred_element_type=F32) + bgu_ref[0]

        @pl.when(mode == W_ALL)
        def _():
            w_finish(e_next, 1 - p, first_half, True)

        glu = jnp.minimum(gu[:, :EXPERT_FF], SWIGLU_LIMIT)
        lin = jnp.clip(gu[:, EXPERT_FF:], -SWIGLU_LIMIT, SWIGLU_LIMIT)
        act = glu * jax.nn.sigmoid(SWIGLU_ALPHA * glu) * (lin + 1.0)

        @pl.when(i >= 2)
        def _():
            all_pieces(lambda k: out_copy(i - 2, k, slot).wait())

        y = jnp.dot(act.astype(BF16), wd_b[p], preferred_element_type=F32) + bd_ref[0]
        ys_ref[slot] = y.astype(ys_ref.dtype)
        all_pieces(lambda k: out_copy(i, k, slot).start())

        @pl.when(mode == W_FIRST_HALF)
        def _():
            w_finish(e_next, 1 - p, first_half, False)

        @pl.when(jnp.logical_or(mode == W_SECOND_HALF, mode == W_ALL))
        def _():
            w_finish(e_next, 1 - p, second_half, False)

    @pl.when(i == nu - 1)
    def _():
        @pl.when(i >= 1)
        def _():
            all_pieces(lambda k: out_copy(i - 1, k, 1 - slot).wait())
        all_pieces(lambda k: out_copy(i, k, slot).wait())


def expert_ffn(block_e, n_used, piece_src, piece_dst, w_parity, next_e, w_mode, xbt, wgu, bgu, wd, bd, layer):
    rows, D = xbt.shape
    _, E, _, F2 = wgu.shape
    nblk = block_e.shape[0]
    bias = lambda w: pl.BlockSpec((1, 1, w), lambda i, be, *_: (be[i], 0, 0))
    grid_spec = pltpu.PrefetchScalarGridSpec(
        num_scalar_prefetch=7,
        grid=(nblk,),
        in_specs=[pl.BlockSpec(memory_space=pl.ANY), pl.BlockSpec(memory_space=pl.ANY), bias(F2),
                  pl.BlockSpec(memory_space=pl.ANY), bias(D)],
        out_specs=pl.BlockSpec(memory_space=pl.ANY),
        scratch_shapes=[pltpu.VMEM((2, MOE_BM, D), BF16), pltpu.VMEM((2, MOE_BM, D), BF16),
                        pltpu.VMEM((2, D, F2), BF16), pltpu.VMEM((2, F2 // 2, D), BF16),
                        pltpu.VMEM((W_STAGES, W_CHUNK_ROWS, D), F32),
                        pltpu.SemaphoreType.DMA((2,)), pltpu.SemaphoreType.DMA((2,)),
                        pltpu.SemaphoreType.DMA((W_STAGES,))],
    )
    return pl.pallas_call(
        functools.partial(_expert_kernel, layer=layer),
        out_shape=jax.ShapeDtypeStruct((rows, D), BF16),
        grid_spec=grid_spec,
        input_output_aliases={7: 0},
        compiler_params=_cparams(("arbitrary",)),
        name="expert_ffn",
    )(block_e, n_used, piece_src, piece_dst, w_parity, next_e, w_mode, xbt, wgu, bgu.reshape(E, 1, F2), wd,
      bd.reshape(E, 1, D))


COMBINE_COLS = 512


def _combine_kernel(meta_ref, wt_ref, x1_ref, gate_ref, g_ref, b_ref, sc_ref, sh_ref, y_ref,
                    x2_ref, hb_ref, acc_ref):
    meta = meta_ref[...]
    wts = wt_ref[...]
    CW = COMBINE_COLS
    col0 = lax.broadcasted_iota(jnp.int32, (meta.shape[0], CW), 1)
    for cc in range(TILE_ROWS // CW):
        hi = jnp.zeros(col0.shape, F32)
        lo = jnp.zeros(col0.shape, F32)
        for kq in range(TOP_K):
            hit = col0 == meta[:, kq:kq + 1] - cc * CW
            w = wts[:, kq:kq + 1]
            w_hi = w.astype(BF16).astype(F32)
            hi = jnp.where(hit, w_hi, hi)
            lo = jnp.where(hit, w - w_hi, lo)
        rows = y_ref[pl.ds(cc * CW, CW), :]
        part = (jnp.dot(hi.astype(BF16), rows, preferred_element_type=F32)
                + jnp.dot(lo.astype(BF16), rows, preferred_element_type=F32))
        if cc == 0:
            acc_ref[...] = part
        else:
            acc_ref[...] += part

    x2 = _layer_norm(DEEPNORM_ALPHA * x1_ref[...] + gate_ref[0] * acc_ref[...], g_ref[...], b_ref[...])
    x2_ref[...] = x2
    hb_ref[...] = (x2 * (1.0 + sc_ref[0]) + sh_ref[0]).astype(hb_ref.dtype)


def moe_combine_ln(ybt, meta, wt, x1, gate2, ln_g, ln_b, scale_next, shift_next, S):
    T, D = x1.shape
    tm = ROUTE_TM
    spb = S // tm
    rowblk = lambda w: pl.BlockSpec((tm, w), lambda i: (i, 0))
    full = pl.BlockSpec((1, D), lambda i: (0, 0))
    perb = pl.BlockSpec((1, 1, D), lambda i: (i // spb, 0, 0))
    return pl.pallas_call(
        _combine_kernel,
        out_shape=(jax.ShapeDtypeStruct((T, D), F32), jax.ShapeDtypeStruct((T, D), BF16)),
        grid=(T // tm,),
        in_specs=[rowblk(LANES), rowblk(LANES), rowblk(D), perb, full, full, perb, perb,
                  pl.BlockSpec((TILE_ROWS, D), lambda i: (i, 0))],
        out_specs=(rowblk(D), rowblk(D)),
        scratch_shapes=[pltpu.VMEM((tm, D), F32)],
        compiler_params=_cparams(("parallel",)),
        name="moe_combine_ln",
    )(meta, wt, x1, gate2, ln_g.reshape(1, D), ln_b.reshape(1, D), scale_next, shift_next, ybt)


def routed_ffn_ln(h2, idx, wt, rank, tile_cnt, wgu, bgu, wd, bd, layer, x1, gate2, ln_g, ln_b,
                  scale_next, shift_next, S):
    T, D = h2.shape
    E = N_EXPERTS
    nt = T // ROUTE_TM
    i32 = jnp.int32
    e4 = idx[:, :TOP_K]
    cnt = tile_cnt[:, 0, :E]
    npc = (cnt + PIECE - 1) // PIECE
    seg_off = (jnp.cumsum(npc, axis=1) - npc) * PIECE
    onehot = e4[:, :, None] == jnp.arange(E, dtype=i32)[None, None, :]
    off_tok = jnp.repeat(seg_off, ROUTE_TM, axis=0)[:, None, :]
    lrow = jnp.sum(jnp.where(onehot, off_tok, 0), axis=-1) + rank[:, :TOP_K]
    per_e = jnp.sum(npc, axis=0)
    per_e_pad = ((per_e + BLOCK_PIECES - 1) // BLOCK_PIECES) * BLOCK_PIECES
    e_end = jnp.cumsum(per_e_pad)
    e_start = e_end - per_e_pad
    t_end = jnp.cumsum(npc, axis=0).T
    t_start = t_end - npc.T
    max_pieces = nt * ((ROUTE_TM * TOP_K + E * (PIECE - 1)) // PIECE) + E * (BLOCK_PIECES - 1)
    nblk = -(-max_pieces // BLOCK_PIECES)
    q = jnp.arange(nblk * BLOCK_PIECES, dtype=i32)
    e_q = jnp.minimum(jnp.sum((e_end[None, :] <= q[:, None]).astype(i32), axis=1), E - 1)
    sel_e = e_q[:, None] == jnp.arange(E, dtype=i32)[None, :]
    of_expert = lambda table: jnp.sum(jnp.where(sel_e, table[None, :], 0), axis=1)
    w_q = q - of_expert(e_start)
    real = w_q < of_expert(per_e)
    rows_of = lambda table: jnp.sum(jnp.where(sel_e[:, :, None], table[None], 0), axis=1)
    t_q = jnp.minimum(jnp.sum((rows_of(t_end) <= w_q[:, None]).astype(i32), axis=1), nt - 1)
    sel_t = t_q[:, None] == jnp.arange(nt, dtype=i32)[None, :]
    of_tile = lambda table: jnp.sum(jnp.where(sel_t, rows_of(table), 0), axis=1)
    k_q = w_q - of_tile(t_start)
    row_q = t_q * TILE_ROWS + of_tile(seg_off.T) + k_q * PIECE
    scratch_q = nt * TILE_ROWS + (((q // BLOCK_PIECES) % 2) * BLOCK_PIECES + q % BLOCK_PIECES) * PIECE
    piece_src = jnp.where(real, row_q, 0).astype(i32)
    piece_dst = jnp.where(real, row_q, scratch_q).astype(i32)
    block_e = e_q[::BLOCK_PIECES]
    n_used = (e_end[-1] // BLOCK_PIECES).astype(i32)
    blk = jnp.arange(nblk, dtype=i32)
    nxt = jnp.concatenate([block_e[1:], block_e[-1:]])
    switch_next = jnp.logical_and(nxt != block_e, blk + 1 < n_used)
    w_parity = (jnp.cumsum(switch_next.astype(i32)) - switch_next.astype(i32)) % 2
    prv = jnp.concatenate([block_e[:1] - 1, block_e[:-1]])
    only_block = jnp.logical_and(switch_next, prv != block_e)
    before_last = jnp.logical_and(jnp.concatenate([switch_next[1:], switch_next[-1:] & False]), nxt == block_e)
    w_mode = jnp.where(only_block, W_ALL, jnp.where(switch_next, W_SECOND_HALF,
                                                    jnp.where(before_last, W_FIRST_HALF, W_NONE))).astype(i32)
    nxt2 = jnp.concatenate([block_e[2:], block_e[-1:], block_e[-1:]])
    next_e = jnp.where(switch_next, nxt, jnp.where(before_last, nxt2, 0)).astype(i32)

    lrow_t = jnp.full((nt + 1, SUBLANES, ROUTE_TM), -1, i32).at[:nt, :TOP_K, :].set(
        lrow.reshape(nt, ROUTE_TM, TOP_K).transpose(0, 2, 1))
    xbt = moe_dispatch(h2, lrow_t)
    ybt = expert_ffn(block_e, n_used.reshape(1), piece_src, piece_dst, w_parity, next_e, w_mode, xbt, wgu, bgu, wd,
                     bd, layer)
    meta = jnp.full((T, LANES), -1, i32).at[:, :TOP_K].set(lrow)
    return moe_combine_ln(ybt, meta, wt, x1, gate2, ln_g, ln_b, scale_next, shift_next, S)


def _mix_weights(w_in_l):
    def cols(rng, pad=0):
        part = w_in_l[:, rng[0]:rng[1]]
        if pad:
            part = jnp.pad(part, ((0, 0), (0, pad)))
        return part
    w_mix = jnp.concatenate([
        cols(SRC_POOL), cols(SRC_GLA_V), cols(SRC_GLA_G), cols(SRC_RWKV_RKV), cols(SRC_ATT),
        cols(SRC_GLA_Q), cols(SRC_GLA_K),
        cols(SRC_RWKV_LO, RWKV_LO_PAD - (SRC_RWKV_LO[1] - SRC_RWKV_LO[0])),
        cols(SRC_GLA_A, GLA_A_PAD - (SRC_GLA_A[1] - SRC_GLA_A[0]))], axis=1).astype(BF16)
    w_gates = w_in_l[:, SRC_GATES[0]:].astype(BF16)
    return w_mix, w_gates


def kernel(x, c, ada_w, ada_b, w_in, pool_w, pool_scale, gla_w_alpha, gla_b_alpha, gla_norm_g, rwkv_mu, rwkv_w0, rwkv_w2, rwkv_a0, rwkv_a2, rwkv_g2, rwkv_k_k, rwkv_k_a, rwkv_r_k, rwkv_ln_g, rwkv_ln_b, w_branch_a, w_branch_b, w_branch_c, w_branch_d, w_out, ln1_g, ln1_b, router_w, router_b, w_gate_up, b_gate_up, w_down, b_down, ln2_g, ln2_b):
    B, S, D = x.shape
    T = B * S
    mod = ada_modulation(c, ada_w, ada_b)
    mods = [[mod[l, :, None, i * D:(i + 1) * D] for i in range(6)] for l in range(DEPTH)]
    x2d = x.reshape(T, D)
    hb2d = modulate(x, mods[0][1], mods[0][0]).reshape(T, D)
    for l in range(DEPTH):
        shift1, scale1, gate1, shift2, scale2, gate2 = mods[l]
        shift_next, scale_next = (mods[l + 1][0], mods[l + 1][1]) if l + 1 < DEPTH else (shift1, scale1)
        w_mix, w_gates = _mix_weights(w_in[l])
        P = matmul(hb2d, w_mix, 1024, MIX_TN).reshape(B, S, MIX_WIDTH)
        y_a = pool_mixer(P, pool_w[l], pool_scale[l])
        y_b = dilated_attention(P)
        y_c = gla_mixer(P, gla_w_alpha[l], gla_b_alpha[l], gla_norm_g[l])
        r_, w_, k_, v_, kk_, b_, gate_, bonus_ = rwkv_prep(
            P, rwkv_mu[l], rwkv_w0[l], rwkv_w2[l], rwkv_a0[l], rwkv_a2[l], rwkv_g2[l],
            rwkv_k_k[l], rwkv_k_a[l], rwkv_r_k[l].reshape(-1))
        y_t = rwkv_chunked(r_, w_, k_, v_, kk_, b_)
        y_d = rwkv_post(y_t, gate_, bonus_, rwkv_ln_g[l], rwkv_ln_b[l])
        ys = [y.reshape(T, -1) for y in (y_a, y_b, y_c, y_d)]
        ws = [w[l].astype(BF16) for w in (w_branch_a, w_branch_b, w_branch_c, w_branch_d)]
        merged = branch_merge(hb2d, w_gates, ys, ws)
        x1, h2, idx, wt, rank, tile_cnt = out_ln_router(
            merged, w_out[l].astype(BF16), x2d, gate1, ln1_g[l], ln1_b[l], scale2, shift2,
            router_w[l], router_b[l], S)
        x2d, hb2d = routed_ffn_ln(h2, idx, wt, rank, tile_cnt, w_gate_up, b_gate_up[l], w_down, b_down[l], l,
                                  x1, gate2, ln2_g[l], ln2_b[l], scale_next, shift_next, S)
    return x2d.reshape(B, S, D)
```

```python
import functools

import numpy as np
import jax
import jax.numpy as jnp
from jax import lax
from jax.experimental import pallas as pl
from jax.experimental.pallas import tpu as pltpu

F32 = jnp.float32
BF16 = jnp.bfloat16
HIGHEST = lax.Precision.HIGHEST

D_MODEL = 2048
DEPTH = 2
POOL_WINDOWS = (2, 4, 8, 16)
POOL_GROUP = 128
POOL_WIDTH = 512
ATT_GROUPS = ((128, 1), (512, 4), (2048, 16))
ATT_HEAD_DIM = 64
ATT_HEADS = 12
ATT_WIDTH = 768
ATT_OUT_WIDTH = 256
ATT_BLOCK = 128
ALIBI_SLOPES = tuple(2.0 ** (-8.0 * (h + 1) / ATT_HEADS) for h in range(ATT_HEADS))
GLA_HEADS = 4
GLA_DK = 64
GLA_DV = 128
GLA_KEY_WIDTH = 256
GLA_VAL_WIDTH = 512
GLA_GATE_RANK = 16
GLA_GATE_TEMP = 16.0
GLA_CHUNK = 32
GLA_NORM_EPS = 1e-6
RWKV_HEADS = 8
RWKV_HEAD_DIM = 64
RWKV_WIDTH = 512
RWKV_DECAY_RANK = 32
RWKV_AAA_RANK = 32
RWKV_GATE_RANK = 96
RWKV_GN_EPS = 64e-5
N_BRANCHES = 4
N_EXPERTS = 32
TOP_K = 4
EXPERT_FF = 1024
SWIGLU_LIMIT = 7.0
SWIGLU_ALPHA = 1.702
LN_EPS = 1e-5
DEEPNORM_ALPHA = (2 * DEPTH) ** 0.25

LANES = 128
SUBLANES = 8
VMEM_LIMIT = 56 * 1024 * 1024

SRC_POOL = (0, 512)
SRC_ATT = (512, 2816)
SRC_GLA_Q = (2816, 3072)
SRC_GLA_K = (3072, 3328)
SRC_GLA_V = (3328, 3840)
SRC_GLA_G = (3840, 4352)
SRC_GLA_A = (4352, 4368)
SRC_RWKV_RKV = (4368, 5904)
SRC_RWKV_LO = (5904, 6064)
SRC_GATES = (6064, 14256)
OFF_POOL = 0
OFF_GLA_V = 512
OFF_GLA_G = 1024
OFF_RWKV_RKV = 1536
OFF_ATT_Q = 3072
OFF_ATT_K = OFF_ATT_Q + ATT_WIDTH
OFF_ATT_V = OFF_ATT_K + ATT_WIDTH
OFF_GLA_Q = 5376
OFF_GLA_K = 5632
OFF_RWKV_LO = 5888
RWKV_LO_PAD = 256
OFF_GLA_A = 6144
GLA_A_PAD = 128
MIX_WIDTH = 6272
MIX_TN = 896

MOE_BM = 256
NEG_BIG = -1e30


def _cparams(sem):
    return pltpu.CompilerParams(dimension_semantics=sem, vmem_limit_bytes=VMEM_LIMIT)


def _ada_kernel(c_ref, w_ref, b_ref, o_ref):
    c = c_ref[...]
    s = c * jax.nn.sigmoid(c)
    o_ref[0] = jnp.dot(s, w_ref[0], precision=HIGHEST, preferred_element_type=F32) + b_ref[0]


def ada_modulation(c, ada_w, ada_b):
    L, D, N = ada_w.shape
    B = c.shape[0]
    cp = jnp.zeros((SUBLANES, D), F32).at[:B].set(c)
    tn = 1024
    out = pl.pallas_call(
        _ada_kernel,
        out_shape=jax.ShapeDtypeStruct((L, SUBLANES, N), F32),
        grid=(L, N // tn),
        in_specs=[pl.BlockSpec((SUBLANES, D), lambda l, j: (0, 0)),
                  pl.BlockSpec((1, D, tn), lambda l, j: (l, 0, j)),
                  pl.BlockSpec((1, 1, tn), lambda l, j: (l, 0, j))],
        out_specs=pl.BlockSpec((1, SUBLANES, tn), lambda l, j: (l, 0, j)),
        compiler_params=_cparams(("parallel", "parallel")),
        name="ada_modulation",
    )(cp, ada_w, ada_b.reshape(L, 1, N))
    return out[:, :B]


def _modulate_kernel(x_ref, sc_ref, sh_ref, o_ref):
    o_ref[0] = (x_ref[0] * (1.0 + sc_ref[0]) + sh_ref[0]).astype(o_ref.dtype)


def modulate(x, scale, shift):
    B, S, D = x.shape
    ts = 1024
    return pl.pallas_call(
        _modulate_kernel,
        out_shape=jax.ShapeDtypeStruct((B, S, D), BF16),
        grid=(B, S // ts),
        in_specs=[pl.BlockSpec((1, ts, D), lambda b, i: (b, i, 0)),
                  pl.BlockSpec((1, 1, D), lambda b, i: (b, 0, 0)),
                  pl.BlockSpec((1, 1, D), lambda b, i: (b, 0, 0))],
        out_specs=pl.BlockSpec((1, ts, D), lambda b, i: (b, i, 0)),
        compiler_params=_cparams(("parallel", "parallel")),
        name="modulate",
    )(x, scale, shift)


def _mm_kernel(x_ref, w_ref, o_ref):
    o_ref[...] = jnp.dot(x_ref[...], w_ref[...], preferred_element_type=F32).astype(o_ref.dtype)


def matmul(x, w, tm, tn, out_dtype=F32):
    M, K = x.shape
    N = w.shape[1]
    return pl.pallas_call(
        _mm_kernel,
        out_shape=jax.ShapeDtypeStruct((M, N), out_dtype),
        grid=(N // tn, M // tm),
        in_specs=[pl.BlockSpec((tm, K), lambda j, i: (i, 0)),
                  pl.BlockSpec((K, tn), lambda j, i: (0, j))],
        out_specs=pl.BlockSpec((tm, tn), lambda j, i: (i, j)),
        compiler_params=_cparams(("parallel", "parallel")),
        name="matmul",
    )(x, w)


def _pool_kernel(p_ref, w_ref, sc_ref, o_ref):
    g = pl.program_id(1)
    v = p_ref[0]
    S = v.shape[0]
    row = lax.broadcasted_iota(jnp.int32, v.shape, 0)
    win = jnp.left_shift(2, g)
    s = v
    pooled_sum = v
    for k, sh in enumerate((1, 2, 4, 8)):
        s = s + jnp.where(row >= sh, pltpu.roll(s, sh, 0), 0.0)
        pooled_sum = jnp.where(g >= k, s, pooled_sum)
    cnt = jnp.minimum(row + 1, win).astype(F32)
    diff = pooled_sum / cnt - v
    y = jnp.dot(diff.astype(BF16), w_ref[0].astype(BF16), preferred_element_type=F32)
    o_ref[0] = y * sc_ref[...]


def pool_mixer(P, pool_w, pool_scale):
    B, S, _ = P.shape
    G = len(POOL_WINDOWS)
    return pl.pallas_call(
        _pool_kernel,
        out_shape=jax.ShapeDtypeStruct((B, S, POOL_WIDTH), F32),
        grid=(B, G),
        in_specs=[pl.BlockSpec((1, S, POOL_GROUP), lambda b, g: (b, 0, OFF_POOL // POOL_GROUP + g)),
                  pl.BlockSpec((1, POOL_GROUP, POOL_GROUP), lambda b, g: (g, 0, 0)),
                  pl.BlockSpec((1, POOL_GROUP), lambda b, g: (0, g))],
        out_specs=pl.BlockSpec((1, S, POOL_GROUP), lambda b, g: (b, 0, g)),
        compiler_params=_cparams(("parallel", "parallel")),
        name="pool_mixer",
    )(P, pool_w, pool_scale.reshape(1, POOL_WIDTH))


ATT_GROUP_HEADS = 4
ATT_RESIDUE_UNROLL = 4


def _att_kernel(q_ref, kc_ref, kp_ref, vc_ref, vp_ref, num_ref, den_ref, m_ref, *, slopes, dilation, heads):
    if heads == ATT_GROUP_HEADS:
        n = pl.program_id(1)
    else:
        pair = pl.program_id(1)
        n = pl.program_id(2)
        slopes = [jnp.where(pair == 0, slopes[h], slopes[heads + h]) for h in range(heads)]
    i = lax.broadcasted_iota(jnp.int32, (ATT_BLOCK, ATT_BLOCK), 0)
    j = lax.broadcasted_iota(jnp.int32, (ATT_BLOCK, ATT_BLOCK), 1)
    dist_c = i - j
    dist_p = dist_c + ATT_BLOCK
    valid_c = dist_c >= 0
    valid_p = jnp.logical_and(dist_p <= ATT_BLOCK, n > 0)
    bias_c = [-(s * dilation) * dist_c.astype(F32) for s in slopes]
    bias_p = [-(s * dilation) * dist_p.astype(F32) for s in slopes]
    head_of_lane = lax.broadcasted_iota(jnp.int32, (ATT_BLOCK, heads * ATT_HEAD_DIM), 1) // ATT_HEAD_DIM
    dn = (((1,), (1,)), ((), ()))

    def per_head_lanes(cols):
        out = cols[-1]
        for h in range(heads - 2, -1, -1):
            out = jnp.where(head_of_lane == h, cols[h], out)
        return out

    def residue(r):
        rows = slice(None) if dilation == 1 else pl.ds(r, ATT_BLOCK, stride=dilation)
        q4, kc4, kp4, vc4, vp4 = (ref[0, rows, :] for ref in (q_ref, kc_ref, kp_ref, vc_ref, vp_ref))
        nums, dens, ms = [], [], []
        for h in range(heads):
            sl = slice(h * ATT_HEAD_DIM, (h + 1) * ATT_HEAD_DIM)
            q = (q4[:, sl] * (ATT_HEAD_DIM ** -0.5)).astype(BF16)
            s_c = lax.dot_general(q, kc4[:, sl].astype(BF16), dn, preferred_element_type=F32)
            s_p = lax.dot_general(q, kp4[:, sl].astype(BF16), dn, preferred_element_type=F32)
            s_c = jnp.where(valid_c, s_c + bias_c[h], NEG_BIG)
            s_p = jnp.where(valid_p, s_p + bias_p[h], NEG_BIG)
            m = jnp.maximum(jnp.max(s_c, -1, keepdims=True), jnp.max(s_p, -1, keepdims=True))
            p_c = jnp.exp(s_c - m)
            p_p = jnp.exp(s_p - m)
            dens.append(jnp.sum(p_c, -1, keepdims=True) + jnp.sum(p_p, -1, keepdims=True))
            nums.append(jnp.dot(p_c.astype(BF16), vc4[:, sl].astype(BF16), preferred_element_type=F32)
                        + jnp.dot(p_p.astype(BF16), vp4[:, sl].astype(BF16), preferred_element_type=F32))
            ms.append(m)
        num_ref[0, rows, :] = jnp.concatenate(nums, axis=-1)
        den_ref[0, rows, :] = per_head_lanes(dens)
        m_ref[0, rows, :] = per_head_lanes(ms)

    if dilation == 1:
        residue(0)
    else:
        def body(it, carry):
            for u in range(ATT_RESIDUE_UNROLL):
                residue(it * ATT_RESIDUE_UNROLL + u)
            return carry

        lax.fori_loop(0, dilation // ATT_RESIDUE_UNROLL, body, 0)


def att_group(P, g, dilation):
    B, S, NP = P.shape
    rows = ATT_BLOCK * dilation
    nb = S // rows
    heads = ATT_GROUP_HEADS if dilation == 1 else 2
    width = heads * ATT_HEAD_DIM
    npairs = ATT_GROUP_HEADS // heads
    qo, ko, vo = (o // width + g * npairs for o in (OFF_ATT_Q, OFF_ATT_K, OFF_ATT_V))
    blk = (1, rows, width)
    if npairs == 1:
        grid, sem = (B, nb), ("parallel", "arbitrary")
        cur = lambda off: pl.BlockSpec(blk, lambda b, n: (b, n, off))
        prev = lambda off: pl.BlockSpec(blk, lambda b, n: (b, jnp.maximum(n - 1, 0), off))
        ospec = pl.BlockSpec(blk, lambda b, n: (b, n, 0))
    else:
        grid, sem = (B, npairs, nb), ("parallel", "parallel", "arbitrary")
        cur = lambda off: pl.BlockSpec(blk, lambda b, hp, n: (b, n, off + hp))
        prev = lambda off: pl.BlockSpec(blk, lambda b, hp, n: (b, jnp.maximum(n - 1, 0), off + hp))
        ospec = pl.BlockSpec(blk, lambda b, hp, n: (b, n, hp))
    oshape = jax.ShapeDtypeStruct((B, S, ATT_OUT_WIDTH), F32)
    slopes = ALIBI_SLOPES[g * ATT_GROUP_HEADS:(g + 1) * ATT_GROUP_HEADS]
    return pl.pallas_call(
        functools.partial(_att_kernel, slopes=slopes, dilation=dilation, heads=heads),
        out_shape=(oshape, oshape, oshape),
        grid=grid,
        in_specs=[cur(qo), cur(ko), prev(ko), cur(vo), prev(vo)],
        out_specs=(ospec, ospec, ospec),
        compiler_params=_cparams(sem),
        name=f"dilated_attention_g{g}",
    )(P, P, P, P, P)


def _att_merge_kernel(*refs):
    o_ref = refs[-1]
    nums, dens, ms = refs[0:3], refs[3:6], refs[6:9]
    mx = jnp.maximum(jnp.maximum(ms[0][0], ms[1][0]), ms[2][0])
    num = jnp.zeros_like(mx)
    den = jnp.zeros_like(mx)
    for g in range(3):
        e = jnp.exp(ms[g][0] - mx)
        num = num + nums[g][0] * e
        den = den + dens[g][0] * e
    o_ref[0] = num / den


def dilated_attention(P):
    B, S, _ = P.shape
    parts = [att_group(P, g, d) for g, (_, d) in enumerate(ATT_GROUPS)]
    args = [p[0] for p in parts] + [p[1] for p in parts] + [p[2] for p in parts]
    ts = 1024
    spec = pl.BlockSpec((1, ts, ATT_OUT_WIDTH), lambda b, i: (b, i, 0))
    return pl.pallas_call(
        _att_merge_kernel,
        out_shape=jax.ShapeDtypeStruct((B, S, ATT_OUT_WIDTH), F32),
        grid=(B, S // ts),
        in_specs=[spec] * 9,
        out_specs=spec,
        compiler_params=_cparams(("parallel", "parallel")),
        name="dilated_attention_merge",
    )(*args)


GLA_TS = 256


def _gla_kernel(q_ref, k_ref, v_ref, g_ref, a_ref, wa_ref, ba_ref, ng_ref, o_ref, st_ref):
    @pl.when(pl.program_id(1) == 0)
    def _():
        st_ref[...] = jnp.zeros_like(st_ref)

    TS = GLA_TS
    nchunk = TS // GLA_CHUNK
    logit = jnp.dot(a_ref[0], wa_ref[...], precision=HIGHEST, preferred_element_type=F32) + ba_ref[...]
    log_a = (jnp.minimum(logit, 0.0) - jnp.log1p(jnp.exp(-jnp.abs(logit)))) / GLA_GATE_TEMP
    i = lax.broadcasted_iota(jnp.int32, (TS, TS), 0)
    j = lax.broadcasted_iota(jnp.int32, (TS, TS), 1)
    same_chunk_causal = jnp.logical_and(i // GLA_CHUNK == j // GLA_CHUNK, j <= i)
    tri = same_chunk_causal.astype(F32)
    bcum = jnp.dot(tri, log_a, precision=HIGHEST, preferred_element_type=F32)
    eb = jnp.exp(bcum)
    qd = q_ref[0] * (GLA_DK ** -0.5) * eb
    kd = k_ref[0] * jnp.exp(-bcum)
    k_all = k_ref[0]
    v_all = v_ref[0]
    dn_nt = (((1,), (1,)), ((), ()))
    dn_tn = (((0,), (0,)), ((), ()))
    for h in range(GLA_HEADS):
        ks = slice(h * GLA_DK, (h + 1) * GLA_DK)
        vs = slice(h * GLA_DV, (h + 1) * GLA_DV)
        qh = qd[:, ks].astype(BF16)
        vh = v_all[:, vs].astype(BF16)
        scores = lax.dot_general(qh, kd[:, ks].astype(BF16), dn_nt, preferred_element_type=F32)
        scores = jnp.where(same_chunk_causal, scores, 0.0)
        o_intra = jnp.dot(scores.astype(BF16), vh, preferred_element_type=F32)
        st = st_ref[h]
        outs = []
        for c in range(nchunk):
            rs = slice(c * GLA_CHUNK, (c + 1) * GLA_CHUNK)
            last = (c + 1) * GLA_CHUNK - 1
            o_c = o_intra[rs] + lax.dot_general(qh[rs], st.astype(BF16), dn_nt, preferred_element_type=F32)
            outs.append(o_c)
            b_last = bcum[last:last + 1, ks]
            k_tail = (k_all[rs, ks] * jnp.exp(b_last - bcum[rs, ks])).astype(BF16)
            st = st * jnp.exp(b_last) + lax.dot_general(vh[rs], k_tail, dn_tn, preferred_element_type=F32)
        st_ref[h] = st
        o = jnp.concatenate(outs, axis=0)
        o = o * lax.rsqrt(jnp.mean(o * o, -1, keepdims=True) + GLA_NORM_EPS)
        gg = g_ref[0][:, vs]
        o_ref[0, :, vs] = o * ng_ref[:, vs] * (gg * jax.nn.sigmoid(gg))


def gla_mixer(P, w_alpha, b_alpha, norm_g):
    B, S, _ = P.shape
    TS = GLA_TS
    wa = jnp.zeros((GLA_A_PAD, GLA_KEY_WIDTH), F32).at[:GLA_GATE_RANK].set(w_alpha)
    col = lambda off, w: pl.BlockSpec((1, TS, w), lambda b, i: (b, i, off // w))
    full = lambda r, c: pl.BlockSpec((r, c), lambda b, i: (0, 0))
    return pl.pallas_call(
        _gla_kernel,
        out_shape=jax.ShapeDtypeStruct((B, S, GLA_VAL_WIDTH), F32),
        grid=(B, S // TS),
        in_specs=[col(OFF_GLA_Q, GLA_KEY_WIDTH), col(OFF_GLA_K, GLA_KEY_WIDTH),
                  col(OFF_GLA_V, GLA_VAL_WIDTH), col(OFF_GLA_G, GLA_VAL_WIDTH),
                  col(OFF_GLA_A, GLA_A_PAD),
                  full(GLA_A_PAD, GLA_KEY_WIDTH), full(1, GLA_KEY_WIDTH), full(1, GLA_VAL_WIDTH)],
        out_specs=pl.BlockSpec((1, TS, GLA_VAL_WIDTH), lambda b, i: (b, i, 0)),
        scratch_shapes=[pltpu.VMEM((GLA_HEADS, GLA_DV, GLA_DK), F32)],
        compiler_params=_cparams(("parallel", "arbitrary")),
        name="gla_mixer",
    )(P, P, P, P, P, wa, b_alpha.reshape(1, -1), norm_g.reshape(1, -1))


RWKV_PREP_TS = 512
RWKV_SHIFT_LO = RWKV_DECAY_RANK + RWKV_AAA_RANK + RWKV_GATE_RANK


def _split_bf16(x):
    hi = x.astype(BF16)
    return hi, (x - hi.astype(F32)).astype(BF16)


def _segment_sum(x, seg):
    n = x.shape[-1]
    i = lax.broadcasted_iota(jnp.int32, (n, n), 0)
    j = lax.broadcasted_iota(jnp.int32, (n, n), 1)
    ones = jnp.where(i // seg == j // seg, 1.0, 0.0).astype(BF16)
    hi, lo = _split_bf16(x)
    return jnp.dot(hi, ones, preferred_element_type=F32) + jnp.dot(lo, ones, preferred_element_type=F32)


def _rwkv_prep_kernel(rkv_ref, lo_ref, rkv_prev_ref, lo_prev_ref, mu_rkv_ref, mu_lo_ref, w0_ref, w2_ref,
                      a0_ref, a2_ref, g2_ref, kk_ref, ka_ref, rk_ref,
                      r_out, w_out, k_out, v_out, kk_out, b_out, gate_out, bonus_out):
    first = pl.program_id(1) == 0

    def shifted(cur, prev_ref, mu):
        prev_row = jnp.where(first, 0.0, prev_ref[0][SUBLANES - 1:SUBLANES, :])
        row = lax.broadcasted_iota(jnp.int32, cur.shape, 0)
        prev = jnp.where(row == 0, prev_row, pltpu.roll(cur, 1, 0))
        return cur + (prev - cur) * mu

    xs = shifted(rkv_ref[0], rkv_prev_ref, mu_rkv_ref[...])
    lo = shifted(lo_ref[0], lo_prev_ref, mu_lo_ref[...])
    W = RWKV_WIDTH
    r, k, v = xs[:, 0:W], xs[:, W:2 * W], xs[:, 2 * W:3 * W]
    hdot = functools.partial(jnp.dot, precision=HIGHEST, preferred_element_type=F32)
    z = w0_ref[...] + hdot(jnp.tanh(lo), w2_ref[...])
    log_decay = -jax.nn.sigmoid(z) * float(np.exp(-0.5))
    a = jax.nn.sigmoid(a0_ref[...] + hdot(lo, a2_ref[...]))
    gate = hdot(jax.nn.sigmoid(lo), g2_ref[...])
    kk = k * kk_ref[...]
    kk_norm = jnp.sqrt(_segment_sum(kk * kk, RWKV_HEAD_DIM))
    kk = kk / jnp.maximum(kk_norm, 1e-12)
    k2 = k * (1.0 + (a - 1.0) * ka_ref[...])
    bonus = _segment_sum(r * k2 * rk_ref[...], RWKV_HEAD_DIM) * v
    r_out[0] = r
    w_out[0] = log_decay
    k_out[0] = k2
    v_out[0] = v
    kk_out[0] = kk
    b_out[0] = kk * a
    gate_out[0] = gate
    bonus_out[0] = bonus


def rwkv_prep(P, mu, w0, w2, a0, a2, g2, k_k, k_a, r_k):
    B, S, _ = P.shape
    TS = RWKV_PREP_TS
    W = RWKV_WIDTH
    W3 = 3 * W
    n_rkv = SRC_RWKV_RKV[1] - SRC_RWKV_RKV[0]
    mu_rkv = mu[:n_rkv].reshape(1, W3)
    mu_lo = jnp.zeros((1, RWKV_LO_PAD), F32).at[0, :RWKV_SHIFT_LO].set(mu[n_rkv:])
    w2p = jnp.zeros((RWKV_LO_PAD, W), F32).at[0:RWKV_DECAY_RANK].set(w2)
    a2p = jnp.zeros((RWKV_LO_PAD, W), F32).at[RWKV_DECAY_RANK:RWKV_DECAY_RANK + RWKV_AAA_RANK].set(a2)
    g2p = jnp.zeros((RWKV_LO_PAD, W), F32).at[RWKV_DECAY_RANK + RWKV_AAA_RANK:RWKV_SHIFT_LO].set(g2)
    row = lambda a: a.reshape(1, -1)
    full = lambda r, c: pl.BlockSpec((r, c), lambda b, i: (0, 0))
    tpb = TS // SUBLANES
    prev_map = lambda off, w: pl.BlockSpec((1, SUBLANES, w), lambda b, i: (b, jnp.maximum(i * tpb - 1, 0), off // w))
    oshape = jax.ShapeDtypeStruct((B, S, W), F32)
    ospec = pl.BlockSpec((1, TS, W), lambda b, i: (b, i, 0))
    return pl.pallas_call(
        _rwkv_prep_kernel,
        out_shape=(oshape,) * 8,
        grid=(B, S // TS),
        in_specs=[pl.BlockSpec((1, TS, W3), lambda b, i: (b, i, OFF_RWKV_RKV // W3)),
                  pl.BlockSpec((1, TS, RWKV_LO_PAD), lambda b, i: (b, i, OFF_RWKV_LO // RWKV_LO_PAD)),
                  prev_map(OFF_RWKV_RKV, W3), prev_map(OFF_RWKV_LO, RWKV_LO_PAD),
                  full(1, W3), full(1, RWKV_LO_PAD), full(1, W), full(RWKV_LO_PAD, W),
                  full(1, W), full(RWKV_LO_PAD, W), full(RWKV_LO_PAD, W),
                  full(1, W), full(1, W), full(1, W)],
        out_specs=(ospec,) * 8,
        compiler_params=_cparams(("parallel", "arbitrary")),
        name="rwkv7_prep",
    )(P, P, P, P, mu_rkv, mu_lo, row(w0), w2p, row(a0), a2p, g2p, row(k_k), row(k_a), row(r_k))


RWKV_CHUNK = 64
RWKV_CHUNK_TT = 512
RWKV_PAIRS_PER_STEP = 4


def _rwkv_chunk_kernel(r_ref, lw_ref, k_ref, v_ref, kk_ref, b_ref, y_ref, h_ref):
    @pl.when(pl.program_id(2) == 0)
    def _():
        h_ref[...] = jnp.zeros_like(h_ref)

    C = RWKV_CHUNK
    TT = RWKV_CHUNK_TT
    N = RWKV_HEAD_DIM
    R2 = 2 * C
    bdot = lambda x, y: jnp.dot(x.astype(BF16), y.astype(BF16), preferred_element_type=F32)
    r, lw, k, v, kk, b = (ref[0] for ref in (r_ref, lw_ref, k_ref, v_ref, kk_ref, b_ref))
    i = lax.broadcasted_iota(jnp.int32, (TT, TT), 0)
    j = lax.broadcasted_iota(jnp.int32, (TT, TT), 1)
    tri = jnp.logical_and(i // C == j // C, j <= i).astype(F32)
    cum = jnp.dot(tri, lw, precision=HIGHEST, preferred_element_type=F32)
    lane_lo = lax.broadcasted_iota(jnp.int32, (C, LANES), 1) < N

    def stack(x):
        return jnp.concatenate([jnp.where(lane_lo, x, 0.0), jnp.where(lane_lo, 0.0, x)], axis=0)

    ti = lax.broadcasted_iota(jnp.int32, (R2, R2), 0)
    si = lax.broadcasted_iota(jnp.int32, (R2, R2), 1)
    strict = ti > si
    incl = ti >= si
    eye = (ti == si).astype(F32)
    blk16 = ti // 16 == si // 16
    off32 = jnp.logical_and(ti // 32 == si // 32, jnp.logical_not(blk16))
    off64 = jnp.logical_and(ti // 64 == si // 64, ti // 32 != si // 32)
    ones_c = jnp.ones((C, LANES), F32)

    items = [(c, p) for c in range(TT // C) for p in range(RWKV_PAIRS_PER_STEP)]
    each = lambda f, *lists: [f(*args) for args in zip(*lists)]
    lanes_of = lambda p: slice(p * LANES, (p + 1) * LANES)
    rows = [(slice(c * C, (c + 1) * C), lanes_of(p)) for c, p in items]
    cu = [cum[rw] for rw in rows]
    cu_last = [cum[(c + 1) * C - 1:(c + 1) * C, lanes_of(p)] for c, p in items]
    e_neg = each(lambda x: jnp.exp(-x), cu)
    e_tail = each(lambda x, xl: jnp.exp(xl - x), cu, cu_last)
    a_t = each(lambda rw, x: stack(-kk[rw] * jnp.exp(x - lw[rw])), rows, cu)
    r_t = each(lambda rw, x: stack(r[rw] * jnp.exp(x)), rows, cu)
    b_t = each(lambda rw, e: stack(b[rw] * e), rows, e_neg)
    k_t = each(lambda rw, e: stack(k[rw] * e), rows, e_neg)
    v_s = each(lambda rw: stack(v[rw]), rows)
    gram = each(lambda a_, r_, b_, k_: lax.dot_general(
        jnp.concatenate([a_, r_], 0).astype(BF16), jnp.concatenate([b_, k_], 0).astype(BF16),
        (((1,), (1,)), ((), ())), preferred_element_type=F32), a_t, r_t, b_t, k_t)
    n_ab = each(lambda g: jnp.where(strict, g[:R2, :R2], 0.0), gram)
    a_ak = each(lambda g: jnp.where(strict, g[:R2, R2:], 0.0), gram)
    a_rb = each(lambda g: jnp.where(incl, g[R2:, :R2], 0.0), gram)
    a_rk = each(lambda g: jnp.where(incl, g[R2:, R2:], 0.0), gram)
    av = each(lambda ak, rk, vv: bdot(jnp.concatenate([ak, rk], 0), vv), a_ak, a_rk, v_s)
    n1 = each(lambda n: jnp.where(blk16, n, 0.0), n_ab)
    n2 = each(lambda x: bdot(x, x), n1)
    n4 = each(lambda x: bdot(x, x), n2)
    n8 = each(lambda x: bdot(x, x), n4)
    xa = each(lambda x1, x2: x1 + x2 + bdot(x1, x2), n1, n2)
    xb = each(lambda x4, x8: x4 + x8 + bdot(x4, x8), n4, n8)
    t_inv = each(lambda p, q: eye + p + q + bdot(p, q), xa, xb)
    for off in (off32, off64):
        mid = each(lambda n, t: bdot(jnp.where(off, n, 0.0), t), n_ab, t_inv)
        t_inv = each(lambda t, m_: t + bdot(t, m_), t_inv, mid)
    tw = each(lambda t, a_, av_: bdot(t, jnp.concatenate([a_, av_[:R2]], axis=1)), t_inv, a_t, av)
    bk_t = each(lambda rw, e: jnp.transpose(jnp.concatenate([stack(b[rw] * e), stack(k[rw] * e)], 0)),
                rows, e_tail)
    pc_col = each(lambda rw: jnp.exp(jnp.dot(jnp.transpose(lw[rw]), ones_c, precision=HIGHEST,
                                             preferred_element_type=F32)), rows)
    w1r = each(lambda t, r_: jnp.concatenate([t[:, :LANES], r_], 0), tw, r_t)

    h = [h_ref[p] for p in range(RWKV_PAIRS_PER_STEP)]
    for it, (c, p) in enumerate(items):
        x = bdot(w1r[it], h[p])
        u = x[:R2] + tw[it][:, LANES:]
        y_bd = x[R2:] + bdot(a_rb[it], u) + av[it][R2:]
        y_ref[(0,) + rows[it]] = y_bd[:C] + y_bd[C:]
        h[p] = pc_col[it] * h[p] + bdot(bk_t[it], jnp.concatenate([u, v_s[it]], 0))
    for p in range(RWKV_PAIRS_PER_STEP):
        h_ref[p] = h[p]


def rwkv_chunked(r, lw, k, v, kk, b):
    B, S, W = r.shape
    TT = RWKV_CHUNK_TT
    lanes = RWKV_PAIRS_PER_STEP * LANES
    spec = pl.BlockSpec((1, TT, lanes), lambda bb, p, i: (bb, i, p))
    return pl.pallas_call(
        _rwkv_chunk_kernel,
        out_shape=jax.ShapeDtypeStruct((B, S, W), F32),
        grid=(B, W // lanes, S // TT),
        in_specs=[spec] * 6,
        out_specs=spec,
        scratch_shapes=[pltpu.VMEM((RWKV_PAIRS_PER_STEP, LANES, LANES), F32)],
        compiler_params=_cparams(("parallel", "parallel", "arbitrary")),
        name="rwkv7_chunked",
    )(r, lw, k, v, kk, b)


def _rwkv_post_kernel(y_ref, gate_ref, bonus_ref, g_ref, b_ref, o_ref):
    y = y_ref[0]
    mean = _segment_sum(y, RWKV_HEAD_DIM) * (1.0 / RWKV_HEAD_DIM)
    d = y - mean
    var = _segment_sum(d * d, RWKV_HEAD_DIM) * (1.0 / RWKV_HEAD_DIM)
    yn = d * lax.rsqrt(var + RWKV_GN_EPS) * g_ref[...] + b_ref[...]
    o_ref[0] = (yn + bonus_ref[0]) * gate_ref[0]


def rwkv_post(y, gate, bonus, ln_g, ln_b):
    B, S, W = y.shape
    ts = 512
    spec = pl.BlockSpec((1, ts, W), lambda b, i: (b, i, 0))
    full = pl.BlockSpec((1, W), lambda b, i: (0, 0))
    return pl.pallas_call(
        _rwkv_post_kernel,
        out_shape=jax.ShapeDtypeStruct((B, S, W), F32),
        grid=(B, S // ts),
        in_specs=[spec, spec, spec, full, full],
        out_specs=spec,
        compiler_params=_cparams(("parallel", "parallel")),
        name="rwkv7_post",
    )(y, gate, bonus, ln_g.reshape(1, W), ln_b.reshape(1, W))


def _merge_kernel(h_ref, wg0_ref, wg1_ref, wg2_ref, wg3_ref, ya_ref, yb_ref, yc_ref, yd_ref,
                  wa_ref, wb_ref, wc_ref, wd_ref, o_ref):
    h = h_ref[...]
    acc = None
    branches = ((wg0_ref, ya_ref, wa_ref), (wg1_ref, yb_ref, wb_ref), (wg2_ref, yc_ref, wc_ref),
                (wg3_ref, yd_ref, wd_ref))
    for wg_ref, y_ref, w_ref in branches:
        gate = jax.nn.sigmoid(jnp.dot(h, wg_ref[...], preferred_element_type=F32))
        proj = jnp.dot(y_ref[...].astype(BF16), w_ref[...], preferred_element_type=F32)
        acc = gate * proj if acc is None else acc + gate * proj
    o_ref[...] = acc.astype(o_ref.dtype)


def branch_merge(hb, wg, ys, ws):
    T, D = hb.shape
    tm, tn = 512, 512
    nj = D // tn
    gspecs = [pl.BlockSpec((D, tn), functools.partial(lambda j, i, br: (0, br * nj + j), br=br))
              for br in range(N_BRANCHES)]
    yspecs = [pl.BlockSpec((tm, y.shape[1]), lambda j, i: (i, 0)) for y in ys]
    wspecs = [pl.BlockSpec((w.shape[0], tn), lambda j, i: (0, j)) for w in ws]
    return pl.pallas_call(
        _merge_kernel,
        out_shape=jax.ShapeDtypeStruct((T, D), BF16),
        grid=(nj, T // tm),
        in_specs=[pl.BlockSpec((tm, D), lambda j, i: (i, 0))] + gspecs + yspecs + wspecs,
        out_specs=pl.BlockSpec((tm, tn), lambda j, i: (i, j)),
        compiler_params=_cparams(("parallel", "parallel")),
        name="branch_merge",
    )(hb, wg, wg, wg, wg, *ys, *ws)


ROUTE_TM = 256
PIECE = 16
TILE_USED_ROWS = ROUTE_TM * TOP_K + N_EXPERTS * (PIECE - 1)
TILE_ROWS = 1536
BLOCK_PIECES = MOE_BM // PIECE
DISPATCH_ROWS = 512


def _layer_norm(z, g, b):
    mu = jnp.mean(z, -1, keepdims=True)
    d = z - mu
    var = jnp.mean(d * d, -1, keepdims=True)
    return d * lax.rsqrt(var + LN_EPS) * g + b


def _out_ln_router_kernel(m_ref, wo_ref, x_ref, gate_ref, g_ref, b_ref, sc_ref, sh_ref, rwh_ref, rwl_ref, rb_ref,
                          x1_ref, xbt_ref, wt_ref, lrow_ref, cnt_ref):
    y = jnp.dot(m_ref[...], wo_ref[...], preferred_element_type=F32)
    x1 = _layer_norm(DEEPNORM_ALPHA * x_ref[...] + gate_ref[0] * y, g_ref[...], b_ref[...])
    x1_ref[...] = x1
    h2 = x1 * (1.0 + sc_ref[0]) + sh_ref[0]
    h2_hi, h2_lo = _split_bf16(h2)
    logits = (jnp.dot(h2_hi, rwh_ref[...], preferred_element_type=F32)
              + jnp.dot(h2_hi, rwl_ref[...], preferred_element_type=F32)
              + jnp.dot(h2_lo, rwh_ref[...], preferred_element_type=F32)) + rb_ref[...]
    lane = lax.broadcasted_iota(jnp.int32, logits.shape, 1).astype(F32)
    vals, idxs = [], []
    cur = logits
    for _ in range(TOP_K):
        m = jnp.max(cur, -1, keepdims=True)
        ix = jnp.min(jnp.where(cur == m, lane, float(LANES)), -1, keepdims=True)
        vals.append(m)
        idxs.append(ix)
        cur = jnp.where(lane == ix, -jnp.inf, cur)
    es = [jnp.exp(v - vals[0]) for v in vals]
    tot = es[0] + es[1] + es[2] + es[3]
    sel = jnp.zeros_like(logits)
    for kq in range(TOP_K):
        sel = jnp.where(lane == idxs[kq], 1.0, sel)
    tm = logits.shape[0]
    ti = lax.broadcasted_iota(jnp.int32, (tm, tm), 0)
    si = lax.broadcasted_iota(jnp.int32, (tm, tm), 1)
    before = jnp.dot((si < ti).astype(BF16), sel.astype(BF16), preferred_element_type=F32)
    cnt = jnp.sum(sel, axis=0, keepdims=True)
    cnt_ref[0] = jnp.broadcast_to(cnt, (SUBLANES, LANES)).astype(jnp.int32)
    pieces = jnp.floor((cnt + (PIECE - 1.0)) * (1.0 / PIECE))
    ei = lax.broadcasted_iota(jnp.int32, (LANES, LANES), 0)
    ej = lax.broadcasted_iota(jnp.int32, (LANES, LANES), 1)
    group_row = PIECE * jnp.dot(jnp.broadcast_to(pieces, (SUBLANES, LANES)).astype(BF16), (ei < ej).astype(BF16),
                                preferred_element_type=F32)[0:1]
    pos = before + group_row
    wt_o = jnp.zeros_like(logits)
    lrow_o = jnp.full(logits.shape, -1.0, F32)
    for kq in range(TOP_K):
        lrow_k = jnp.sum(jnp.where(lane == idxs[kq], pos, 0.0), -1, keepdims=True)
        wt_o = jnp.where(lane == kq, es[kq] / tot, wt_o)
        lrow_o = jnp.where(lane == kq, lrow_k, lrow_o)
    wt_ref[...] = wt_o
    lrow_ref[...] = lrow_o.astype(jnp.int32)
    lrow_t = jnp.transpose(lrow_o).astype(jnp.int32)
    for rc in range(TILE_ROWS // DISPATCH_ROWS):
        row = lax.broadcasted_iota(jnp.int32, (DISPATCH_ROWS, tm), 0) + rc * DISPATCH_ROWS
        hit = row == lrow_t[0:1, :]
        for kq in range(1, TOP_K):
            hit = jnp.logical_or(hit, row == lrow_t[kq:kq + 1, :])
        onehot = jnp.where(hit, 1.0, 0.0).astype(BF16)
        xbt_ref[pl.ds(rc * DISPATCH_ROWS, DISPATCH_ROWS), :] = jnp.dot(
            onehot, h2_hi, preferred_element_type=F32).astype(xbt_ref.dtype)


def out_ln_router(merged, w_out_b, x2d, gate1, ln_g, ln_b, scale2, shift2, router_w, router_b, S):
    T, D = x2d.shape
    tm = ROUTE_TM
    nt = T // tm
    spb = S // tm
    rw = jnp.zeros((D, LANES), F32).at[:, :N_EXPERTS].set(router_w)
    rw_hi, rw_lo = _split_bf16(rw)
    rb = jnp.full((1, LANES), NEG_BIG, F32).at[0, :N_EXPERTS].set(router_b)
    rowblk = lambda w: pl.BlockSpec((tm, w), lambda i: (i, 0))
    full = lambda r, c: pl.BlockSpec((r, c), lambda i: (0, 0))
    perb = pl.BlockSpec((1, 1, D), lambda i: (i // spb, 0, 0))
    tile_spec = pl.BlockSpec((1, SUBLANES, LANES), lambda i: (i, 0, 0))
    tile_shape = jax.ShapeDtypeStruct((nt, SUBLANES, LANES), jnp.int32)
    return pl.pallas_call(
        _out_ln_router_kernel,
        out_shape=(jax.ShapeDtypeStruct((T, D), F32), jax.ShapeDtypeStruct((nt * TILE_ROWS, D), BF16),
                   jax.ShapeDtypeStruct((T, LANES), F32), jax.ShapeDtypeStruct((T, LANES), jnp.int32), tile_shape),
        grid=(nt,),
        in_specs=[rowblk(D), full(D, D), rowblk(D), perb, full(1, D), full(1, D), perb, perb,
                  full(D, LANES), full(D, LANES), full(1, LANES)],
        out_specs=(rowblk(D), pl.BlockSpec((TILE_ROWS, D), lambda i: (i, 0)), rowblk(LANES), rowblk(LANES),
                   tile_spec),
        compiler_params=_cparams(("parallel",)),
        name="out_proj_ln_router",
    )(merged, w_out_b, x2d, gate1, ln_g.reshape(1, D), ln_b.reshape(1, D), scale2, shift2, rw_hi, rw_lo, rb)


W_CHUNK_ROWS = 512
W_STAGES = 3
W_NONE, W_FIRST_HALF, W_SECOND_HALF, W_ALL = 0, 1, 2, 3
GU_CHUNKS = D_MODEL // W_CHUNK_ROWS
DN_CHUNKS = EXPERT_FF // W_CHUNK_ROWS


def _expert_kernel(be_ref, nu_ref, src_ref, dst_ref, par_ref, nxt_ref, mode_ref, x_hbm, wgu_hbm, bgu_ref, wd_hbm, bd_ref,
                   y_hbm, xs_ref, ys_ref, wgu_b, wd_b, stg_ref, isem, osem, wsem, *, layer):
    i = pl.program_id(0)
    nu = nu_ref[0]
    slot = i % 2
    n_chunks = GU_CHUNKS + DN_CHUNKS

    def in_copy(blk, k, s):
        src = pl.multiple_of(src_ref[blk * BLOCK_PIECES + k], PIECE)
        return pltpu.make_async_copy(x_hbm.at[pl.ds(src, PIECE), :], xs_ref.at[s, pl.ds(k * PIECE, PIECE), :],
                                     isem.at[s])

    def out_copy(blk, k, s):
        dst = pl.multiple_of(dst_ref[blk * BLOCK_PIECES + k], PIECE)
        return pltpu.make_async_copy(ys_ref.at[s, pl.ds(k * PIECE, PIECE), :], y_hbm.at[pl.ds(dst, PIECE), :],
                                     osem.at[s])

    def all_pieces(fn):
        for k in range(BLOCK_PIECES):
            fn(k)

    def w_copy(e, c):
        s = c % W_STAGES
        if c < GU_CHUNKS:
            src = wgu_hbm.at[layer, e, pl.ds(c * W_CHUNK_ROWS, W_CHUNK_ROWS), :]
        else:
            src = wd_hbm.at[layer, e, pl.ds((c - GU_CHUNKS) * W_CHUNK_ROWS, W_CHUNK_ROWS), :]
        return pltpu.make_async_copy(src, stg_ref.at[s], wsem.at[s])

    def w_convert(c, p):
        w = stg_ref[c % W_STAGES].astype(BF16)
        if c < GU_CHUNKS:
            wgu_b[p, pl.ds(c * W_CHUNK_ROWS, W_CHUNK_ROWS), :] = w
        else:
            wd_b[p, pl.ds((c - GU_CHUNKS) * W_CHUNK_ROWS, W_CHUNK_ROWS), :] = w

    def w_finish(e, p, chunks, refill):
        for c in chunks:
            w_copy(e, c).wait()
            w_convert(c, p)
            if refill and c + W_STAGES < n_chunks:
                w_copy(e, c + W_STAGES).start()

    @pl.when(i == 0)
    def _():
        all_pieces(lambda k: in_copy(0, k, 0).start())
        e0 = be_ref[0]
        for c in range(W_STAGES):
            w_copy(e0, c).start()
        w_finish(e0, par_ref[0], range(n_chunks), True)

    @pl.when(i + 1 < nu)
    def _():
        all_pieces(lambda k: in_copy(i + 1, k, 1 - slot).start())

    @pl.when(i < nu)
    def _():
        p = par_ref[i]
        e_next = nxt_ref[i]
        mode = mode_ref[i]
        first_half = range(W_STAGES)
        second_half = range(W_STAGES, n_chunks)

        @pl.when(jnp.logical_or(mode == W_FIRST_HALF, mode == W_ALL))
        def _():
            for c in first_half:
                w_copy(e_next, c).start()

        @pl.when(mode == W_SECOND_HALF)
        def _():
            for c in second_half:
                w_copy(e_next, c).start()

        all_pieces(lambda k: in_copy(i, k, slot).wait())
        gu = jnp.dot(xs_ref[slot], wgu_b[p], preferred_element_type=F32) + bgu_ref[0]

        @pl.when(mode == W_ALL)
        def _():
            w_finish(e_next, 1 - p, first_half, True)

        glu = jnp.minimum(gu[:, :EXPERT_FF], SWIGLU_LIMIT)
        lin = jnp.clip(gu[:, EXPERT_FF:], -SWIGLU_LIMIT, SWIGLU_LIMIT)
        act = glu * jax.nn.sigmoid(SWIGLU_ALPHA * glu) * (lin + 1.0)

        @pl.when(i >= 2)
        def _():
            all_pieces(lambda k: out_copy(i - 2, k, slot).wait())

        y = jnp.dot(act.astype(BF16), wd_b[p], preferred_element_type=F32) + bd_ref[0]
        ys_ref[slot] = y.astype(ys_ref.dtype)
        all_pieces(lambda k: out_copy(i, k, slot).start())

        @pl.when(mode == W_FIRST_HALF)
        def _():
            w_finish(e_next, 1 - p, first_half, False)

        @pl.when(jnp.logical_or(mode == W_SECOND_HALF, mode == W_ALL))
        def _():
            w_finish(e_next, 1 - p, second_half, False)

    @pl.when(i == nu - 1)
    def _():
        @pl.when(i >= 1)
        def _():
            all_pieces(lambda k: out_copy(i - 1, k, 1 - slot).wait())
        all_pieces(lambda k: out_copy(i, k, slot).wait())


def expert_ffn(block_e, n_used, piece_src, piece_dst, w_parity, next_e, w_mode, xbt, wgu, bgu, wd, bd, layer):
    rows, D = xbt.shape
    _, E, _, F2 = wgu.shape
    nblk = block_e.shape[0]
    bias = lambda w: pl.BlockSpec((1, 1, w), lambda i, be, *_: (be[i], 0, 0))
    grid_spec = pltpu.PrefetchScalarGridSpec(
        num_scalar_prefetch=7,
        grid=(nblk,),
        in_specs=[pl.BlockSpec(memory_space=pl.ANY), pl.BlockSpec(memory_space=pl.ANY), bias(F2),
                  pl.BlockSpec(memory_space=pl.ANY), bias(D)],
        out_specs=pl.BlockSpec(memory_space=pl.ANY),
        scratch_shapes=[pltpu.VMEM((2, MOE_BM, D), BF16), pltpu.VMEM((2, MOE_BM, D), BF16),
                        pltpu.VMEM((2, D, F2), BF16), pltpu.VMEM((2, F2 // 2, D), BF16),
                        pltpu.VMEM((W_STAGES, W_CHUNK_ROWS, D), F32),
                        pltpu.SemaphoreType.DMA((2,)), pltpu.SemaphoreType.DMA((2,)),
                        pltpu.SemaphoreType.DMA((W_STAGES,))],
    )
    return pl.pallas_call(
        functools.partial(_expert_kernel, layer=layer),
        out_shape=jax.ShapeDtypeStruct((rows, D), BF16),
        grid_spec=grid_spec,
        input_output_aliases={7: 0},
        compiler_params=_cparams(("arbitrary",)),
        name="expert_ffn",
    )(block_e, n_used, piece_src, piece_dst, w_parity, next_e, w_mode, xbt, wgu, bgu.reshape(E, 1, F2), wd,
      bd.reshape(E, 1, D))


COMBINE_COLS = 512


def _combine_kernel(meta_ref, wt_ref, x1_ref, gate_ref, g_ref, b_ref, sc_ref, sh_ref, y_ref,
                    x2_ref, hb_ref, acc_ref):
    meta = meta_ref[...]
    wts = wt_ref[...]
    CW = COMBINE_COLS
    col0 = lax.broadcasted_iota(jnp.int32, (meta.shape[0], CW), 1)
    for cc in range(TILE_ROWS // CW):
        hi = jnp.zeros(col0.shape, F32)
        lo = jnp.zeros(col0.shape, F32)
        for kq in range(TOP_K):
            hit = col0 == meta[:, kq:kq + 1] - cc * CW
            w = wts[:, kq:kq + 1]
            w_hi = w.astype(BF16).astype(F32)
            hi = jnp.where(hit, w_hi, hi)
            lo = jnp.where(hit, w - w_hi, lo)
        rows = y_ref[pl.ds(cc * CW, CW), :]
        part = (jnp.dot(hi.astype(BF16), rows, preferred_element_type=F32)
                + jnp.dot(lo.astype(BF16), rows, preferred_element_type=F32))
        if cc == 0:
            acc_ref[...] = part
        else:
            acc_ref[...] += part

    x2 = _layer_norm(DEEPNORM_ALPHA * x1_ref[...] + gate_ref[0] * acc_ref[...], g_ref[...], b_ref[...])
    x2_ref[...] = x2
    hb_ref[...] = (x2 * (1.0 + sc_ref[0]) + sh_ref[0]).astype(hb_ref.dtype)


def moe_combine_ln(ybt, meta, wt, x1, gate2, ln_g, ln_b, scale_next, shift_next, S):
    T, D = x1.shape
    tm = ROUTE_TM
    spb = S // tm
    rowblk = lambda w: pl.BlockSpec((tm, w), lambda i: (i, 0))
    full = pl.BlockSpec((1, D), lambda i: (0, 0))
    perb = pl.BlockSpec((1, 1, D), lambda i: (i // spb, 0, 0))
    return pl.pallas_call(
        _combine_kernel,
        out_shape=(jax.ShapeDtypeStruct((T, D), F32), jax.ShapeDtypeStruct((T, D), BF16)),
        grid=(T // tm,),
        in_specs=[rowblk(LANES), rowblk(LANES), rowblk(D), perb, full, full, perb, perb,
                  pl.BlockSpec((TILE_ROWS, D), lambda i: (i, 0))],
        out_specs=(rowblk(D), rowblk(D)),
        scratch_shapes=[pltpu.VMEM((tm, D), F32)],
        compiler_params=_cparams(("parallel",)),
        name="moe_combine_ln",
    )(meta, wt, x1, gate2, ln_g.reshape(1, D), ln_b.reshape(1, D), scale_next, shift_next, ybt)


def routed_ffn_ln(xbt, lrow, wt, tile_cnt, wgu, bgu, wd, bd, layer, x1, gate2, ln_g, ln_b,
                  scale_next, shift_next, S):
    T, D = x1.shape
    E = N_EXPERTS
    nt = T // ROUTE_TM
    i32 = jnp.int32
    cnt = tile_cnt[:, 0, :E]
    npc = (cnt + PIECE - 1) // PIECE
    seg_off = (jnp.cumsum(npc, axis=1) - npc) * PIECE
    per_e = jnp.sum(npc, axis=0)
    per_e_pad = ((per_e + BLOCK_PIECES - 1) // BLOCK_PIECES) * BLOCK_PIECES
    e_end = jnp.cumsum(per_e_pad)
    e_start = e_end - per_e_pad
    t_end = jnp.cumsum(npc, axis=0).T
    t_start = t_end - npc.T
    max_pieces = nt * (TILE_USED_ROWS // PIECE) + E * (BLOCK_PIECES - 1)
    nblk = -(-max_pieces // BLOCK_PIECES)
    q = jnp.arange(nblk * BLOCK_PIECES, dtype=i32)
    e_q = jnp.minimum(jnp.sum((e_end[None, :] <= q[:, None]).astype(i32), axis=1), E - 1)
    sel_e = e_q[:, None] == jnp.arange(E, dtype=i32)[None, :]
    of_expert = lambda table: jnp.sum(jnp.where(sel_e, table[None, :], 0), axis=1)
    w_q = q - of_expert(e_start)
    real = w_q < of_expert(per_e)
    rows_of = lambda table: jnp.sum(jnp.where(sel_e[:, :, None], table[None], 0), axis=1)
    t_q = jnp.minimum(jnp.sum((rows_of(t_end) <= w_q[:, None]).astype(i32), axis=1), nt - 1)
    sel_t = t_q[:, None] == jnp.arange(nt, dtype=i32)[None, :]
    of_tile = lambda table: jnp.sum(jnp.where(sel_t, rows_of(table), 0), axis=1)
    k_q = w_q - of_tile(t_start)
    row_q = t_q * TILE_ROWS + of_tile(seg_off.T) + k_q * PIECE
    tail_pieces = (TILE_ROWS - TILE_USED_ROWS) // PIECE
    assert 2 * BLOCK_PIECES <= nt * tail_pieces, "not enough tile tail rows for the padding pieces"
    scratch_id = ((q // BLOCK_PIECES) % 2) * BLOCK_PIECES + q % BLOCK_PIECES
    scratch_q = (scratch_id // tail_pieces) * TILE_ROWS + TILE_USED_ROWS + (scratch_id % tail_pieces) * PIECE
    piece_src = jnp.where(real, row_q, 0).astype(i32)
    piece_dst = jnp.where(real, row_q, scratch_q).astype(i32)
    block_e = e_q[::BLOCK_PIECES]
    n_used = (e_end[-1] // BLOCK_PIECES).astype(i32)
    blk = jnp.arange(nblk, dtype=i32)
    nxt = jnp.concatenate([block_e[1:], block_e[-1:]])
    switch_next = jnp.logical_and(nxt != block_e, blk + 1 < n_used)
    w_parity = (jnp.cumsum(switch_next.astype(i32)) - switch_next.astype(i32)) % 2
    prv = jnp.concatenate([block_e[:1] - 1, block_e[:-1]])
    only_block = jnp.logical_and(switch_next, prv != block_e)
    before_last = jnp.logical_and(jnp.concatenate([switch_next[1:], switch_next[-1:] & False]), nxt == block_e)
    w_mode = jnp.where(only_block, W_ALL, jnp.where(switch_next, W_SECOND_HALF,
                                                    jnp.where(before_last, W_FIRST_HALF, W_NONE))).astype(i32)
    nxt2 = jnp.concatenate([block_e[2:], block_e[-1:], block_e[-1:]])
    next_e = jnp.where(switch_next, nxt, jnp.where(before_last, nxt2, 0)).astype(i32)

    ybt = expert_ffn(block_e, n_used.reshape(1), piece_src, piece_dst, w_parity, next_e, w_mode, xbt, wgu, bgu, wd,
                     bd, layer)
    return moe_combine_ln(ybt, lrow, wt, x1, gate2, ln_g, ln_b, scale_next, shift_next, S)


def _mix_weights(w_in_l):
    def cols(rng, pad=0):
        part = w_in_l[:, rng[0]:rng[1]]
        if pad:
            part = jnp.pad(part, ((0, 0), (0, pad)))
        return part
    w_mix = jnp.concatenate([
        cols(SRC_POOL), cols(SRC_GLA_V), cols(SRC_GLA_G), cols(SRC_RWKV_RKV), cols(SRC_ATT),
        cols(SRC_GLA_Q), cols(SRC_GLA_K),
        cols(SRC_RWKV_LO, RWKV_LO_PAD - (SRC_RWKV_LO[1] - SRC_RWKV_LO[0])),
        cols(SRC_GLA_A, GLA_A_PAD - (SRC_GLA_A[1] - SRC_GLA_A[0]))], axis=1).astype(BF16)
    w_gates = w_in_l[:, SRC_GATES[0]:].astype(BF16)
    return w_mix, w_gates


def kernel(x, c, ada_w, ada_b, w_in, pool_w, pool_scale, gla_w_alpha, gla_b_alpha, gla_norm_g, rwkv_mu, rwkv_w0, rwkv_w2, rwkv_a0, rwkv_a2, rwkv_g2, rwkv_k_k, rwkv_k_a, rwkv_r_k, rwkv_ln_g, rwkv_ln_b, w_branch_a, w_branch_b, w_branch_c, w_branch_d, w_out, ln1_g, ln1_b, router_w, router_b, w_gate_up, b_gate_up, w_down, b_down, ln2_g, ln2_b):
    B, S, D = x.shape
    T = B * S
    mod = ada_modulation(c, ada_w, ada_b)
    mods = [[mod[l, :, None, i * D:(i + 1) * D] for i in range(6)] for l in range(DEPTH)]
    x2d = x.reshape(T, D)
    hb2d = modulate(x, mods[0][1], mods[0][0]).reshape(T, D)
    for l in range(DEPTH):
        shift1, scale1, gate1, shift2, scale2, gate2 = mods[l]
        shift_next, scale_next = (mods[l + 1][0], mods[l + 1][1]) if l + 1 < DEPTH else (shift1, scale1)
        w_mix, w_gates = _mix_weights(w_in[l])
        P = matmul(hb2d, w_mix, 1024, MIX_TN).reshape(B, S, MIX_WIDTH)
        y_a = pool_mixer(P, pool_w[l], pool_scale[l])
        y_b = dilated_attention(P)
        y_c = gla_mixer(P, gla_w_alpha[l], gla_b_alpha[l], gla_norm_g[l])
        r_, w_, k_, v_, kk_, b_, gate_, bonus_ = rwkv_prep(
            P, rwkv_mu[l], rwkv_w0[l], rwkv_w2[l], rwkv_a0[l], rwkv_a2[l], rwkv_g2[l],
            rwkv_k_k[l], rwkv_k_a[l], rwkv_r_k[l].reshape(-1))
        y_t = rwkv_chunked(r_, w_, k_, v_, kk_, b_)
        y_d = rwkv_post(y_t, gate_, bonus_, rwkv_ln_g[l], rwkv_ln_b[l])
        ys = [y.reshape(T, -1) for y in (y_a, y_b, y_c, y_d)]
        ws = [w[l].astype(BF16) for w in (w_branch_a, w_branch_b, w_branch_c, w_branch_d)]
        merged = branch_merge(hb2d, w_gates, ys, ws)
        x1, xbt, wt, lrow, tile_cnt = out_ln_router(
            merged, w_out[l].astype(BF16), x2d, gate1, ln1_g[l], ln1_b[l], scale2, shift2,
            router_w[l], router_b[l], S)
        x2d, hb2d = routed_ffn_ln(xbt, lrow, wt, tile_cnt, w_gate_up, b_gate_up[l], w_down, b_down[l], l,
                                  x1, gate2, ln2_g[l], ln2_b[l], scale_next, shift_next, S)
    return x2d.reshape(B, S, D)
```

```python
import functools

import numpy as np
import jax
import jax.numpy as jnp
from jax import lax
from jax.experimental import pallas as pl
from jax.experimental.pallas import tpu as pltpu

F32 = jnp.float32
BF16 = jnp.bfloat16
HIGHEST = lax.Precision.HIGHEST

D_MODEL = 2048
DEPTH = 2
POOL_WINDOWS = (2, 4, 8, 16)
POOL_GROUP = 128
POOL_WIDTH = 512
ATT_GROUPS = ((128, 1), (512, 4), (2048, 16))
ATT_HEAD_DIM = 64
ATT_HEADS = 12
ATT_WIDTH = 768
ATT_OUT_WIDTH = 256
ATT_BLOCK = 128
ALIBI_SLOPES = tuple(2.0 ** (-8.0 * (h + 1) / ATT_HEADS) for h in range(ATT_HEADS))
GLA_HEADS = 4
GLA_DK = 64
GLA_DV = 128
GLA_KEY_WIDTH = 256
GLA_VAL_WIDTH = 512
GLA_GATE_RANK = 16
GLA_GATE_TEMP = 16.0
GLA_CHUNK = 32
GLA_NORM_EPS = 1e-6
RWKV_HEADS = 8
RWKV_HEAD_DIM = 64
RWKV_WIDTH = 512
RWKV_DECAY_RANK = 32
RWKV_AAA_RANK = 32
RWKV_GATE_RANK = 96
RWKV_GN_EPS = 64e-5
N_BRANCHES = 4
N_EXPERTS = 32
TOP_K = 4
EXPERT_FF = 1024
SWIGLU_LIMIT = 7.0
SWIGLU_ALPHA = 1.702
LN_EPS = 1e-5
DEEPNORM_ALPHA = (2 * DEPTH) ** 0.25

LANES = 128
SUBLANES = 8
VMEM_LIMIT = 56 * 1024 * 1024

SRC_POOL = (0, 512)
SRC_ATT = (512, 2816)
SRC_GLA_Q = (2816, 3072)
SRC_GLA_K = (3072, 3328)
SRC_GLA_V = (3328, 3840)
SRC_GLA_G = (3840, 4352)
SRC_GLA_A = (4352, 4368)
SRC_RWKV_RKV = (4368, 5904)
SRC_RWKV_LO = (5904, 6064)
SRC_GATES = (6064, 14256)
OFF_POOL = 0
OFF_GLA_V = 512
OFF_GLA_G = 1024
OFF_RWKV_RKV = 1536
OFF_ATT_Q = 3072
OFF_ATT_K = OFF_ATT_Q + ATT_WIDTH
OFF_ATT_V = OFF_ATT_K + ATT_WIDTH
OFF_GLA_Q = 5376
OFF_GLA_K = 5632
OFF_RWKV_LO = 5888
RWKV_LO_PAD = 256
OFF_GLA_A = 6144
GLA_A_PAD = 128
MIX_WIDTH = 6272
MIX_TN = 896

MOE_BM = 256
NEG_BIG = -1e30


def _cparams(sem):
    return pltpu.CompilerParams(dimension_semantics=sem, vmem_limit_bytes=VMEM_LIMIT)


def _split_bf16(x):
    hi = x.astype(BF16)
    return hi, (x - hi.astype(F32)).astype(BF16)


def _dot3(x, w):
    x_hi, x_lo = _split_bf16(x)
    w_hi, w_lo = _split_bf16(w)
    return (jnp.dot(x_hi, w_hi, preferred_element_type=F32) + jnp.dot(x_hi, w_lo, preferred_element_type=F32)
            + jnp.dot(x_lo, w_hi, preferred_element_type=F32))


def _dot_exact_lhs(m, x):
    hi = x.astype(BF16)
    r1 = x - hi.astype(F32)
    mid = r1.astype(BF16)
    lo = (r1 - mid.astype(F32)).astype(BF16)
    mb = m.astype(BF16)
    return (jnp.dot(mb, hi, preferred_element_type=F32) + jnp.dot(mb, mid, preferred_element_type=F32)
            + jnp.dot(mb, lo, preferred_element_type=F32))


def _ada_kernel(c_ref, w_ref, b_ref, o_ref):
    c = c_ref[...]
    s = c * jax.nn.sigmoid(c)
    o_ref[0] = _dot3(s, w_ref[0]) + b_ref[0]


def ada_modulation(c, ada_w, ada_b):
    L, D, N = ada_w.shape
    B = c.shape[0]
    cp = jnp.zeros((SUBLANES, D), F32).at[:B].set(c)
    tn = 1024
    out = pl.pallas_call(
        _ada_kernel,
        out_shape=jax.ShapeDtypeStruct((L, SUBLANES, N), F32),
        grid=(L, N // tn),
        in_specs=[pl.BlockSpec((SUBLANES, D), lambda l, j: (0, 0)),
                  pl.BlockSpec((1, D, tn), lambda l, j: (l, 0, j)),
                  pl.BlockSpec((1, 1, tn), lambda l, j: (l, 0, j))],
        out_specs=pl.BlockSpec((1, SUBLANES, tn), lambda l, j: (l, 0, j)),
        compiler_params=_cparams(("parallel", "parallel")),
        name="ada_modulation",
    )(cp, ada_w, ada_b.reshape(L, 1, N))
    return out[:, :B]


def _modulate_kernel(x_ref, sc_ref, sh_ref, o_ref):
    o_ref[0] = (x_ref[0] * (1.0 + sc_ref[0]) + sh_ref[0]).astype(o_ref.dtype)


def modulate(x, scale, shift):
    B, S, D = x.shape
    ts = 1024
    return pl.pallas_call(
        _modulate_kernel,
        out_shape=jax.ShapeDtypeStruct((B, S, D), BF16),
        grid=(B, S // ts),
        in_specs=[pl.BlockSpec((1, ts, D), lambda b, i: (b, i, 0)),
                  pl.BlockSpec((1, 1, D), lambda b, i: (b, 0, 0)),
                  pl.BlockSpec((1, 1, D), lambda b, i: (b, 0, 0))],
        out_specs=pl.BlockSpec((1, ts, D), lambda b, i: (b, i, 0)),
        compiler_params=_cparams(("parallel", "parallel")),
        name="modulate",
    )(x, scale, shift)


def _mm_kernel(x_ref, w_ref, o_ref):
    o_ref[...] = jnp.dot(x_ref[...], w_ref[...], preferred_element_type=F32).astype(o_ref.dtype)


def matmul(x, w, tm, tn, out_dtype=F32):
    M, K = x.shape
    N = w.shape[1]
    return pl.pallas_call(
        _mm_kernel,
        out_shape=jax.ShapeDtypeStruct((M, N), out_dtype),
        grid=(N // tn, M // tm),
        in_specs=[pl.BlockSpec((tm, K), lambda j, i: (i, 0)),
                  pl.BlockSpec((K, tn), lambda j, i: (0, j))],
        out_specs=pl.BlockSpec((tm, tn), lambda j, i: (i, j)),
        compiler_params=_cparams(("parallel", "parallel")),
        name="matmul",
    )(x, w)


def _pool_kernel(p_ref, w_ref, sc_ref, o_ref):
    g = pl.program_id(1)
    v = p_ref[0]
    S = v.shape[0]
    row = lax.broadcasted_iota(jnp.int32, v.shape, 0)
    win = jnp.left_shift(2, g)
    s = v
    pooled_sum = v
    for k, sh in enumerate((1, 2, 4, 8)):
        s = s + jnp.where(row >= sh, pltpu.roll(s, sh, 0), 0.0)
        pooled_sum = jnp.where(g >= k, s, pooled_sum)
    cnt = jnp.minimum(row + 1, win).astype(F32)
    diff = pooled_sum / cnt - v
    y = jnp.dot(diff.astype(BF16), w_ref[0].astype(BF16), preferred_element_type=F32)
    o_ref[0] = y * sc_ref[...]


def pool_mixer(P, pool_w, pool_scale):
    B, S, _ = P.shape
    G = len(POOL_WINDOWS)
    return pl.pallas_call(
        _pool_kernel,
        out_shape=jax.ShapeDtypeStruct((B, S, POOL_WIDTH), F32),
        grid=(B, G),
        in_specs=[pl.BlockSpec((1, S, POOL_GROUP), lambda b, g: (b, 0, OFF_POOL // POOL_GROUP + g)),
                  pl.BlockSpec((1, POOL_GROUP, POOL_GROUP), lambda b, g: (g, 0, 0)),
                  pl.BlockSpec((1, POOL_GROUP), lambda b, g: (0, g))],
        out_specs=pl.BlockSpec((1, S, POOL_GROUP), lambda b, g: (b, 0, g)),
        compiler_params=_cparams(("parallel", "parallel")),
        name="pool_mixer",
    )(P, pool_w, pool_scale.reshape(1, POOL_WIDTH))


ATT_GROUP_HEADS = 4
ATT_RESIDUE_UNROLL = 4


def _att_kernel(q_ref, kc_ref, kp_ref, vc_ref, vp_ref, num_ref, den_ref, m_ref, *, slopes, dilation, heads):
    if heads == ATT_GROUP_HEADS:
        n = pl.program_id(1)
    else:
        pair = pl.program_id(1)
        n = pl.program_id(2)
        slopes = [jnp.where(pair == 0, slopes[h], slopes[heads + h]) for h in range(heads)]
    i = lax.broadcasted_iota(jnp.int32, (ATT_BLOCK, ATT_BLOCK), 0)
    j = lax.broadcasted_iota(jnp.int32, (ATT_BLOCK, ATT_BLOCK), 1)
    dist_c = i - j
    dist_p = dist_c + ATT_BLOCK
    valid_c = dist_c >= 0
    valid_p = jnp.logical_and(dist_p <= ATT_BLOCK, n > 0)
    bias_c = [-(s * dilation) * dist_c.astype(F32) for s in slopes]
    bias_p = [-(s * dilation) * dist_p.astype(F32) for s in slopes]
    head_of_lane = lax.broadcasted_iota(jnp.int32, (ATT_BLOCK, heads * ATT_HEAD_DIM), 1) // ATT_HEAD_DIM
    dn = (((1,), (1,)), ((), ()))

    def per_head_lanes(cols):
        out = cols[-1]
        for h in range(heads - 2, -1, -1):
            out = jnp.where(head_of_lane == h, cols[h], out)
        return out

    def residue(r):
        rows = slice(None) if dilation == 1 else pl.ds(r, ATT_BLOCK, stride=dilation)
        q4, kc4, kp4, vc4, vp4 = (ref[0, rows, :] for ref in (q_ref, kc_ref, kp_ref, vc_ref, vp_ref))
        nums, dens, ms = [], [], []
        for h in range(heads):
            sl = slice(h * ATT_HEAD_DIM, (h + 1) * ATT_HEAD_DIM)
            q = (q4[:, sl] * (ATT_HEAD_DIM ** -0.5)).astype(BF16)
            s_c = lax.dot_general(q, kc4[:, sl].astype(BF16), dn, preferred_element_type=F32)
            s_p = lax.dot_general(q, kp4[:, sl].astype(BF16), dn, preferred_element_type=F32)
            s_c = jnp.where(valid_c, s_c + bias_c[h], NEG_BIG)
            s_p = jnp.where(valid_p, s_p + bias_p[h], NEG_BIG)
            m = jnp.maximum(jnp.max(s_c, -1, keepdims=True), jnp.max(s_p, -1, keepdims=True))
            p_c = jnp.exp(s_c - m)
            p_p = jnp.exp(s_p - m)
            dens.append(jnp.sum(p_c, -1, keepdims=True) + jnp.sum(p_p, -1, keepdims=True))
            nums.append(jnp.dot(p_c.astype(BF16), vc4[:, sl].astype(BF16), preferred_element_type=F32)
                        + jnp.dot(p_p.astype(BF16), vp4[:, sl].astype(BF16), preferred_element_type=F32))
            ms.append(m)
        num_ref[0, rows, :] = jnp.concatenate(nums, axis=-1)
        den_ref[0, rows, :] = per_head_lanes(dens)
        m_ref[0, rows, :] = per_head_lanes(ms)

    if dilation == 1:
        residue(0)
    else:
        def body(it, carry):
            for u in range(ATT_RESIDUE_UNROLL):
                residue(it * ATT_RESIDUE_UNROLL + u)
            return carry

        lax.fori_loop(0, dilation // ATT_RESIDUE_UNROLL, body, 0)


def att_group(P, g, dilation):
    B, S, NP = P.shape
    rows = ATT_BLOCK * dilation
    nb = S // rows
    heads = ATT_GROUP_HEADS if dilation == 1 else 2
    width = heads * ATT_HEAD_DIM
    npairs = ATT_GROUP_HEADS // heads
    qo, ko, vo = (o // width + g * npairs for o in (OFF_ATT_Q, OFF_ATT_K, OFF_ATT_V))
    blk = (1, rows, width)
    if npairs == 1:
        grid, sem = (B, nb), ("parallel", "arbitrary")
        cur = lambda off: pl.BlockSpec(blk, lambda b, n: (b, n, off))
        prev = lambda off: pl.BlockSpec(blk, lambda b, n: (b, jnp.maximum(n - 1, 0), off))
        ospec = pl.BlockSpec(blk, lambda b, n: (b, n, 0))
    else:
        grid, sem = (B, npairs, nb), ("parallel", "parallel", "arbitrary")
        cur = lambda off: pl.BlockSpec(blk, lambda b, hp, n: (b, n, off + hp))
        prev = lambda off: pl.BlockSpec(blk, lambda b, hp, n: (b, jnp.maximum(n - 1, 0), off + hp))
        ospec = pl.BlockSpec(blk, lambda b, hp, n: (b, n, hp))
    oshape = jax.ShapeDtypeStruct((B, S, ATT_OUT_WIDTH), F32)
    slopes = ALIBI_SLOPES[g * ATT_GROUP_HEADS:(g + 1) * ATT_GROUP_HEADS]
    return pl.pallas_call(
        functools.partial(_att_kernel, slopes=slopes, dilation=dilation, heads=heads),
        out_shape=(oshape, oshape, oshape),
        grid=grid,
        in_specs=[cur(qo), cur(ko), prev(ko), cur(vo), prev(vo)],
        out_specs=(ospec, ospec, ospec),
        compiler_params=_cparams(sem),
        name=f"dilated_attention_g{g}",
    )(P, P, P, P, P)


def _att_merge_kernel(*refs):
    o_ref = refs[-1]
    nums, dens, ms = refs[0:3], refs[3:6], refs[6:9]
    mx = jnp.maximum(jnp.maximum(ms[0][0], ms[1][0]), ms[2][0])
    num = jnp.zeros_like(mx)
    den = jnp.zeros_like(mx)
    for g in range(3):
        e = jnp.exp(ms[g][0] - mx)
        num = num + nums[g][0] * e
        den = den + dens[g][0] * e
    o_ref[0] = num / den


def dilated_attention(P):
    B, S, _ = P.shape
    parts = [att_group(P, g, d) for g, (_, d) in enumerate(ATT_GROUPS)]
    args = [p[0] for p in parts] + [p[1] for p in parts] + [p[2] for p in parts]
    ts = 1024
    spec = pl.BlockSpec((1, ts, ATT_OUT_WIDTH), lambda b, i: (b, i, 0))
    return pl.pallas_call(
        _att_merge_kernel,
        out_shape=jax.ShapeDtypeStruct((B, S, ATT_OUT_WIDTH), F32),
        grid=(B, S // ts),
        in_specs=[spec] * 9,
        out_specs=spec,
        compiler_params=_cparams(("parallel", "parallel")),
        name="dilated_attention_merge",
    )(*args)


GLA_TS = 256


def _gla_kernel(q_ref, k_ref, v_ref, g_ref, a_ref, wa_ref, ba_ref, ng_ref, o_ref, st_ref):
    @pl.when(pl.program_id(1) == 0)
    def _():
        st_ref[...] = jnp.zeros_like(st_ref)

    TS = GLA_TS
    nchunk = TS // GLA_CHUNK
    logit = jnp.dot(a_ref[0], wa_ref[...], precision=HIGHEST, preferred_element_type=F32) + ba_ref[...]
    log_a = (jnp.minimum(logit, 0.0) - jnp.log1p(jnp.exp(-jnp.abs(logit)))) / GLA_GATE_TEMP
    i = lax.broadcasted_iota(jnp.int32, (TS, TS), 0)
    j = lax.broadcasted_iota(jnp.int32, (TS, TS), 1)
    same_chunk_causal = jnp.logical_and(i // GLA_CHUNK == j // GLA_CHUNK, j <= i)
    tri = same_chunk_causal.astype(F32)
    bcum = _dot_exact_lhs(tri, log_a)
    eb = jnp.exp(bcum)
    qd = q_ref[0] * (GLA_DK ** -0.5) * eb
    kd = k_ref[0] * jnp.exp(-bcum)
    k_all = k_ref[0]
    v_all = v_ref[0]
    dn_nt = (((1,), (1,)), ((), ()))
    dn_tn = (((0,), (0,)), ((), ()))
    heads = range(GLA_HEADS)
    ks = [slice(h * GLA_DK, (h + 1) * GLA_DK) for h in heads]
    vs = [slice(h * GLA_DV, (h + 1) * GLA_DV) for h in heads]
    qh = [qd[:, ks[h]].astype(BF16) for h in heads]
    vh = [v_all[:, vs[h]].astype(BF16) for h in heads]
    scores = [lax.dot_general(qh[h], kd[:, ks[h]].astype(BF16), dn_nt, preferred_element_type=F32) for h in heads]
    scores = [jnp.where(same_chunk_causal, sc, 0.0).astype(BF16) for sc in scores]
    o_intra = [jnp.dot(scores[h], vh[h], preferred_element_type=F32) for h in heads]
    st = [st_ref[h] for h in heads]
    outs = [[] for _ in heads]
    for c in range(nchunk):
        rs = slice(c * GLA_CHUNK, (c + 1) * GLA_CHUNK)
        last = (c + 1) * GLA_CHUNK - 1
        for h in heads:
            o_c = o_intra[h][rs] + lax.dot_general(qh[h][rs], st[h].astype(BF16), dn_nt,
                                                   preferred_element_type=F32)
            outs[h].append(o_c)
            b_last = bcum[last:last + 1, ks[h]]
            k_tail = (k_all[rs, ks[h]] * jnp.exp(b_last - bcum[rs, ks[h]])).astype(BF16)
            st[h] = st[h] * jnp.exp(b_last) + lax.dot_general(vh[h][rs], k_tail, dn_tn,
                                                              preferred_element_type=F32)
    for h in heads:
        st_ref[h] = st[h]
        o = jnp.concatenate(outs[h], axis=0)
        o = o * lax.rsqrt(jnp.mean(o * o, -1, keepdims=True) + GLA_NORM_EPS)
        gg = g_ref[0][:, vs[h]]
        o_ref[0, :, vs[h]] = o * ng_ref[:, vs[h]] * (gg * jax.nn.sigmoid(gg))


def gla_mixer(P, w_alpha, b_alpha, norm_g):
    B, S, _ = P.shape
    TS = GLA_TS
    wa = jnp.zeros((GLA_A_PAD, GLA_KEY_WIDTH), F32).at[:GLA_GATE_RANK].set(w_alpha)
    col = lambda off, w: pl.BlockSpec((1, TS, w), lambda b, i: (b, i, off // w))
    full = lambda r, c: pl.BlockSpec((r, c), lambda b, i: (0, 0))
    return pl.pallas_call(
        _gla_kernel,
        out_shape=jax.ShapeDtypeStruct((B, S, GLA_VAL_WIDTH), F32),
        grid=(B, S // TS),
        in_specs=[col(OFF_GLA_Q, GLA_KEY_WIDTH), col(OFF_GLA_K, GLA_KEY_WIDTH),
                  col(OFF_GLA_V, GLA_VAL_WIDTH), col(OFF_GLA_G, GLA_VAL_WIDTH),
                  col(OFF_GLA_A, GLA_A_PAD),
                  full(GLA_A_PAD, GLA_KEY_WIDTH), full(1, GLA_KEY_WIDTH), full(1, GLA_VAL_WIDTH)],
        out_specs=pl.BlockSpec((1, TS, GLA_VAL_WIDTH), lambda b, i: (b, i, 0)),
        scratch_shapes=[pltpu.VMEM((GLA_HEADS, GLA_DV, GLA_DK), F32)],
        compiler_params=_cparams(("parallel", "arbitrary")),
        name="gla_mixer",
    )(P, P, P, P, P, wa, b_alpha.reshape(1, -1), norm_g.reshape(1, -1))


RWKV_PREP_TS = 512
RWKV_SHIFT_LO = RWKV_DECAY_RANK + RWKV_AAA_RANK + RWKV_GATE_RANK


def _segment_sum(x, seg):
    n = x.shape[-1]
    i = lax.broadcasted_iota(jnp.int32, (n, n), 0)
    j = lax.broadcasted_iota(jnp.int32, (n, n), 1)
    ones = jnp.where(i // seg == j // seg, 1.0, 0.0).astype(BF16)
    hi, lo = _split_bf16(x)
    return jnp.dot(hi, ones, preferred_element_type=F32) + jnp.dot(lo, ones, preferred_element_type=F32)


def _rwkv_prep_kernel(rkv_ref, lo_ref, rkv_prev_ref, lo_prev_ref, mu_rkv_ref, mu_lo_ref, w0_ref, w2_ref,
                      a0_ref, a2_ref, g2_ref, kk_ref, ka_ref, rk_ref,
                      r_out, w_out, k_out, v_out, kk_out, b_out, gate_out, bonus_out):
    first = pl.program_id(1) == 0

    def shifted(cur, prev_ref, mu):
        prev_row = jnp.where(first, 0.0, prev_ref[0][SUBLANES - 1:SUBLANES, :])
        row = lax.broadcasted_iota(jnp.int32, cur.shape, 0)
        prev = jnp.where(row == 0, prev_row, pltpu.roll(cur, 1, 0))
        return cur + (prev - cur) * mu

    xs = shifted(rkv_ref[0], rkv_prev_ref, mu_rkv_ref[...])
    lo = shifted(lo_ref[0], lo_prev_ref, mu_lo_ref[...])
    W = RWKV_WIDTH
    r, k, v = xs[:, 0:W], xs[:, W:2 * W], xs[:, 2 * W:3 * W]
    z = w0_ref[...] + _dot3(jnp.tanh(lo), w2_ref[...])
    log_decay = -jax.nn.sigmoid(z) * float(np.exp(-0.5))
    a = jax.nn.sigmoid(a0_ref[...] + _dot3(lo, a2_ref[...]))
    gate = _dot3(jax.nn.sigmoid(lo), g2_ref[...])
    kk = k * kk_ref[...]
    kk_norm = jnp.sqrt(_segment_sum(kk * kk, RWKV_HEAD_DIM))
    kk = kk / jnp.maximum(kk_norm, 1e-12)
    k2 = k * (1.0 + (a - 1.0) * ka_ref[...])
    bonus = _segment_sum(r * k2 * rk_ref[...], RWKV_HEAD_DIM) * v
    r_out[0] = r
    w_out[0] = log_decay
    k_out[0] = k2
    v_out[0] = v
    kk_out[0] = kk
    b_out[0] = kk * a
    gate_out[0] = gate
    bonus_out[0] = bonus


def rwkv_prep(P, mu, w0, w2, a0, a2, g2, k_k, k_a, r_k):
    B, S, _ = P.shape
    TS = RWKV_PREP_TS
    W = RWKV_WIDTH
    W3 = 3 * W
    n_rkv = SRC_RWKV_RKV[1] - SRC_RWKV_RKV[0]
    mu_rkv = mu[:n_rkv].reshape(1, W3)
    mu_lo = jnp.zeros((1, RWKV_LO_PAD), F32).at[0, :RWKV_SHIFT_LO].set(mu[n_rkv:])
    w2p = jnp.zeros((RWKV_LO_PAD, W), F32).at[0:RWKV_DECAY_RANK].set(w2)
    a2p = jnp.zeros((RWKV_LO_PAD, W), F32).at[RWKV_DECAY_RANK:RWKV_DECAY_RANK + RWKV_AAA_RANK].set(a2)
    g2p = jnp.zeros((RWKV_LO_PAD, W), F32).at[RWKV_DECAY_RANK + RWKV_AAA_RANK:RWKV_SHIFT_LO].set(g2)
    row = lambda a: a.reshape(1, -1)
    full = lambda r, c: pl.BlockSpec((r, c), lambda b, i: (0, 0))
    tpb = TS // SUBLANES
    prev_map = lambda off, w: pl.BlockSpec((1, SUBLANES, w), lambda b, i: (b, jnp.maximum(i * tpb - 1, 0), off // w))
    oshape = jax.ShapeDtypeStruct((B, S, W), F32)
    ospec = pl.BlockSpec((1, TS, W), lambda b, i: (b, i, 0))
    return pl.pallas_call(
        _rwkv_prep_kernel,
        out_shape=(oshape,) * 8,
        grid=(B, S // TS),
        in_specs=[pl.BlockSpec((1, TS, W3), lambda b, i: (b, i, OFF_RWKV_RKV // W3)),
                  pl.BlockSpec((1, TS, RWKV_LO_PAD), lambda b, i: (b, i, OFF_RWKV_LO // RWKV_LO_PAD)),
                  prev_map(OFF_RWKV_RKV, W3), prev_map(OFF_RWKV_LO, RWKV_LO_PAD),
                  full(1, W3), full(1, RWKV_LO_PAD), full(1, W), full(RWKV_LO_PAD, W),
                  full(1, W), full(RWKV_LO_PAD, W), full(RWKV_LO_PAD, W),
                  full(1, W), full(1, W), full(1, W)],
        out_specs=(ospec,) * 8,
        compiler_params=_cparams(("parallel", "arbitrary")),
        name="rwkv7_prep",
    )(P, P, P, P, mu_rkv, mu_lo, row(w0), w2p, row(a0), a2p, g2p, row(k_k), row(k_a), row(r_k))


RWKV_CHUNK = 64
RWKV_CHUNK_TT = 512
RWKV_PAIRS_PER_STEP = 4


def _rwkv_chunk_kernel(r_ref, lw_ref, k_ref, v_ref, kk_ref, b_ref, y_ref, h_ref):
    @pl.when(pl.program_id(2) == 0)
    def _():
        h_ref[...] = jnp.zeros_like(h_ref)

    C = RWKV_CHUNK
    TT = RWKV_CHUNK_TT
    N = RWKV_HEAD_DIM
    R2 = 2 * C
    bdot = lambda x, y: jnp.dot(x.astype(BF16), y.astype(BF16), preferred_element_type=F32)
    r, lw, k, v, kk, b = (ref[0] for ref in (r_ref, lw_ref, k_ref, v_ref, kk_ref, b_ref))
    i = lax.broadcasted_iota(jnp.int32, (TT, TT), 0)
    j = lax.broadcasted_iota(jnp.int32, (TT, TT), 1)
    tri = jnp.logical_and(i // C == j // C, j <= i).astype(F32)
    cum = _dot_exact_lhs(tri, lw)
    lane_lo = lax.broadcasted_iota(jnp.int32, (C, LANES), 1) < N

    def stack(x):
        return jnp.concatenate([jnp.where(lane_lo, x, 0.0), jnp.where(lane_lo, 0.0, x)], axis=0)

    ti = lax.broadcasted_iota(jnp.int32, (R2, R2), 0)
    si = lax.broadcasted_iota(jnp.int32, (R2, R2), 1)
    strict = ti > si
    incl = ti >= si
    eye = (ti == si).astype(F32)
    blk16 = ti // 16 == si // 16
    off32 = jnp.logical_and(ti // 32 == si // 32, jnp.logical_not(blk16))
    off64 = jnp.logical_and(ti // 64 == si // 64, ti // 32 != si // 32)
    ones_c = jnp.ones((C, LANES), F32)

    items = [(c, p) for c in range(TT // C) for p in range(RWKV_PAIRS_PER_STEP)]
    each = lambda f, *lists: [f(*args) for args in zip(*lists)]
    lanes_of = lambda p: slice(p * LANES, (p + 1) * LANES)
    rows = [(slice(c * C, (c + 1) * C), lanes_of(p)) for c, p in items]
    cu = [cum[rw] for rw in rows]
    cu_last = [cum[(c + 1) * C - 1:(c + 1) * C, lanes_of(p)] for c, p in items]
    e_neg = each(lambda x: jnp.exp(-x), cu)
    e_tail = each(lambda x, xl: jnp.exp(xl - x), cu, cu_last)
    a_t = each(lambda rw, x: stack(-kk[rw] * jnp.exp(x - lw[rw])), rows, cu)
    r_t = each(lambda rw, x: stack(r[rw] * jnp.exp(x)), rows, cu)
    b_t = each(lambda rw, e: stack(b[rw] * e), rows, e_neg)
    k_t = each(lambda rw, e: stack(k[rw] * e), rows, e_neg)
    v_s = each(lambda rw: stack(v[rw]), rows)
    gram = each(lambda a_, r_, b_, k_: lax.dot_general(
        jnp.concatenate([a_, r_], 0).astype(BF16), jnp.concatenate([b_, k_], 0).astype(BF16),
        (((1,), (1,)), ((), ())), preferred_element_type=F32), a_t, r_t, b_t, k_t)
    n_ab = each(lambda g: jnp.where(strict, g[:R2, :R2], 0.0), gram)
    a_ak = each(lambda g: jnp.where(strict, g[:R2, R2:], 0.0), gram)
    a_rb = each(lambda g: jnp.where(incl, g[R2:, :R2], 0.0), gram)
    a_rk = each(lambda g: jnp.where(incl, g[R2:, R2:], 0.0), gram)
    av = each(lambda ak, rk, vv: bdot(jnp.concatenate([ak, rk], 0), vv), a_ak, a_rk, v_s)
    n1 = each(lambda n: jnp.where(blk16, n, 0.0), n_ab)
    n2 = each(lambda x: bdot(x, x), n1)
    n4 = each(lambda x: bdot(x, x), n2)
    n8 = each(lambda x: bdot(x, x), n4)
    xa = each(lambda x1, x2: x1 + x2 + bdot(x1, x2), n1, n2)
    xb = each(lambda x4, x8: x4 + x8 + bdot(x4, x8), n4, n8)
    t_inv = each(lambda p, q: eye + p + q + bdot(p, q), xa, xb)
    for off in (off32, off64):
        mid = each(lambda n, t: bdot(jnp.where(off, n, 0.0), t), n_ab, t_inv)
        t_inv = each(lambda t, m_: t + bdot(t, m_), t_inv, mid)
    tw = each(lambda t, a_, av_: bdot(t, jnp.concatenate([a_, av_[:R2]], axis=1)), t_inv, a_t, av)
    bk_t = each(lambda rw, e: jnp.transpose(jnp.concatenate([stack(b[rw] * e), stack(k[rw] * e)], 0)),
                rows, e_tail)
    pc_col = each(lambda rw: jnp.exp(jnp.dot(jnp.transpose(lw[rw]), ones_c, precision=HIGHEST,
                                             preferred_element_type=F32)), rows)
    w1r = each(lambda t, r_: jnp.concatenate([t[:, :LANES], r_], 0), tw, r_t)

    h = [h_ref[p] for p in range(RWKV_PAIRS_PER_STEP)]
    for it, (c, p) in enumerate(items):
        x = bdot(w1r[it], h[p])
        u = x[:R2] + tw[it][:, LANES:]
        y_bd = x[R2:] + bdot(a_rb[it], u) + av[it][R2:]
        y_ref[(0,) + rows[it]] = y_bd[:C] + y_bd[C:]
        h[p] = pc_col[it] * h[p] + bdot(bk_t[it], jnp.concatenate([u, v_s[it]], 0))
    for p in range(RWKV_PAIRS_PER_STEP):
        h_ref[p] = h[p]


def rwkv_chunked(r, lw, k, v, kk, b):
    B, S, W = r.shape
    TT = RWKV_CHUNK_TT
    lanes = RWKV_PAIRS_PER_STEP * LANES
    spec = pl.BlockSpec((1, TT, lanes), lambda bb, p, i: (bb, i, p))
    return pl.pallas_call(
        _rwkv_chunk_kernel,
        out_shape=jax.ShapeDtypeStruct((B, S, W), F32),
        grid=(B, W // lanes, S // TT),
        in_specs=[spec] * 6,
        out_specs=spec,
        scratch_shapes=[pltpu.VMEM((RWKV_PAIRS_PER_STEP, LANES, LANES), F32)],
        compiler_params=_cparams(("parallel", "parallel", "arbitrary")),
        name="rwkv7_chunked",
    )(r, lw, k, v, kk, b)


def _rwkv_post_kernel(y_ref, gate_ref, bonus_ref, g_ref, b_ref, o_ref):
    y = y_ref[0]
    mean = _segment_sum(y, RWKV_HEAD_DIM) * (1.0 / RWKV_HEAD_DIM)
    d = y - mean
    var = _segment_sum(d * d, RWKV_HEAD_DIM) * (1.0 / RWKV_HEAD_DIM)
    yn = d * lax.rsqrt(var + RWKV_GN_EPS) * g_ref[...] + b_ref[...]
    o_ref[0] = (yn + bonus_ref[0]) * gate_ref[0]


def rwkv_post(y, gate, bonus, ln_g, ln_b):
    B, S, W = y.shape
    ts = 512
    spec = pl.BlockSpec((1, ts, W), lambda b, i: (b, i, 0))
    full = pl.BlockSpec((1, W), lambda b, i: (0, 0))
    return pl.pallas_call(
        _rwkv_post_kernel,
        out_shape=jax.ShapeDtypeStruct((B, S, W), F32),
        grid=(B, S // ts),
        in_specs=[spec, spec, spec, full, full],
        out_specs=spec,
        compiler_params=_cparams(("parallel", "parallel")),
        name="rwkv7_post",
    )(y, gate, bonus, ln_g.reshape(1, W), ln_b.reshape(1, W))


def _merge_kernel(h_ref, wg0_ref, wg1_ref, wg2_ref, wg3_ref, ya_ref, yb_ref, yc_ref, yd_ref,
                  wa_ref, wb_ref, wc_ref, wd_ref, o_ref):
    h = h_ref[...]
    acc = None
    branches = ((wg0_ref, ya_ref, wa_ref), (wg1_ref, yb_ref, wb_ref), (wg2_ref, yc_ref, wc_ref),
                (wg3_ref, yd_ref, wd_ref))
    for wg_ref, y_ref, w_ref in branches:
        gate = jax.nn.sigmoid(jnp.dot(h, wg_ref[...], preferred_element_type=F32))
        proj = jnp.dot(y_ref[...].astype(BF16), w_ref[...], preferred_element_type=F32)
        acc = gate * proj if acc is None else acc + gate * proj
    o_ref[...] = acc.astype(o_ref.dtype)


def branch_merge(hb, wg, ys, ws):
    T, D = hb.shape
    tm, tn = 512, 512
    nj = D // tn
    gspecs = [pl.BlockSpec((D, tn), functools.partial(lambda j, i, br: (0, br * nj + j), br=br))
              for br in range(N_BRANCHES)]
    yspecs = [pl.BlockSpec((tm, y.shape[1]), lambda j, i: (i, 0)) for y in ys]
    wspecs = [pl.BlockSpec((w.shape[0], tn), lambda j, i: (0, j)) for w in ws]
    return pl.pallas_call(
        _merge_kernel,
        out_shape=jax.ShapeDtypeStruct((T, D), BF16),
        grid=(nj, T // tm),
        in_specs=[pl.BlockSpec((tm, D), lambda j, i: (i, 0))] + gspecs + yspecs + wspecs,
        out_specs=pl.BlockSpec((tm, tn), lambda j, i: (i, j)),
        compiler_params=_cparams(("parallel", "parallel")),
        name="branch_merge",
    )(hb, wg, wg, wg, wg, *ys, *ws)


ROUTE_TM = 256
PIECE = 16
TILE_USED_ROWS = ROUTE_TM * TOP_K + N_EXPERTS * (PIECE - 1)
TILE_ROWS = 1536
BLOCK_PIECES = MOE_BM // PIECE
DISPATCH_ROWS = 512


def _layer_norm(z, g, b):
    mu = jnp.mean(z, -1, keepdims=True)
    d = z - mu
    var = jnp.mean(d * d, -1, keepdims=True)
    return d * lax.rsqrt(var + LN_EPS) * g + b


def _out_ln_router_kernel(m_ref, wo_ref, x_ref, gate_ref, g_ref, b_ref, sc_ref, sh_ref, rwh_ref, rwl_ref, rb_ref,
                          x1_ref, xbt_ref, wt_ref, lrow_ref, cnt_ref):
    y = jnp.dot(m_ref[...], wo_ref[...], preferred_element_type=F32)
    x1 = _layer_norm(DEEPNORM_ALPHA * x_ref[...] + gate_ref[0] * y, g_ref[...], b_ref[...])
    x1_ref[...] = x1
    h2 = x1 * (1.0 + sc_ref[0]) + sh_ref[0]
    h2_hi, h2_lo = _split_bf16(h2)
    logits = (jnp.dot(h2_hi, rwh_ref[...], preferred_element_type=F32)
              + jnp.dot(h2_hi, rwl_ref[...], preferred_element_type=F32)
              + jnp.dot(h2_lo, rwh_ref[...], preferred_element_type=F32)) + rb_ref[...]
    lane = lax.broadcasted_iota(jnp.int32, logits.shape, 1).astype(F32)
    vals, idxs = [], []
    cur = logits
    for _ in range(TOP_K):
        m = jnp.max(cur, -1, keepdims=True)
        ix = jnp.min(jnp.where(cur == m, lane, float(LANES)), -1, keepdims=True)
        vals.append(m)
        idxs.append(ix)
        cur = jnp.where(lane == ix, -jnp.inf, cur)
    es = [jnp.exp(v - vals[0]) for v in vals]
    tot = es[0] + es[1] + es[2] + es[3]
    sel = jnp.zeros_like(logits)
    for kq in range(TOP_K):
        sel = jnp.where(lane == idxs[kq], 1.0, sel)
    tm = logits.shape[0]
    ti = lax.broadcasted_iota(jnp.int32, (tm, tm), 0)
    si = lax.broadcasted_iota(jnp.int32, (tm, tm), 1)
    before = jnp.dot((si < ti).astype(BF16), sel.astype(BF16), preferred_element_type=F32)
    cnt = jnp.sum(sel, axis=0, keepdims=True)
    cnt_ref[0] = jnp.broadcast_to(cnt, (SUBLANES, LANES)).astype(jnp.int32)
    pieces = jnp.floor((cnt + (PIECE - 1.0)) * (1.0 / PIECE))
    ei = lax.broadcasted_iota(jnp.int32, (LANES, LANES), 0)
    ej = lax.broadcasted_iota(jnp.int32, (LANES, LANES), 1)
    group_row = PIECE * jnp.dot(jnp.broadcast_to(pieces, (SUBLANES, LANES)).astype(BF16), (ei < ej).astype(BF16),
                                preferred_element_type=F32)[0:1]
    pos = before + group_row
    wt_o = jnp.zeros_like(logits)
    lrow_o = jnp.full(logits.shape, -1.0, F32)
    for kq in range(TOP_K):
        lrow_k = jnp.sum(jnp.where(lane == idxs[kq], pos, 0.0), -1, keepdims=True)
        wt_o = jnp.where(lane == kq, es[kq] / tot, wt_o)
        lrow_o = jnp.where(lane == kq, lrow_k, lrow_o)
    wt_ref[...] = wt_o
    lrow_ref[...] = lrow_o.astype(jnp.int32)
    lrow_t = jnp.transpose(lrow_o).astype(jnp.int32)
    for rc in range(TILE_ROWS // DISPATCH_ROWS):
        row = lax.broadcasted_iota(jnp.int32, (DISPATCH_ROWS, tm), 0) + rc * DISPATCH_ROWS
        hit = row == lrow_t[0:1, :]
        for kq in range(1, TOP_K):
            hit = jnp.logical_or(hit, row == lrow_t[kq:kq + 1, :])
        onehot = jnp.where(hit, 1.0, 0.0).astype(BF16)
        xbt_ref[pl.ds(rc * DISPATCH_ROWS, DISPATCH_ROWS), :] = jnp.dot(
            onehot, h2_hi, preferred_element_type=F32).astype(xbt_ref.dtype)


def out_ln_router(merged, w_out_b, x2d, gate1, ln_g, ln_b, scale2, shift2, router_w, router_b, S):
    T, D = x2d.shape
    tm = ROUTE_TM
    nt = T // tm
    spb = S // tm
    rw = jnp.zeros((D, LANES), F32).at[:, :N_EXPERTS].set(router_w)
    rw_hi, rw_lo = _split_bf16(rw)
    rb = jnp.full((1, LANES), NEG_BIG, F32).at[0, :N_EXPERTS].set(router_b)
    rowblk = lambda w: pl.BlockSpec((tm, w), lambda i: (i, 0))
    full = lambda r, c: pl.BlockSpec((r, c), lambda i: (0, 0))
    perb = pl.BlockSpec((1, 1, D), lambda i: (i // spb, 0, 0))
    tile_spec = pl.BlockSpec((1, SUBLANES, LANES), lambda i: (i, 0, 0))
    tile_shape = jax.ShapeDtypeStruct((nt, SUBLANES, LANES), jnp.int32)
    return pl.pallas_call(
        _out_ln_router_kernel,
        out_shape=(jax.ShapeDtypeStruct((T, D), F32), jax.ShapeDtypeStruct((nt * TILE_ROWS, D), BF16),
                   jax.ShapeDtypeStruct((T, LANES), F32), jax.ShapeDtypeStruct((T, LANES), jnp.int32), tile_shape),
        grid=(nt,),
        in_specs=[rowblk(D), full(D, D), rowblk(D), perb, full(1, D), full(1, D), perb, perb,
                  full(D, LANES), full(D, LANES), full(1, LANES)],
        out_specs=(rowblk(D), pl.BlockSpec((TILE_ROWS, D), lambda i: (i, 0)), rowblk(LANES), rowblk(LANES),
                   tile_spec),
        compiler_params=_cparams(("parallel",)),
        name="out_proj_ln_router",
    )(merged, w_out_b, x2d, gate1, ln_g.reshape(1, D), ln_b.reshape(1, D), scale2, shift2, rw_hi, rw_lo, rb)


W_CHUNK_ROWS = 512
W_STAGES = 3
W_NONE, W_FIRST_HALF, W_SECOND_HALF, W_ALL = 0, 1, 2, 3
GU_CHUNKS = D_MODEL // W_CHUNK_ROWS
DN_CHUNKS = EXPERT_FF // W_CHUNK_ROWS


def _expert_kernel(be_ref, nu_ref, src_ref, dst_ref, par_ref, nxt_ref, mode_ref, x_hbm, wgu_hbm, bgu_ref, wd_hbm, bd_ref,
                   y_hbm, xs_ref, ys_ref, wgu_b, wd_b, stg_ref, isem, osem, wsem, *, layer):
    i = pl.program_id(0)
    nu = nu_ref[0]
    slot = i % 2
    n_chunks = GU_CHUNKS + DN_CHUNKS

    def in_copy(blk, k, s):
        src = pl.multiple_of(src_ref[blk * BLOCK_PIECES + k], PIECE)
        return pltpu.make_async_copy(x_hbm.at[pl.ds(src, PIECE), :], xs_ref.at[s, pl.ds(k * PIECE, PIECE), :],
                                     isem.at[s])

    def out_copy(blk, k, s):
        dst = pl.multiple_of(dst_ref[blk * BLOCK_PIECES + k], PIECE)
        return pltpu.make_async_copy(ys_ref.at[s, pl.ds(k * PIECE, PIECE), :], y_hbm.at[pl.ds(dst, PIECE), :],
                                     osem.at[s])

    def all_pieces(fn):
        for k in range(BLOCK_PIECES):
            fn(k)

    def w_copy(e, c):
        s = c % W_STAGES
        if c < GU_CHUNKS:
            src = wgu_hbm.at[layer, e, pl.ds(c * W_CHUNK_ROWS, W_CHUNK_ROWS), :]
        else:
            src = wd_hbm.at[layer, e, pl.ds((c - GU_CHUNKS) * W_CHUNK_ROWS, W_CHUNK_ROWS), :]
        return pltpu.make_async_copy(src, stg_ref.at[s], wsem.at[s])

    def w_convert(c, p):
        w = stg_ref[c % W_STAGES].astype(BF16)
        if c < GU_CHUNKS:
            wgu_b[p, pl.ds(c * W_CHUNK_ROWS, W_CHUNK_ROWS), :] = w
        else:
            wd_b[p, pl.ds((c - GU_CHUNKS) * W_CHUNK_ROWS, W_CHUNK_ROWS), :] = w

    def w_finish(e, p, chunks, refill):
        for c in chunks:
            w_copy(e, c).wait()
            w_convert(c, p)
            if refill and c + W_STAGES < n_chunks:
                w_copy(e, c + W_STAGES).start()

    @pl.when(i == 0)
    def _():
        all_pieces(lambda k: in_copy(0, k, 0).start())
        e0 = be_ref[0]
        for c in range(W_STAGES):
            w_copy(e0, c).start()
        w_finish(e0, par_ref[0], range(n_chunks), True)

    @pl.when(i + 1 < nu)
    def _():
        all_pieces(lambda k: in_copy(i + 1, k, 1 - slot).start())

    @pl.when(i < nu)
    def _():
        p = par_ref[i]
        e_next = nxt_ref[i]
        mode = mode_ref[i]
        first_half = range(W_STAGES)
        second_half = range(W_STAGES, n_chunks)

        @pl.when(jnp.logical_or(mode == W_FIRST_HALF, mode == W_ALL))
        def _():
            for c in first_half:
                w_copy(e_next, c).start()

        @pl.when(mode == W_SECOND_HALF)
        def _():
            for c in second_half:
                w_copy(e_next, c).start()

        all_pieces(lambda k: in_copy(i, k, slot).wait())
        gu = jnp.dot(xs_ref[slot], wgu_b[p], preferred_element_type=F32) + bgu_ref[0]

        @pl.when(mode == W_ALL)
        def _():
            w_finish(e_next, 1 - p, first_half, True)

        glu = jnp.minimum(gu[:, :EXPERT_FF], SWIGLU_LIMIT)
        lin = jnp.clip(gu[:, EXPERT_FF:], -SWIGLU_LIMIT, SWIGLU_LIMIT)
        act = glu * jax.nn.sigmoid(SWIGLU_ALPHA * glu) * (lin + 1.0)

        @pl.when(i >= 2)
        def _():
            all_pieces(lambda k: out_copy(i - 2, k, slot).wait())

        y = jnp.dot(act.astype(BF16), wd_b[p], preferred_element_type=F32) + bd_ref[0]
        ys_ref[slot] = y.astype(ys_ref.dtype)
        all_pieces(lambda k: out_copy(i, k, slot).start())

        @pl.when(mode == W_FIRST_HALF)
        def _():
            w_finish(e_next, 1 - p, first_half, False)

        @pl.when(jnp.logical_or(mode == W_SECOND_HALF, mode == W_ALL))
        def _():
            w_finish(e_next, 1 - p, second_half, False)

    @pl.when(i == nu - 1)
    def _():
        @pl.when(i >= 1)
        def _():
            all_pieces(lambda k: out_copy(i - 1, k, 1 - slot).wait())
        all_pieces(lambda k: out_copy(i, k, slot).wait())


def expert_ffn(block_e, n_used, piece_src, piece_dst, w_parity, next_e, w_mode, xbt, wgu, bgu, wd, bd, layer):
    rows, D = xbt.shape
    _, E, _, F2 = wgu.shape
    nblk = block_e.shape[0]
    bias = lambda w: pl.BlockSpec((1, 1, w), lambda i, be, *_: (be[i], 0, 0))
    grid_spec = pltpu.PrefetchScalarGridSpec(
        num_scalar_prefetch=7,
        grid=(nblk,),
        in_specs=[pl.BlockSpec(memory_space=pl.ANY), pl.BlockSpec(memory_space=pl.ANY), bias(F2),
                  pl.BlockSpec(memory_space=pl.ANY), bias(D)],
        out_specs=pl.BlockSpec(memory_space=pl.ANY),
        scratch_shapes=[pltpu.VMEM((2, MOE_BM, D), BF16), pltpu.VMEM((2, MOE_BM, D), BF16),
                        pltpu.VMEM((2, D, F2), BF16), pltpu.VMEM((2, F2 // 2, D), BF16),
                        pltpu.VMEM((W_STAGES, W_CHUNK_ROWS, D), F32),
                        pltpu.SemaphoreType.DMA((2,)), pltpu.SemaphoreType.DMA((2,)),
                        pltpu.SemaphoreType.DMA((W_STAGES,))],
    )
    return pl.pallas_call(
        functools.partial(_expert_kernel, layer=layer),
        out_shape=jax.ShapeDtypeStruct((rows, D), BF16),
        grid_spec=grid_spec,
        input_output_aliases={7: 0},
        compiler_params=_cparams(("arbitrary",)),
        name="expert_ffn",
    )(block_e, n_used, piece_src, piece_dst, w_parity, next_e, w_mode, xbt, wgu, bgu.reshape(E, 1, F2), wd,
      bd.reshape(E, 1, D))


COMBINE_COLS = 512


def _combine_kernel(meta_ref, wt_ref, x1_ref, gate_ref, g_ref, b_ref, sc_ref, sh_ref, y_ref,
                    x2_ref, hb_ref, acc_ref):
    meta = meta_ref[...]
    wts = wt_ref[...]
    CW = COMBINE_COLS
    col0 = lax.broadcasted_iota(jnp.int32, (meta.shape[0], CW), 1)
    for cc in range(TILE_ROWS // CW):
        hi = jnp.zeros(col0.shape, F32)
        lo = jnp.zeros(col0.shape, F32)
        for kq in range(TOP_K):
            hit = col0 == meta[:, kq:kq + 1] - cc * CW
            w = wts[:, kq:kq + 1]
            w_hi = w.astype(BF16).astype(F32)
            hi = jnp.where(hit, w_hi, hi)
            lo = jnp.where(hit, w - w_hi, lo)
        rows = y_ref[pl.ds(cc * CW, CW), :]
        part = (jnp.dot(hi.astype(BF16), rows, preferred_element_type=F32)
                + jnp.dot(lo.astype(BF16), rows, preferred_element_type=F32))
        if cc == 0:
            acc_ref[...] = part
        else:
            acc_ref[...] += part

    x2 = _layer_norm(DEEPNORM_ALPHA * x1_ref[...] + gate_ref[0] * acc_ref[...], g_ref[...], b_ref[...])
    x2_ref[...] = x2
    hb_ref[...] = (x2 * (1.0 + sc_ref[0]) + sh_ref[0]).astype(hb_ref.dtype)


def moe_combine_ln(ybt, meta, wt, x1, gate2, ln_g, ln_b, scale_next, shift_next, S):
    T, D = x1.shape
    tm = ROUTE_TM
    spb = S // tm
    rowblk = lambda w: pl.BlockSpec((tm, w), lambda i: (i, 0))
    full = pl.BlockSpec((1, D), lambda i: (0, 0))
    perb = pl.BlockSpec((1, 1, D), lambda i: (i // spb, 0, 0))
    return pl.pallas_call(
        _combine_kernel,
        out_shape=(jax.ShapeDtypeStruct((T, D), F32), jax.ShapeDtypeStruct((T, D), BF16)),
        grid=(T // tm,),
        in_specs=[rowblk(LANES), rowblk(LANES), rowblk(D), perb, full, full, perb, perb,
                  pl.BlockSpec((TILE_ROWS, D), lambda i: (i, 0))],
        out_specs=(rowblk(D), rowblk(D)),
        scratch_shapes=[pltpu.VMEM((tm, D), F32)],
        compiler_params=_cparams(("parallel",)),
        name="moe_combine_ln",
    )(meta, wt, x1, gate2, ln_g.reshape(1, D), ln_b.reshape(1, D), scale_next, shift_next, ybt)


def routed_ffn_ln(xbt, lrow, wt, tile_cnt, wgu, bgu, wd, bd, layer, x1, gate2, ln_g, ln_b,
                  scale_next, shift_next, S):
    T, D = x1.shape
    E = N_EXPERTS
    nt = T // ROUTE_TM
    i32 = jnp.int32
    cnt = tile_cnt[:, 0, :E]
    npc = (cnt + PIECE - 1) // PIECE
    seg_off = (jnp.cumsum(npc, axis=1) - npc) * PIECE
    per_e = jnp.sum(npc, axis=0)
    per_e_pad = ((per_e + BLOCK_PIECES - 1) // BLOCK_PIECES) * BLOCK_PIECES
    e_end = jnp.cumsum(per_e_pad)
    e_start = e_end - per_e_pad
    t_end = jnp.cumsum(npc, axis=0).T
    t_start = t_end - npc.T
    max_pieces = nt * (TILE_USED_ROWS // PIECE) + E * (BLOCK_PIECES - 1)
    nblk = -(-max_pieces // BLOCK_PIECES)
    q = jnp.arange(nblk * BLOCK_PIECES, dtype=i32)
    e_q = jnp.minimum(jnp.sum((e_end[None, :] <= q[:, None]).astype(i32), axis=1), E - 1)
    sel_e = e_q[:, None] == jnp.arange(E, dtype=i32)[None, :]
    of_expert = lambda table: jnp.sum(jnp.where(sel_e, table[None, :], 0), axis=1)
    w_q = q - of_expert(e_start)
    real = w_q < of_expert(per_e)
    rows_of = lambda table: jnp.sum(jnp.where(sel_e[:, :, None], table[None], 0), axis=1)
    t_q = jnp.minimum(jnp.sum((rows_of(t_end) <= w_q[:, None]).astype(i32), axis=1), nt - 1)
    sel_t = t_q[:, None] == jnp.arange(nt, dtype=i32)[None, :]
    of_tile = lambda table: jnp.sum(jnp.where(sel_t, rows_of(table), 0), axis=1)
    k_q = w_q - of_tile(t_start)
    row_q = t_q * TILE_ROWS + of_tile(seg_off.T) + k_q * PIECE
    tail_pieces = (TILE_ROWS - TILE_USED_ROWS) // PIECE
    assert 2 * BLOCK_PIECES <= nt * tail_pieces, "not enough tile tail rows for the padding pieces"
    scratch_id = ((q // BLOCK_PIECES) % 2) * BLOCK_PIECES + q % BLOCK_PIECES
    scratch_q = (scratch_id // tail_pieces) * TILE_ROWS + TILE_USED_ROWS + (scratch_id % tail_pieces) * PIECE
    piece_src = jnp.where(real, row_q, 0).astype(i32)
    piece_dst = jnp.where(real, row_q, scratch_q).astype(i32)
    block_e = e_q[::BLOCK_PIECES]
    n_used = (e_end[-1] // BLOCK_PIECES).astype(i32)
    blk = jnp.arange(nblk, dtype=i32)
    nxt = jnp.concatenate([block_e[1:], block_e[-1:]])
    switch_next = jnp.logical_and(nxt != block_e, blk + 1 < n_used)
    w_parity = (jnp.cumsum(switch_next.astype(i32)) - switch_next.astype(i32)) % 2
    prv = jnp.concatenate([block_e[:1] - 1, block_e[:-1]])
    only_block = jnp.logical_and(switch_next, prv != block_e)
    before_last = jnp.logical_and(jnp.concatenate([switch_next[1:], switch_next[-1:] & False]), nxt == block_e)
    w_mode = jnp.where(only_block, W_ALL, jnp.where(switch_next, W_SECOND_HALF,
                                                    jnp.where(before_last, W_FIRST_HALF, W_NONE))).astype(i32)
    nxt2 = jnp.concatenate([block_e[2:], block_e[-1:], block_e[-1:]])
    next_e = jnp.where(switch_next, nxt, jnp.where(before_last, nxt2, 0)).astype(i32)

    ybt = expert_ffn(block_e, n_used.reshape(1), piece_src, piece_dst, w_parity, next_e, w_mode, xbt, wgu, bgu, wd,
                     bd, layer)
    return moe_combine_ln(ybt, lrow, wt, x1, gate2, ln_g, ln_b, scale_next, shift_next, S)


def _mix_weights(w_in_l):
    def cols(rng, pad=0):
        part = w_in_l[:, rng[0]:rng[1]]
        if pad:
            part = jnp.pad(part, ((0, 0), (0, pad)))
        return part
    w_mix = jnp.concatenate([
        cols(SRC_POOL), cols(SRC_GLA_V), cols(SRC_GLA_G), cols(SRC_RWKV_RKV), cols(SRC_ATT),
        cols(SRC_GLA_Q), cols(SRC_GLA_K),
        cols(SRC_RWKV_LO, RWKV_LO_PAD - (SRC_RWKV_LO[1] - SRC_RWKV_LO[0])),
        cols(SRC_GLA_A, GLA_A_PAD - (SRC_GLA_A[1] - SRC_GLA_A[0]))], axis=1).astype(BF16)
    w_gates = w_in_l[:, SRC_GATES[0]:].astype(BF16)
    return w_mix, w_gates


def kernel(x, c, ada_w, ada_b, w_in, pool_w, pool_scale, gla_w_alpha, gla_b_alpha, gla_norm_g, rwkv_mu, rwkv_w0, rwkv_w2, rwkv_a0, rwkv_a2, rwkv_g2, rwkv_k_k, rwkv_k_a, rwkv_r_k, rwkv_ln_g, rwkv_ln_b, w_branch_a, w_branch_b, w_branch_c, w_branch_d, w_out, ln1_g, ln1_b, router_w, router_b, w_gate_up, b_gate_up, w_down, b_down, ln2_g, ln2_b):
    B, S, D = x.shape
    T = B * S
    mod = ada_modulation(c, ada_w, ada_b)
    mods = [[mod[l, :, None, i * D:(i + 1) * D] for i in range(6)] for l in range(DEPTH)]
    x2d = x.reshape(T, D)
    hb2d = modulate(x, mods[0][1], mods[0][0]).reshape(T, D)
    for l in range(DEPTH):
        shift1, scale1, gate1, shift2, scale2, gate2 = mods[l]
        shift_next, scale_next = (mods[l + 1][0], mods[l + 1][1]) if l + 1 < DEPTH else (shift1, scale1)
        w_mix, w_gates = _mix_weights(w_in[l])
        P = matmul(hb2d, w_mix, 1024, MIX_TN).reshape(B, S, MIX_WIDTH)
        y_a = pool_mixer(P, pool_w[l], pool_scale[l])
        y_b = dilated_attention(P)
        y_c = gla_mixer(P, gla_w_alpha[l], gla_b_alpha[l], gla_norm_g[l])
        r_, w_, k_, v_, kk_, b_, gate_, bonus_ = rwkv_prep(
            P, rwkv_mu[l], rwkv_w0[l], rwkv_w2[l], rwkv_a0[l], rwkv_a2[l], rwkv_g2[l],
            rwkv_k_k[l], rwkv_k_a[l], rwkv_r_k[l].reshape(-1))
        y_t = rwkv_chunked(r_, w_, k_, v_, kk_, b_)
        y_d = rwkv_post(y_t, gate_, bonus_, rwkv_ln_g[l], rwkv_ln_b[l])
        ys = [y.reshape(T, -1) for y in (y_a, y_b, y_c, y_d)]
        ws = [w[l].astype(BF16) for w in (w_branch_a, w_branch_b, w_branch_c, w_branch_d)]
        merged = branch_merge(hb2d, w_gates, ys, ws)
        x1, xbt, wt, lrow, tile_cnt = out_ln_router(
            merged, w_out[l].astype(BF16), x2d, gate1, ln1_g[l], ln1_b[l], scale2, shift2,
            router_w[l], router_b[l], S)
        x2d, hb2d = routed_ffn_ln(xbt, lrow, wt, tile_cnt, w_gate_up, b_gate_up[l], w_down, b_down[l], l,
                                  x1, gate2, ln2_g[l], ln2_b[l], scale_next, shift_next, S)
    return x2d.reshape(B, S, D)
```

```python
import functools

import numpy as np
import jax
import jax.numpy as jnp
from jax import lax
from jax.experimental import pallas as pl
from jax.experimental.pallas import tpu as pltpu

F32 = jnp.float32
BF16 = jnp.bfloat16
HIGHEST = lax.Precision.HIGHEST

D_MODEL = 2048
DEPTH = 2
POOL_WINDOWS = (2, 4, 8, 16)
POOL_GROUP = 128
POOL_WIDTH = 512
ATT_GROUPS = ((128, 1), (512, 4), (2048, 16))
ATT_HEAD_DIM = 64
ATT_HEADS = 12
ATT_WIDTH = 768
ATT_OUT_WIDTH = 256
ATT_BLOCK = 128
ALIBI_SLOPES = tuple(2.0 ** (-8.0 * (h + 1) / ATT_HEADS) for h in range(ATT_HEADS))
GLA_HEADS = 4
GLA_DK = 64
GLA_DV = 128
GLA_KEY_WIDTH = 256
GLA_VAL_WIDTH = 512
GLA_GATE_RANK = 16
GLA_GATE_TEMP = 16.0
GLA_CHUNK = 32
GLA_NORM_EPS = 1e-6
RWKV_HEADS = 8
RWKV_HEAD_DIM = 64
RWKV_WIDTH = 512
RWKV_DECAY_RANK = 32
RWKV_AAA_RANK = 32
RWKV_GATE_RANK = 96
RWKV_GN_EPS = 64e-5
N_BRANCHES = 4
N_EXPERTS = 32
TOP_K = 4
EXPERT_FF = 1024
SWIGLU_LIMIT = 7.0
SWIGLU_ALPHA = 1.702
LN_EPS = 1e-5
DEEPNORM_ALPHA = (2 * DEPTH) ** 0.25

LANES = 128
SUBLANES = 8
VMEM_LIMIT = 56 * 1024 * 1024

SRC_POOL = (0, 512)
SRC_ATT = (512, 2816)
SRC_GLA_Q = (2816, 3072)
SRC_GLA_K = (3072, 3328)
SRC_GLA_V = (3328, 3840)
SRC_GLA_G = (3840, 4352)
SRC_GLA_A = (4352, 4368)
SRC_RWKV_RKV = (4368, 5904)
SRC_RWKV_LO = (5904, 6064)
SRC_GATES = (6064, 14256)
OFF_POOL = 0
OFF_GLA_V = 512
OFF_GLA_G = 1024
OFF_RWKV_RKV = 1536
OFF_ATT_Q = 3072
OFF_ATT_K = OFF_ATT_Q + ATT_WIDTH
OFF_ATT_V = OFF_ATT_K + ATT_WIDTH
OFF_GLA_Q = 5376
OFF_GLA_K = 5632
OFF_RWKV_LO = 5888
RWKV_LO_PAD = 256
OFF_GLA_A = 6144
GLA_A_PAD = 128
MIX_WIDTH = 6272
MIX_TN = 896

MOE_BM = 256
NEG_BIG = -1e30


def _cparams(sem):
    return pltpu.CompilerParams(dimension_semantics=sem, vmem_limit_bytes=VMEM_LIMIT)


def _split_bf16(x):
    hi = x.astype(BF16)
    return hi, (x - hi.astype(F32)).astype(BF16)


def _dot3(x, w):
    x_hi, x_lo = _split_bf16(x)
    w_hi, w_lo = _split_bf16(w)
    return (jnp.dot(x_hi, w_hi, preferred_element_type=F32) + jnp.dot(x_hi, w_lo, preferred_element_type=F32)
            + jnp.dot(x_lo, w_hi, preferred_element_type=F32))


def _dot_exact_lhs(m, x):
    hi = x.astype(BF16)
    r1 = x - hi.astype(F32)
    mid = r1.astype(BF16)
    lo = (r1 - mid.astype(F32)).astype(BF16)
    mb = m.astype(BF16)
    return (jnp.dot(mb, hi, preferred_element_type=F32) + jnp.dot(mb, mid, preferred_element_type=F32)
            + jnp.dot(mb, lo, preferred_element_type=F32))


def _ada_kernel(c_ref, w_ref, b_ref, o_ref):
    c = c_ref[...]
    s = c * jax.nn.sigmoid(c)
    o_ref[0] = _dot3(s, w_ref[0]) + b_ref[0]


def ada_modulation(c, ada_w, ada_b):
    L, D, N = ada_w.shape
    B = c.shape[0]
    cp = jnp.zeros((SUBLANES, D), F32).at[:B].set(c)
    tn = 1024
    out = pl.pallas_call(
        _ada_kernel,
        out_shape=jax.ShapeDtypeStruct((L, SUBLANES, N), F32),
        grid=(L, N // tn),
        in_specs=[pl.BlockSpec((SUBLANES, D), lambda l, j: (0, 0)),
                  pl.BlockSpec((1, D, tn), lambda l, j: (l, 0, j)),
                  pl.BlockSpec((1, 1, tn), lambda l, j: (l, 0, j))],
        out_specs=pl.BlockSpec((1, SUBLANES, tn), lambda l, j: (l, 0, j)),
        compiler_params=_cparams(("parallel", "parallel")),
        name="ada_modulation",
    )(cp, ada_w, ada_b.reshape(L, 1, N))
    return out[:, :B]


def _modulate_kernel(x_ref, sc_ref, sh_ref, o_ref):
    o_ref[0] = (x_ref[0] * (1.0 + sc_ref[0]) + sh_ref[0]).astype(o_ref.dtype)


def modulate(x, scale, shift):
    B, S, D = x.shape
    ts = 1024
    return pl.pallas_call(
        _modulate_kernel,
        out_shape=jax.ShapeDtypeStruct((B, S, D), BF16),
        grid=(B, S // ts),
        in_specs=[pl.BlockSpec((1, ts, D), lambda b, i: (b, i, 0)),
                  pl.BlockSpec((1, 1, D), lambda b, i: (b, 0, 0)),
                  pl.BlockSpec((1, 1, D), lambda b, i: (b, 0, 0))],
        out_specs=pl.BlockSpec((1, ts, D), lambda b, i: (b, i, 0)),
        compiler_params=_cparams(("parallel", "parallel")),
        name="modulate",
    )(x, scale, shift)


def _mm_kernel(x_ref, w_ref, o_ref):
    o_ref[...] = jnp.dot(x_ref[...], w_ref[...], preferred_element_type=F32).astype(o_ref.dtype)


def matmul(x, w, tm, tn, out_dtype=F32):
    M, K = x.shape
    N = w.shape[1]
    return pl.pallas_call(
        _mm_kernel,
        out_shape=jax.ShapeDtypeStruct((M, N), out_dtype),
        grid=(N // tn, M // tm),
        in_specs=[pl.BlockSpec((tm, K), lambda j, i: (i, 0)),
                  pl.BlockSpec((K, tn), lambda j, i: (0, j))],
        out_specs=pl.BlockSpec((tm, tn), lambda j, i: (i, j)),
        compiler_params=_cparams(("parallel", "parallel")),
        name="matmul",
    )(x, w)


def _pool_kernel(p_ref, w_ref, sc_ref, o_ref):
    g = pl.program_id(1)
    v = p_ref[0]
    S = v.shape[0]
    row = lax.broadcasted_iota(jnp.int32, v.shape, 0)
    win = jnp.left_shift(2, g)
    s = v
    pooled_sum = v
    for k, sh in enumerate((1, 2, 4, 8)):
        s = s + jnp.where(row >= sh, pltpu.roll(s, sh, 0), 0.0)
        pooled_sum = jnp.where(g >= k, s, pooled_sum)
    cnt = jnp.minimum(row + 1, win).astype(F32)
    diff = pooled_sum / cnt - v
    y = jnp.dot(diff.astype(BF16), w_ref[0].astype(BF16), preferred_element_type=F32)
    o_ref[0] = y * sc_ref[...]


def pool_mixer(P, pool_w, pool_scale):
    B, S, _ = P.shape
    G = len(POOL_WINDOWS)
    return pl.pallas_call(
        _pool_kernel,
        out_shape=jax.ShapeDtypeStruct((B, S, POOL_WIDTH), F32),
        grid=(B, G),
        in_specs=[pl.BlockSpec((1, S, POOL_GROUP), lambda b, g: (b, 0, OFF_POOL // POOL_GROUP + g)),
                  pl.BlockSpec((1, POOL_GROUP, POOL_GROUP), lambda b, g: (g, 0, 0)),
                  pl.BlockSpec((1, POOL_GROUP), lambda b, g: (0, g))],
        out_specs=pl.BlockSpec((1, S, POOL_GROUP), lambda b, g: (b, 0, g)),
        compiler_params=_cparams(("parallel", "parallel")),
        name="pool_mixer",
    )(P, pool_w, pool_scale.reshape(1, POOL_WIDTH))


ATT_GROUP_HEADS = 4
ATT_RESIDUE_UNROLL = 4


def _att_kernel(q_ref, kc_ref, kp_ref, vc_ref, vp_ref, num_ref, den_ref, m_ref, *, slopes, dilation, heads):
    if heads == ATT_GROUP_HEADS:
        n = pl.program_id(1)
    else:
        pair = pl.program_id(1)
        n = pl.program_id(2)
        slopes = [jnp.where(pair == 0, slopes[h], slopes[heads + h]) for h in range(heads)]
    i = lax.broadcasted_iota(jnp.int32, (ATT_BLOCK, ATT_BLOCK), 0)
    j = lax.broadcasted_iota(jnp.int32, (ATT_BLOCK, ATT_BLOCK), 1)
    dist_c = i - j
    dist_p = dist_c + ATT_BLOCK
    valid_c = dist_c >= 0
    valid_p = jnp.logical_and(dist_p <= ATT_BLOCK, n > 0)
    bias_c = [-(s * dilation) * dist_c.astype(F32) for s in slopes]
    bias_p = [-(s * dilation) * dist_p.astype(F32) for s in slopes]
    head_of_lane = lax.broadcasted_iota(jnp.int32, (ATT_BLOCK, heads * ATT_HEAD_DIM), 1) // ATT_HEAD_DIM
    dn = (((1,), (1,)), ((), ()))

    def per_head_lanes(cols):
        out = cols[-1]
        for h in range(heads - 2, -1, -1):
            out = jnp.where(head_of_lane == h, cols[h], out)
        return out

    def residue(r):
        rows = slice(None) if dilation == 1 else pl.ds(r, ATT_BLOCK, stride=dilation)
        q4, kc4, kp4, vc4, vp4 = (ref[0, rows, :] for ref in (q_ref, kc_ref, kp_ref, vc_ref, vp_ref))
        nums, dens, ms = [], [], []
        for h in range(heads):
            sl = slice(h * ATT_HEAD_DIM, (h + 1) * ATT_HEAD_DIM)
            q = (q4[:, sl] * (ATT_HEAD_DIM ** -0.5)).astype(BF16)
            s_c = lax.dot_general(q, kc4[:, sl].astype(BF16), dn, preferred_element_type=F32)
            s_p = lax.dot_general(q, kp4[:, sl].astype(BF16), dn, preferred_element_type=F32)
            s_c = jnp.where(valid_c, s_c + bias_c[h], NEG_BIG)
            s_p = jnp.where(valid_p, s_p + bias_p[h], NEG_BIG)
            m = jnp.maximum(jnp.max(s_c, -1, keepdims=True), jnp.max(s_p, -1, keepdims=True))
            p_c = jnp.exp(s_c - m)
            p_p = jnp.exp(s_p - m)
            dens.append(jnp.sum(p_c, -1, keepdims=True) + jnp.sum(p_p, -1, keepdims=True))
            nums.append(jnp.dot(p_c.astype(BF16), vc4[:, sl].astype(BF16), preferred_element_type=F32)
                        + jnp.dot(p_p.astype(BF16), vp4[:, sl].astype(BF16), preferred_element_type=F32))
            ms.append(m)
        num_ref[0, rows, :] = jnp.concatenate(nums, axis=-1)
        den_ref[0, rows, :] = per_head_lanes(dens)
        m_ref[0, rows, :] = per_head_lanes(ms)

    if dilation == 1:
        residue(0)
    else:
        def body(it, carry):
            for u in range(ATT_RESIDUE_UNROLL):
                residue(it * ATT_RESIDUE_UNROLL + u)
            return carry

        lax.fori_loop(0, dilation // ATT_RESIDUE_UNROLL, body, 0)


def att_group(P, g, dilation):
    B, S, NP = P.shape
    rows = ATT_BLOCK * dilation
    nb = S // rows
    heads = ATT_GROUP_HEADS if dilation == 1 else 2
    width = heads * ATT_HEAD_DIM
    npairs = ATT_GROUP_HEADS // heads
    qo, ko, vo = (o // width + g * npairs for o in (OFF_ATT_Q, OFF_ATT_K, OFF_ATT_V))
    blk = (1, rows, width)
    if npairs == 1:
        grid, sem = (B, nb), ("parallel", "arbitrary")
        cur = lambda off: pl.BlockSpec(blk, lambda b, n: (b, n, off))
        prev = lambda off: pl.BlockSpec(blk, lambda b, n: (b, jnp.maximum(n - 1, 0), off))
        ospec = pl.BlockSpec(blk, lambda b, n: (b, n, 0))
    else:
        grid, sem = (B, npairs, nb), ("parallel", "parallel", "arbitrary")
        cur = lambda off: pl.BlockSpec(blk, lambda b, hp, n: (b, n, off + hp))
        prev = lambda off: pl.BlockSpec(blk, lambda b, hp, n: (b, jnp.maximum(n - 1, 0), off + hp))
        ospec = pl.BlockSpec(blk, lambda b, hp, n: (b, n, hp))
    oshape = jax.ShapeDtypeStruct((B, S, ATT_OUT_WIDTH), F32)
    slopes = ALIBI_SLOPES[g * ATT_GROUP_HEADS:(g + 1) * ATT_GROUP_HEADS]
    return pl.pallas_call(
        functools.partial(_att_kernel, slopes=slopes, dilation=dilation, heads=heads),
        out_shape=(oshape, oshape, oshape),
        grid=grid,
        in_specs=[cur(qo), cur(ko), prev(ko), cur(vo), prev(vo)],
        out_specs=(ospec, ospec, ospec),
        compiler_params=_cparams(sem),
        name=f"dilated_attention_g{g}",
    )(P, P, P, P, P)


def _att_merge_kernel(*refs):
    o_ref = refs[-1]
    nums, dens, ms = refs[0:3], refs[3:6], refs[6:9]
    mx = jnp.maximum(jnp.maximum(ms[0][0], ms[1][0]), ms[2][0])
    num = jnp.zeros_like(mx)
    den = jnp.zeros_like(mx)
    for g in range(3):
        e = jnp.exp(ms[g][0] - mx)
        num = num + nums[g][0] * e
        den = den + dens[g][0] * e
    o_ref[0] = num / den


def dilated_attention(P):
    B, S, _ = P.shape
    parts = [att_group(P, g, d) for g, (_, d) in enumerate(ATT_GROUPS)]
    args = [p[0] for p in parts] + [p[1] for p in parts] + [p[2] for p in parts]
    ts = 1024
    spec = pl.BlockSpec((1, ts, ATT_OUT_WIDTH), lambda b, i: (b, i, 0))
    return pl.pallas_call(
        _att_merge_kernel,
        out_shape=jax.ShapeDtypeStruct((B, S, ATT_OUT_WIDTH), F32),
        grid=(B, S // ts),
        in_specs=[spec] * 9,
        out_specs=spec,
        compiler_params=_cparams(("parallel", "parallel")),
        name="dilated_attention_merge",
    )(*args)


GLA_TS = 256


def _gla_kernel(q_ref, k_ref, v_ref, g_ref, a_ref, wa_ref, ba_ref, ng_ref, o_ref, st_ref):
    @pl.when(pl.program_id(1) == 0)
    def _():
        st_ref[...] = jnp.zeros_like(st_ref)

    TS = GLA_TS
    nchunk = TS // GLA_CHUNK
    logit = jnp.dot(a_ref[0], wa_ref[...], precision=HIGHEST, preferred_element_type=F32) + ba_ref[...]
    log_a = (jnp.minimum(logit, 0.0) - jnp.log1p(jnp.exp(-jnp.abs(logit)))) / GLA_GATE_TEMP
    i = lax.broadcasted_iota(jnp.int32, (TS, TS), 0)
    j = lax.broadcasted_iota(jnp.int32, (TS, TS), 1)
    same_chunk_causal = jnp.logical_and(i // GLA_CHUNK == j // GLA_CHUNK, j <= i)
    tri = same_chunk_causal.astype(F32)
    bcum = _dot_exact_lhs(tri, log_a)
    eb = jnp.exp(bcum)
    qd = q_ref[0] * (GLA_DK ** -0.5) * eb
    kd = k_ref[0] * jnp.exp(-bcum)
    k_all = k_ref[0]
    v_all = v_ref[0]
    dn_nt = (((1,), (1,)), ((), ()))
    dn_tn = (((0,), (0,)), ((), ()))
    heads = range(GLA_HEADS)
    ks = [slice(h * GLA_DK, (h + 1) * GLA_DK) for h in heads]
    vs = [slice(h * GLA_DV, (h + 1) * GLA_DV) for h in heads]
    qh = [qd[:, ks[h]].astype(BF16) for h in heads]
    vh = [v_all[:, vs[h]].astype(BF16) for h in heads]
    scores = [lax.dot_general(qh[h], kd[:, ks[h]].astype(BF16), dn_nt, preferred_element_type=F32) for h in heads]
    scores = [jnp.where(same_chunk_causal, sc, 0.0).astype(BF16) for sc in scores]
    o_intra = [jnp.dot(scores[h], vh[h], preferred_element_type=F32) for h in heads]
    st = [st_ref[h] for h in heads]
    outs = [[] for _ in heads]
    for c in range(nchunk):
        rs = slice(c * GLA_CHUNK, (c + 1) * GLA_CHUNK)
        last = (c + 1) * GLA_CHUNK - 1
        for h in heads:
            o_c = o_intra[h][rs] + lax.dot_general(qh[h][rs], st[h].astype(BF16), dn_nt,
                                                   preferred_element_type=F32)
            outs[h].append(o_c)
            b_last = bcum[last:last + 1, ks[h]]
            k_tail = (k_all[rs, ks[h]] * jnp.exp(b_last - bcum[rs, ks[h]])).astype(BF16)
            st[h] = st[h] * jnp.exp(b_last) + lax.dot_general(vh[h][rs], k_tail, dn_tn,
                                                              preferred_element_type=F32)
    for h in heads:
        st_ref[h] = st[h]
        o = jnp.concatenate(outs[h], axis=0)
        o = o * lax.rsqrt(jnp.mean(o * o, -1, keepdims=True) + GLA_NORM_EPS)
        gg = g_ref[0][:, vs[h]]
        o_ref[0, :, vs[h]] = o * ng_ref[:, vs[h]] * (gg * jax.nn.sigmoid(gg))


def gla_mixer(P, w_alpha, b_alpha, norm_g):
    B, S, _ = P.shape
    TS = GLA_TS
    wa = jnp.zeros((GLA_A_PAD, GLA_KEY_WIDTH), F32).at[:GLA_GATE_RANK].set(w_alpha)
    col = lambda off, w: pl.BlockSpec((1, TS, w), lambda b, i: (b, i, off // w))
    full = lambda r, c: pl.BlockSpec((r, c), lambda b, i: (0, 0))
    return pl.pallas_call(
        _gla_kernel,
        out_shape=jax.ShapeDtypeStruct((B, S, GLA_VAL_WIDTH), F32),
        grid=(B, S // TS),
        in_specs=[col(OFF_GLA_Q, GLA_KEY_WIDTH), col(OFF_GLA_K, GLA_KEY_WIDTH),
                  col(OFF_GLA_V, GLA_VAL_WIDTH), col(OFF_GLA_G, GLA_VAL_WIDTH),
                  col(OFF_GLA_A, GLA_A_PAD),
                  full(GLA_A_PAD, GLA_KEY_WIDTH), full(1, GLA_KEY_WIDTH), full(1, GLA_VAL_WIDTH)],
        out_specs=pl.BlockSpec((1, TS, GLA_VAL_WIDTH), lambda b, i: (b, i, 0)),
        scratch_shapes=[pltpu.VMEM((GLA_HEADS, GLA_DV, GLA_DK), F32)],
        compiler_params=_cparams(("parallel", "arbitrary")),
        name="gla_mixer",
    )(P, P, P, P, P, wa, b_alpha.reshape(1, -1), norm_g.reshape(1, -1))


RWKV_PREP_TS = 512
RWKV_SHIFT_LO = RWKV_DECAY_RANK + RWKV_AAA_RANK + RWKV_GATE_RANK


def _segment_sum(x, seg):
    n = x.shape[-1]
    i = lax.broadcasted_iota(jnp.int32, (n, n), 0)
    j = lax.broadcasted_iota(jnp.int32, (n, n), 1)
    ones = jnp.where(i // seg == j // seg, 1.0, 0.0).astype(BF16)
    hi, lo = _split_bf16(x)
    return jnp.dot(hi, ones, preferred_element_type=F32) + jnp.dot(lo, ones, preferred_element_type=F32)


def _rwkv_prep_kernel(rkv_ref, lo_ref, rkv_prev_ref, lo_prev_ref, mu_rkv_ref, mu_lo_ref, w0_ref, w2_ref,
                      a0_ref, a2_ref, g2_ref, kk_ref, ka_ref, rk_ref,
                      r_out, w_out, k_out, v_out, kk_out, b_out, gate_out, bonus_out):
    first = pl.program_id(1) == 0

    def shifted(cur, prev_ref, mu):
        prev_row = jnp.where(first, 0.0, prev_ref[0][SUBLANES - 1:SUBLANES, :])
        row = lax.broadcasted_iota(jnp.int32, cur.shape, 0)
        prev = jnp.where(row == 0, prev_row, pltpu.roll(cur, 1, 0))
        return cur + (prev - cur) * mu

    xs = shifted(rkv_ref[0], rkv_prev_ref, mu_rkv_ref[...])
    lo = shifted(lo_ref[0], lo_prev_ref, mu_lo_ref[...])
    W = RWKV_WIDTH
    r, k, v = xs[:, 0:W], xs[:, W:2 * W], xs[:, 2 * W:3 * W]
    z = w0_ref[...] + _dot3(jnp.tanh(lo), w2_ref[...])
    log_decay = -jax.nn.sigmoid(z) * float(np.exp(-0.5))
    a = jax.nn.sigmoid(a0_ref[...] + _dot3(lo, a2_ref[...]))
    gate = _dot3(jax.nn.sigmoid(lo), g2_ref[...])
    kk = k * kk_ref[...]
    kk_norm = jnp.sqrt(_segment_sum(kk * kk, RWKV_HEAD_DIM))
    kk = kk / jnp.maximum(kk_norm, 1e-12)
    k2 = k * (1.0 + (a - 1.0) * ka_ref[...])
    bonus = _segment_sum(r * k2 * rk_ref[...], RWKV_HEAD_DIM) * v
    r_out[0] = r
    w_out[0] = log_decay
    k_out[0] = k2
    v_out[0] = v
    kk_out[0] = kk
    b_out[0] = kk * a
    gate_out[0] = gate
    bonus_out[0] = bonus


def rwkv_prep(P, mu, w0, w2, a0, a2, g2, k_k, k_a, r_k):
    B, S, _ = P.shape
    TS = RWKV_PREP_TS
    W = RWKV_WIDTH
    W3 = 3 * W
    n_rkv = SRC_RWKV_RKV[1] - SRC_RWKV_RKV[0]
    mu_rkv = mu[:n_rkv].reshape(1, W3)
    mu_lo = jnp.zeros((1, RWKV_LO_PAD), F32).at[0, :RWKV_SHIFT_LO].set(mu[n_rkv:])
    w2p = jnp.zeros((RWKV_LO_PAD, W), F32).at[0:RWKV_DECAY_RANK].set(w2)
    a2p = jnp.zeros((RWKV_LO_PAD, W), F32).at[RWKV_DECAY_RANK:RWKV_DECAY_RANK + RWKV_AAA_RANK].set(a2)
    g2p = jnp.zeros((RWKV_LO_PAD, W), F32).at[RWKV_DECAY_RANK + RWKV_AAA_RANK:RWKV_SHIFT_LO].set(g2)
    row = lambda a: a.reshape(1, -1)
    full = lambda r, c: pl.BlockSpec((r, c), lambda b, i: (0, 0))
    tpb = TS // SUBLANES
    prev_map = lambda off, w: pl.BlockSpec((1, SUBLANES, w), lambda b, i: (b, jnp.maximum(i * tpb - 1, 0), off // w))
    oshape = jax.ShapeDtypeStruct((B, S, W), F32)
    ospec = pl.BlockSpec((1, TS, W), lambda b, i: (b, i, 0))
    return pl.pallas_call(
        _rwkv_prep_kernel,
        out_shape=(oshape,) * 8,
        grid=(B, S // TS),
        in_specs=[pl.BlockSpec((1, TS, W3), lambda b, i: (b, i, OFF_RWKV_RKV // W3)),
                  pl.BlockSpec((1, TS, RWKV_LO_PAD), lambda b, i: (b, i, OFF_RWKV_LO // RWKV_LO_PAD)),
                  prev_map(OFF_RWKV_RKV, W3), prev_map(OFF_RWKV_LO, RWKV_LO_PAD),
                  full(1, W3), full(1, RWKV_LO_PAD), full(1, W), full(RWKV_LO_PAD, W),
                  full(1, W), full(RWKV_LO_PAD, W), full(RWKV_LO_PAD, W),
                  full(1, W), full(1, W), full(1, W)],
        out_specs=(ospec,) * 8,
        compiler_params=_cparams(("parallel", "arbitrary")),
        name="rwkv7_prep",
    )(P, P, P, P, mu_rkv, mu_lo, row(w0), w2p, row(a0), a2p, g2p, row(k_k), row(k_a), row(r_k))


RWKV_CHUNK = 64
RWKV_CHUNK_TT = 512
RWKV_PAIRS_PER_STEP = 4


def _rwkv_chunk_kernel(r_ref, lw_ref, k_ref, v_ref, kk_ref, b_ref, y_ref, h_ref):
    @pl.when(pl.program_id(2) == 0)
    def _():
        h_ref[...] = jnp.zeros_like(h_ref)

    C = RWKV_CHUNK
    TT = RWKV_CHUNK_TT
    N = RWKV_HEAD_DIM
    R2 = 2 * C
    bdot = lambda x, y: jnp.dot(x.astype(BF16), y.astype(BF16), preferred_element_type=F32)
    r, lw, k, v, kk, b = (ref[0] for ref in (r_ref, lw_ref, k_ref, v_ref, kk_ref, b_ref))
    i = lax.broadcasted_iota(jnp.int32, (TT, TT), 0)
    j = lax.broadcasted_iota(jnp.int32, (TT, TT), 1)
    tri = jnp.logical_and(i // C == j // C, j <= i).astype(F32)
    cum = _dot_exact_lhs(tri, lw)
    lane_lo = lax.broadcasted_iota(jnp.int32, (C, LANES), 1) < N

    def stack(x):
        return jnp.concatenate([jnp.where(lane_lo, x, 0.0), jnp.where(lane_lo, 0.0, x)], axis=0)

    ti = lax.broadcasted_iota(jnp.int32, (R2, R2), 0)
    si = lax.broadcasted_iota(jnp.int32, (R2, R2), 1)
    strict = ti > si
    incl = ti >= si
    eye = (ti == si).astype(F32)
    blk16 = ti // 16 == si // 16
    off32 = jnp.logical_and(ti // 32 == si // 32, jnp.logical_not(blk16))
    off64 = jnp.logical_and(ti // 64 == si // 64, ti // 32 != si // 32)
    ones_c = jnp.ones((C, LANES), F32)

    items = [(c, p) for c in range(TT // C) for p in range(RWKV_PAIRS_PER_STEP)]
    each = lambda f, *lists: [f(*args) for args in zip(*lists)]
    lanes_of = lambda p: slice(p * LANES, (p + 1) * LANES)
    rows = [(slice(c * C, (c + 1) * C), lanes_of(p)) for c, p in items]
    cu = [cum[rw] for rw in rows]
    cu_last = [cum[(c + 1) * C - 1:(c + 1) * C, lanes_of(p)] for c, p in items]
    e_neg = each(lambda x: jnp.exp(-x), cu)
    e_tail = each(lambda x, xl: jnp.exp(xl - x), cu, cu_last)
    a_t = each(lambda rw, x: stack(-kk[rw] * jnp.exp(x - lw[rw])), rows, cu)
    r_t = each(lambda rw, x: stack(r[rw] * jnp.exp(x)), rows, cu)
    b_t = each(lambda rw, e: stack(b[rw] * e), rows, e_neg)
    k_t = each(lambda rw, e: stack(k[rw] * e), rows, e_neg)
    v_s = each(lambda rw: stack(v[rw]), rows)
    gram = each(lambda a_, r_, b_, k_: lax.dot_general(
        jnp.concatenate([a_, r_], 0).astype(BF16), jnp.concatenate([b_, k_], 0).astype(BF16),
        (((1,), (1,)), ((), ())), preferred_element_type=F32), a_t, r_t, b_t, k_t)
    n_ab = each(lambda g: jnp.where(strict, g[:R2, :R2], 0.0), gram)
    a_ak = each(lambda g: jnp.where(strict, g[:R2, R2:], 0.0), gram)
    a_rb = each(lambda g: jnp.where(incl, g[R2:, :R2], 0.0), gram)
    a_rk = each(lambda g: jnp.where(incl, g[R2:, R2:], 0.0), gram)
    av = each(lambda ak, rk, vv: bdot(jnp.concatenate([ak, rk], 0), vv), a_ak, a_rk, v_s)
    n1 = each(lambda n: jnp.where(blk16, n, 0.0), n_ab)
    n2 = each(lambda x: bdot(x, x), n1)
    n4 = each(lambda x: bdot(x, x), n2)
    n8 = each(lambda x: bdot(x, x), n4)
    xa = each(lambda x1, x2: x1 + x2 + bdot(x1, x2), n1, n2)
    xb = each(lambda x4, x8: x4 + x8 + bdot(x4, x8), n4, n8)
    t_inv = each(lambda p, q: eye + p + q + bdot(p, q), xa, xb)
    for off in (off32, off64):
        mid = each(lambda n, t: bdot(jnp.where(off, n, 0.0), t), n_ab, t_inv)
        t_inv = each(lambda t, m_: t + bdot(t, m_), t_inv, mid)
    tw = each(lambda t, a_, av_: bdot(t, jnp.concatenate([a_, av_[:R2]], axis=1)), t_inv, a_t, av)
    bk_t = each(lambda rw, e: jnp.transpose(jnp.concatenate([stack(b[rw] * e), stack(k[rw] * e)], 0)),
                rows, e_tail)
    pc_col = each(lambda rw: jnp.exp(jnp.dot(jnp.transpose(lw[rw]), ones_c, precision=HIGHEST,
                                             preferred_element_type=F32)), rows)
    w1r = each(lambda t, r_: jnp.concatenate([t[:, :LANES], r_], 0), tw, r_t)

    h = [h_ref[p] for p in range(RWKV_PAIRS_PER_STEP)]
    for it, (c, p) in enumerate(items):
        x = bdot(w1r[it], h[p])
        u = x[:R2] + tw[it][:, LANES:]
        y_bd = x[R2:] + bdot(a_rb[it], u) + av[it][R2:]
        y_ref[(0,) + rows[it]] = y_bd[:C] + y_bd[C:]
        h[p] = pc_col[it] * h[p] + bdot(bk_t[it], jnp.concatenate([u, v_s[it]], 0))
    for p in range(RWKV_PAIRS_PER_STEP):
        h_ref[p] = h[p]


def rwkv_chunked(r, lw, k, v, kk, b):
    B, S, W = r.shape
    TT = RWKV_CHUNK_TT
    lanes = RWKV_PAIRS_PER_STEP * LANES
    spec = pl.BlockSpec((1, TT, lanes), lambda bb, p, i: (bb, i, p))
    return pl.pallas_call(
        _rwkv_chunk_kernel,
        out_shape=jax.ShapeDtypeStruct((B, S, W), F32),
        grid=(B, W // lanes, S // TT),
        in_specs=[spec] * 6,
        out_specs=spec,
        scratch_shapes=[pltpu.VMEM((RWKV_PAIRS_PER_STEP, LANES, LANES), F32)],
        compiler_params=_cparams(("parallel", "parallel", "arbitrary")),
        name="rwkv7_chunked",
    )(r, lw, k, v, kk, b)


def _rwkv_post_kernel(y_ref, gate_ref, bonus_ref, g_ref, b_ref, o_ref):
    y = y_ref[0]
    mean = _segment_sum(y, RWKV_HEAD_DIM) * (1.0 / RWKV_HEAD_DIM)
    d = y - mean
    var = _segment_sum(d * d, RWKV_HEAD_DIM) * (1.0 / RWKV_HEAD_DIM)
    yn = d * lax.rsqrt(var + RWKV_GN_EPS) * g_ref[...] + b_ref[...]
    o_ref[0] = (yn + bonus_ref[0]) * gate_ref[0]


def rwkv_post(y, gate, bonus, ln_g, ln_b):
    B, S, W = y.shape
    ts = 512
    spec = pl.BlockSpec((1, ts, W), lambda b, i: (b, i, 0))
    full = pl.BlockSpec((1, W), lambda b, i: (0, 0))
    return pl.pallas_call(
        _rwkv_post_kernel,
        out_shape=jax.ShapeDtypeStruct((B, S, W), F32),
        grid=(B, S // ts),
        in_specs=[spec, spec, spec, full, full],
        out_specs=spec,
        compiler_params=_cparams(("parallel", "parallel")),
        name="rwkv7_post",
    )(y, gate, bonus, ln_g.reshape(1, W), ln_b.reshape(1, W))


def _merge_kernel(h_ref, wg0_ref, wg1_ref, wg2_ref, wg3_ref, ya_ref, yb_ref, yc_ref, yd_ref,
                  wa_ref, wb_ref, wc_ref, wd_ref, o_ref):
    h = h_ref[...]
    acc = None
    branches = ((wg0_ref, ya_ref, wa_ref), (wg1_ref, yb_ref, wb_ref), (wg2_ref, yc_ref, wc_ref),
                (wg3_ref, yd_ref, wd_ref))
    for wg_ref, y_ref, w_ref in branches:
        gate = jax.nn.sigmoid(jnp.dot(h, wg_ref[...], preferred_element_type=F32))
        proj = jnp.dot(y_ref[...].astype(BF16), w_ref[...], preferred_element_type=F32)
        acc = gate * proj if acc is None else acc + gate * proj
    o_ref[...] = acc.astype(o_ref.dtype)


def branch_merge(hb, wg, ys, ws):
    T, D = hb.shape
    tm, tn = 512, 512
    nj = D // tn
    gspecs = [pl.BlockSpec((D, tn), functools.partial(lambda j, i, br: (0, br * nj + j), br=br))
              for br in range(N_BRANCHES)]
    yspecs = [pl.BlockSpec((tm, y.shape[1]), lambda j, i: (i, 0)) for y in ys]
    wspecs = [pl.BlockSpec((w.shape[0], tn), lambda j, i: (0, j)) for w in ws]
    return pl.pallas_call(
        _merge_kernel,
        out_shape=jax.ShapeDtypeStruct((T, D), BF16),
        grid=(nj, T // tm),
        in_specs=[pl.BlockSpec((tm, D), lambda j, i: (i, 0))] + gspecs + yspecs + wspecs,
        out_specs=pl.BlockSpec((tm, tn), lambda j, i: (i, j)),
        compiler_params=_cparams(("parallel", "parallel")),
        name="branch_merge",
    )(hb, wg, wg, wg, wg, *ys, *ws)


ROUTE_TM = 256
PIECE = 16
TILE_USED_ROWS = ROUTE_TM * TOP_K + N_EXPERTS * (PIECE - 1)
TILE_ROWS = 1536
BLOCK_PIECES = MOE_BM // PIECE
DISPATCH_ROWS = 512


def _layer_norm(z, g, b):
    mu = jnp.mean(z, -1, keepdims=True)
    d = z - mu
    var = jnp.mean(d * d, -1, keepdims=True)
    return d * lax.rsqrt(var + LN_EPS) * g + b


def _out_ln_router_kernel(m_ref, wo_ref, x_ref, gate_ref, g_ref, b_ref, sc_ref, sh_ref, rwh_ref, rwl_ref, rb_ref,
                          x1_ref, xbt_ref, wt_ref, lrow_ref, cnt_ref):
    y = jnp.dot(m_ref[...], wo_ref[...], preferred_element_type=F32)
    x1 = _layer_norm(DEEPNORM_ALPHA * x_ref[...] + gate_ref[0] * y, g_ref[...], b_ref[...])
    x1_ref[...] = x1
    h2 = x1 * (1.0 + sc_ref[0]) + sh_ref[0]
    h2_hi, h2_lo = _split_bf16(h2)
    logits = (jnp.dot(h2_hi, rwh_ref[...], preferred_element_type=F32)
              + jnp.dot(h2_hi, rwl_ref[...], preferred_element_type=F32)
              + jnp.dot(h2_lo, rwh_ref[...], preferred_element_type=F32)) + rb_ref[...]
    lane = lax.broadcasted_iota(jnp.int32, logits.shape, 1).astype(F32)
    vals, idxs = [], []
    cur = logits
    for _ in range(TOP_K):
        m = jnp.max(cur, -1, keepdims=True)
        ix = jnp.min(jnp.where(cur == m, lane, float(LANES)), -1, keepdims=True)
        vals.append(m)
        idxs.append(ix)
        cur = jnp.where(lane == ix, -jnp.inf, cur)
    es = [jnp.exp(v - vals[0]) for v in vals]
    tot = es[0] + es[1] + es[2] + es[3]
    sel = jnp.zeros_like(logits)
    for kq in range(TOP_K):
        sel = jnp.where(lane == idxs[kq], 1.0, sel)
    tm = logits.shape[0]
    ti = lax.broadcasted_iota(jnp.int32, (tm, tm), 0)
    si = lax.broadcasted_iota(jnp.int32, (tm, tm), 1)
    before = jnp.dot((si < ti).astype(BF16), sel.astype(BF16), preferred_element_type=F32)
    cnt = jnp.sum(sel, axis=0, keepdims=True)
    cnt_ref[0] = jnp.broadcast_to(cnt, (SUBLANES, LANES)).astype(jnp.int32)
    pieces = jnp.floor((cnt + (PIECE - 1.0)) * (1.0 / PIECE))
    ei = lax.broadcasted_iota(jnp.int32, (LANES, LANES), 0)
    ej = lax.broadcasted_iota(jnp.int32, (LANES, LANES), 1)
    group_row = PIECE * jnp.dot(jnp.broadcast_to(pieces, (SUBLANES, LANES)).astype(BF16), (ei < ej).astype(BF16),
                                preferred_element_type=F32)[0:1]
    pos = before + group_row
    wt_o = jnp.zeros_like(logits)
    lrow_o = jnp.full(logits.shape, -1.0, F32)
    for kq in range(TOP_K):
        lrow_k = jnp.sum(jnp.where(lane == idxs[kq], pos, 0.0), -1, keepdims=True)
        wt_o = jnp.where(lane == kq, es[kq] / tot, wt_o)
        lrow_o = jnp.where(lane == kq, lrow_k, lrow_o)
    wt_ref[...] = wt_o
    lrow_ref[...] = lrow_o.astype(jnp.int32)
    lrow_t = jnp.transpose(lrow_o).astype(jnp.int32)
    for rc in range(TILE_ROWS // DISPATCH_ROWS):
        row = lax.broadcasted_iota(jnp.int32, (DISPATCH_ROWS, tm), 0) + rc * DISPATCH_ROWS
        hit = row == lrow_t[0:1, :]
        for kq in range(1, TOP_K):
            hit = jnp.logical_or(hit, row == lrow_t[kq:kq + 1, :])
        onehot = jnp.where(hit, 1.0, 0.0).astype(BF16)
        xbt_ref[pl.ds(rc * DISPATCH_ROWS, DISPATCH_ROWS), :] = jnp.dot(
            onehot, h2_hi, preferred_element_type=F32).astype(xbt_ref.dtype)


def out_ln_router(merged, w_out_b, x2d, gate1, ln_g, ln_b, scale2, shift2, router_w, router_b, S):
    T, D = x2d.shape
    tm = ROUTE_TM
    nt = T // tm
    spb = S // tm
    rw = jnp.zeros((D, LANES), F32).at[:, :N_EXPERTS].set(router_w)
    rw_hi, rw_lo = _split_bf16(rw)
    rb = jnp.full((1, LANES), NEG_BIG, F32).at[0, :N_EXPERTS].set(router_b)
    rowblk = lambda w: pl.BlockSpec((tm, w), lambda i: (i, 0))
    full = lambda r, c: pl.BlockSpec((r, c), lambda i: (0, 0))
    perb = pl.BlockSpec((1, 1, D), lambda i: (i // spb, 0, 0))
    tile_spec = pl.BlockSpec((1, SUBLANES, LANES), lambda i: (i, 0, 0))
    tile_shape = jax.ShapeDtypeStruct((nt, SUBLANES, LANES), jnp.int32)
    return pl.pallas_call(
        _out_ln_router_kernel,
        out_shape=(jax.ShapeDtypeStruct((T, D), F32), jax.ShapeDtypeStruct((nt * TILE_ROWS, D), BF16),
                   jax.ShapeDtypeStruct((T, LANES), F32), jax.ShapeDtypeStruct((T, LANES), jnp.int32), tile_shape),
        grid=(nt,),
        in_specs=[rowblk(D), full(D, D), rowblk(D), perb, full(1, D), full(1, D), perb, perb,
                  full(D, LANES), full(D, LANES), full(1, LANES)],
        out_specs=(rowblk(D), pl.BlockSpec((TILE_ROWS, D), lambda i: (i, 0)), rowblk(LANES), rowblk(LANES),
                   tile_spec),
        compiler_params=_cparams(("parallel",)),
        name="out_proj_ln_router",
    )(merged, w_out_b, x2d, gate1, ln_g.reshape(1, D), ln_b.reshape(1, D), scale2, shift2, rw_hi, rw_lo, rb)


W_CHUNK_ROWS = 512
W_STAGES = 3
W_NONE, W_FIRST_HALF, W_SECOND_HALF, W_ALL = 0, 1, 2, 3
GU_CHUNKS = D_MODEL // W_CHUNK_ROWS
DN_CHUNKS = EXPERT_FF // W_CHUNK_ROWS


def _expert_kernel(be_ref, nu_ref, src_ref, dst_ref, par_ref, nxt_ref, mode_ref, x_hbm, wgu_hbm, bgu_ref, wd_hbm, bd_ref,
                   y_hbm, xs_ref, ys_ref, wgu_b, wd_b, stg_ref, isem, osem, wsem, *, layer):
    i = pl.program_id(0)
    nu = nu_ref[0]
    slot = i % 2
    n_chunks = GU_CHUNKS + DN_CHUNKS

    def in_copy(blk, k, s):
        src = pl.multiple_of(src_ref[blk * BLOCK_PIECES + k], PIECE)
        return pltpu.make_async_copy(x_hbm.at[pl.ds(src, PIECE), :], xs_ref.at[s, pl.ds(k * PIECE, PIECE), :],
                                     isem.at[s])

    def out_copy(blk, k, s):
        dst = pl.multiple_of(dst_ref[blk * BLOCK_PIECES + k], PIECE)
        return pltpu.make_async_copy(ys_ref.at[s, pl.ds(k * PIECE, PIECE), :], y_hbm.at[pl.ds(dst, PIECE), :],
                                     osem.at[s])

    def all_pieces(fn):
        for k in range(BLOCK_PIECES):
            fn(k)

    def w_copy(e, c):
        s = c % W_STAGES
        if c < GU_CHUNKS:
            src = wgu_hbm.at[layer, e, pl.ds(c * W_CHUNK_ROWS, W_CHUNK_ROWS), :]
        else:
            src = wd_hbm.at[layer, e, pl.ds((c - GU_CHUNKS) * W_CHUNK_ROWS, W_CHUNK_ROWS), :]
        return pltpu.make_async_copy(src, stg_ref.at[s], wsem.at[s])

    def w_convert(c, p):
        w = stg_ref[c % W_STAGES].astype(BF16)
        if c < GU_CHUNKS:
            wgu_b[p, pl.ds(c * W_CHUNK_ROWS, W_CHUNK_ROWS), :] = w
        else:
            wd_b[p, pl.ds((c - GU_CHUNKS) * W_CHUNK_ROWS, W_CHUNK_ROWS), :] = w

    def w_finish(e, p, chunks, refill):
        for c in chunks:
            w_copy(e, c).wait()
            w_convert(c, p)
            if refill and c + W_STAGES < n_chunks:
                w_copy(e, c + W_STAGES).start()

    @pl.when(i == 0)
    def _():
        all_pieces(lambda k: in_copy(0, k, 0).start())
        e0 = be_ref[0]
        for c in range(W_STAGES):
            w_copy(e0, c).start()
        w_finish(e0, par_ref[0], range(n_chunks), True)

    @pl.when(i + 1 < nu)
    def _():
        all_pieces(lambda k: in_copy(i + 1, k, 1 - slot).start())

    @pl.when(i < nu)
    def _():
        p = par_ref[i]
        e_next = nxt_ref[i]
        mode = mode_ref[i]
        first_half = range(W_STAGES)
        second_half = range(W_STAGES, n_chunks)

        @pl.when(jnp.logical_or(mode == W_FIRST_HALF, mode == W_ALL))
        def _():
            for c in first_half:
                w_copy(e_next, c).start()

        @pl.when(mode == W_SECOND_HALF)
        def _():
            for c in second_half:
                w_copy(e_next, c).start()

        all_pieces(lambda k: in_copy(i, k, slot).wait())

        @pl.when(i >= 2)
        def _():
            all_pieces(lambda k: out_copy(i - 2, k, slot).wait())

        gu = jnp.dot(xs_ref[slot], wgu_b[p], preferred_element_type=F32) + bgu_ref[0]
        glu = jnp.minimum(gu[:, :EXPERT_FF], SWIGLU_LIMIT)
        lin = jnp.clip(gu[:, EXPERT_FF:], -SWIGLU_LIMIT, SWIGLU_LIMIT)
        act = glu * jax.nn.sigmoid(SWIGLU_ALPHA * glu) * (lin + 1.0)
        y = jnp.dot(act.astype(BF16), wd_b[p], preferred_element_type=F32) + bd_ref[0]
        ys_ref[slot] = y.astype(ys_ref.dtype)
        all_pieces(lambda k: out_copy(i, k, slot).start())

        @pl.when(mode == W_ALL)
        def _():
            w_finish(e_next, 1 - p, first_half, True)

        @pl.when(mode == W_FIRST_HALF)
        def _():
            w_finish(e_next, 1 - p, first_half, False)

        @pl.when(jnp.logical_or(mode == W_SECOND_HALF, mode == W_ALL))
        def _():
            w_finish(e_next, 1 - p, second_half, False)

    @pl.when(i == nu - 1)
    def _():
        @pl.when(i >= 1)
        def _():
            all_pieces(lambda k: out_copy(i - 1, k, 1 - slot).wait())
        all_pieces(lambda k: out_copy(i, k, slot).wait())


def expert_ffn(block_e, n_used, piece_src, piece_dst, w_parity, next_e, w_mode, xbt, wgu, bgu, wd, bd, layer):
    rows, D = xbt.shape
    _, E, _, F2 = wgu.shape
    nblk = block_e.shape[0]
    bias = lambda w: pl.BlockSpec((1, 1, w), lambda i, be, *_: (be[i], 0, 0))
    grid_spec = pltpu.PrefetchScalarGridSpec(
        num_scalar_prefetch=7,
        grid=(nblk,),
        in_specs=[pl.BlockSpec(memory_space=pl.ANY), pl.BlockSpec(memory_space=pl.ANY), bias(F2),
                  pl.BlockSpec(memory_space=pl.ANY), bias(D)],
        out_specs=pl.BlockSpec(memory_space=pl.ANY),
        scratch_shapes=[pltpu.VMEM((2, MOE_BM, D), BF16), pltpu.VMEM((2, MOE_BM, D), BF16),
                        pltpu.VMEM((2, D, F2), BF16), pltpu.VMEM((2, F2 // 2, D), BF16),
                        pltpu.VMEM((W_STAGES, W_CHUNK_ROWS, D), F32),
                        pltpu.SemaphoreType.DMA((2,)), pltpu.SemaphoreType.DMA((2,)),
                        pltpu.SemaphoreType.DMA((W_STAGES,))],
    )
    return pl.pallas_call(
        functools.partial(_expert_kernel, layer=layer),
        out_shape=jax.ShapeDtypeStruct((rows, D), BF16),
        grid_spec=grid_spec,
        input_output_aliases={7: 0},
        compiler_params=_cparams(("arbitrary",)),
        name="expert_ffn",
    )(block_e, n_used, piece_src, piece_dst, w_parity, next_e, w_mode, xbt, wgu, bgu.reshape(E, 1, F2), wd,
      bd.reshape(E, 1, D))


COMBINE_COLS = 512


def _combine_kernel(meta_ref, wt_ref, x1_ref, gate_ref, g_ref, b_ref, sc_ref, sh_ref, y_ref,
                    x2_ref, hb_ref, acc_ref):
    meta = meta_ref[...]
    wts = wt_ref[...]
    CW = COMBINE_COLS
    col0 = lax.broadcasted_iota(jnp.int32, (meta.shape[0], CW), 1)
    for cc in range(TILE_ROWS // CW):
        hi = jnp.zeros(col0.shape, F32)
        lo = jnp.zeros(col0.shape, F32)
        for kq in range(TOP_K):
            hit = col0 == meta[:, kq:kq + 1] - cc * CW
            w = wts[:, kq:kq + 1]
            w_hi = w.astype(BF16).astype(F32)
            hi = jnp.where(hit, w_hi, hi)
            lo = jnp.where(hit, w - w_hi, lo)
        rows = y_ref[pl.ds(cc * CW, CW), :]
        part = (jnp.dot(hi.astype(BF16), rows, preferred_element_type=F32)
                + jnp.dot(lo.astype(BF16), rows, preferred_element_type=F32))
        if cc == 0:
            acc_ref[...] = part
        else:
            acc_ref[...] += part

    x2 = _layer_norm(DEEPNORM_ALPHA * x1_ref[...] + gate_ref[0] * acc_ref[...], g_ref[...], b_ref[...])
    x2_ref[...] = x2
    hb_ref[...] = (x2 * (1.0 + sc_ref[0]) + sh_ref[0]).astype(hb_ref.dtype)


def moe_combine_ln(ybt, meta, wt, x1, gate2, ln_g, ln_b, scale_next, shift_next, S):
    T, D = x1.shape
    tm = ROUTE_TM
    spb = S // tm
    rowblk = lambda w: pl.BlockSpec((tm, w), lambda i: (i, 0))
    full = pl.BlockSpec((1, D), lambda i: (0, 0))
    perb = pl.BlockSpec((1, 1, D), lambda i: (i // spb, 0, 0))
    return pl.pallas_call(
        _combine_kernel,
        out_shape=(jax.ShapeDtypeStruct((T, D), F32), jax.ShapeDtypeStruct((T, D), BF16)),
        grid=(T // tm,),
        in_specs=[rowblk(LANES), rowblk(LANES), rowblk(D), perb, full, full, perb, perb,
                  pl.BlockSpec((TILE_ROWS, D), lambda i: (i, 0))],
        out_specs=(rowblk(D), rowblk(D)),
        scratch_shapes=[pltpu.VMEM((tm, D), F32)],
        compiler_params=_cparams(("parallel",)),
        name="moe_combine_ln",
    )(meta, wt, x1, gate2, ln_g.reshape(1, D), ln_b.reshape(1, D), scale_next, shift_next, ybt)


def routed_ffn_ln(xbt, lrow, wt, tile_cnt, wgu, bgu, wd, bd, layer, x1, gate2, ln_g, ln_b,
                  scale_next, shift_next, S):
    T, D = x1.shape
    E = N_EXPERTS
    nt = T // ROUTE_TM
    i32 = jnp.int32
    cnt = tile_cnt[:, 0, :E]
    npc = (cnt + PIECE - 1) // PIECE
    seg_off = (jnp.cumsum(npc, axis=1) - npc) * PIECE
    per_e = jnp.sum(npc, axis=0)
    per_e_pad = ((per_e + BLOCK_PIECES - 1) // BLOCK_PIECES) * BLOCK_PIECES
    e_end = jnp.cumsum(per_e_pad)
    e_start = e_end - per_e_pad
    t_end = jnp.cumsum(npc, axis=0).T
    t_start = t_end - npc.T
    max_pieces = nt * (TILE_USED_ROWS // PIECE) + E * (BLOCK_PIECES - 1)
    nblk = -(-max_pieces // BLOCK_PIECES)
    q = jnp.arange(nblk * BLOCK_PIECES, dtype=i32)
    e_q = jnp.minimum(jnp.sum((e_end[None, :] <= q[:, None]).astype(i32), axis=1), E - 1)
    sel_e = e_q[:, None] == jnp.arange(E, dtype=i32)[None, :]
    of_expert = lambda table: jnp.sum(jnp.where(sel_e, table[None, :], 0), axis=1)
    w_q = q - of_expert(e_start)
    real = w_q < of_expert(per_e)
    rows_of = lambda table: jnp.sum(jnp.where(sel_e[:, :, None], table[None], 0), axis=1)
    t_q = jnp.minimum(jnp.sum((rows_of(t_end) <= w_q[:, None]).astype(i32), axis=1), nt - 1)
    sel_t = t_q[:, None] == jnp.arange(nt, dtype=i32)[None, :]
    of_tile = lambda table: jnp.sum(jnp.where(sel_t, rows_of(table), 0), axis=1)
    k_q = w_q - of_tile(t_start)
    row_q = t_q * TILE_ROWS + of_tile(seg_off.T) + k_q * PIECE
    tail_pieces = (TILE_ROWS - TILE_USED_ROWS) // PIECE
    assert 2 * BLOCK_PIECES <= nt * tail_pieces, "not enough tile tail rows for the padding pieces"
    scratch_id = ((q // BLOCK_PIECES) % 2) * BLOCK_PIECES + q % BLOCK_PIECES
    scratch_q = (scratch_id // tail_pieces) * TILE_ROWS + TILE_USED_ROWS + (scratch_id % tail_pieces) * PIECE
    piece_src = jnp.where(real, row_q, 0).astype(i32)
    piece_dst = jnp.where(real, row_q, scratch_q).astype(i32)
    block_e = e_q[::BLOCK_PIECES]
    n_used = (e_end[-1] // BLOCK_PIECES).astype(i32)
    blk = jnp.arange(nblk, dtype=i32)
    nxt = jnp.concatenate([block_e[1:], block_e[-1:]])
    switch_next = jnp.logical_and(nxt != block_e, blk + 1 < n_used)
    w_parity = (jnp.cumsum(switch_next.astype(i32)) - switch_next.astype(i32)) % 2
    prv = jnp.concatenate([block_e[:1] - 1, block_e[:-1]])
    only_block = jnp.logical_and(switch_next, prv != block_e)
    before_last = jnp.logical_and(jnp.concatenate([switch_next[1:], switch_next[-1:] & False]), nxt == block_e)
    w_mode = jnp.where(only_block, W_ALL, jnp.where(switch_next, W_SECOND_HALF,
                                                    jnp.where(before_last, W_FIRST_HALF, W_NONE))).astype(i32)
    nxt2 = jnp.concatenate([block_e[2:], block_e[-1:], block_e[-1:]])
    next_e = jnp.where(switch_next, nxt, jnp.where(before_last, nxt2, 0)).astype(i32)

    ybt = expert_ffn(block_e, n_used.reshape(1), piece_src, piece_dst, w_parity, next_e, w_mode, xbt, wgu, bgu, wd,
                     bd, layer)
    return moe_combine_ln(ybt, lrow, wt, x1, gate2, ln_g, ln_b, scale_next, shift_next, S)


def _mix_weights(w_in_l):
    def cols(rng, pad=0):
        part = w_in_l[:, rng[0]:rng[1]]
        if pad:
            part = jnp.pad(part, ((0, 0), (0, pad)))
        return part
    w_mix = jnp.concatenate([
        cols(SRC_POOL), cols(SRC_GLA_V), cols(SRC_GLA_G), cols(SRC_RWKV_RKV), cols(SRC_ATT),
        cols(SRC_GLA_Q), cols(SRC_GLA_K),
        cols(SRC_RWKV_LO, RWKV_LO_PAD - (SRC_RWKV_LO[1] - SRC_RWKV_LO[0])),
        cols(SRC_GLA_A, GLA_A_PAD - (SRC_GLA_A[1] - SRC_GLA_A[0]))], axis=1).astype(BF16)
    w_gates = w_in_l[:, SRC_GATES[0]:].astype(BF16)
    return w_mix, w_gates


def kernel(x, c, ada_w, ada_b, w_in, pool_w, pool_scale, gla_w_alpha, gla_b_alpha, gla_norm_g, rwkv_mu, rwkv_w0, rwkv_w2, rwkv_a0, rwkv_a2, rwkv_g2, rwkv_k_k, rwkv_k_a, rwkv_r_k, rwkv_ln_g, rwkv_ln_b, w_branch_a, w_branch_b, w_branch_c, w_branch_d, w_out, ln1_g, ln1_b, router_w, router_b, w_gate_up, b_gate_up, w_down, b_down, ln2_g, ln2_b):
    B, S, D = x.shape
    T = B * S
    mod = ada_modulation(c, ada_w, ada_b)
    mods = [[mod[l, :, None, i * D:(i + 1) * D] for i in range(6)] for l in range(DEPTH)]
    x2d = x.reshape(T, D)
    hb2d = modulate(x, mods[0][1], mods[0][0]).reshape(T, D)
    for l in range(DEPTH):
        shift1, scale1, gate1, shift2, scale2, gate2 = mods[l]
        shift_next, scale_next = (mods[l + 1][0], mods[l + 1][1]) if l + 1 < DEPTH else (shift1, scale1)
        w_mix, w_gates = _mix_weights(w_in[l])
        P = matmul(hb2d, w_mix, 1024, MIX_TN).reshape(B, S, MIX_WIDTH)
        y_a = pool_mixer(P, pool_w[l], pool_scale[l])
        y_b = dilated_attention(P)
        y_c = gla_mixer(P, gla_w_alpha[l], gla_b_alpha[l], gla_norm_g[l])
        r_, w_, k_, v_, kk_, b_, gate_, bonus_ = rwkv_prep(
            P, rwkv_mu[l], rwkv_w0[l], rwkv_w2[l], rwkv_a0[l], rwkv_a2[l], rwkv_g2[l],
            rwkv_k_k[l], rwkv_k_a[l], rwkv_r_k[l].reshape(-1))
        y_t = rwkv_chunked(r_, w_, k_, v_, kk_, b_)
        y_d = rwkv_post(y_t, gate_, bonus_, rwkv_ln_g[l], rwkv_ln_b[l])
        ys = [y.reshape(T, -1) for y in (y_a, y_b, y_c, y_d)]
        ws = [w[l].astype(BF16) for w in (w_branch_a, w_branch_b, w_branch_c, w_branch_d)]
        merged = branch_merge(hb2d, w_gates, ys, ws)
        x1, xbt, wt, lrow, tile_cnt = out_ln_router(
            merged, w_out[l].astype(BF16), x2d, gate1, ln1_g[l], ln1_b[l], scale2, shift2,
            router_w[l], router_b[l], S)
        x2d, hb2d = routed_ffn_ln(xbt, lrow, wt, tile_cnt, w_gate_up, b_gate_up[l], w_down, b_down[l], l,
                                  x1, gate2, ln2_g[l], ln2_b[l], scale_next, shift_next, S)
    return x2d.reshape(B, S, D)
```

```python
import functools

import numpy as np
import jax
import jax.numpy as jnp
from jax import lax
from jax.experimental import pallas as pl
from jax.experimental.pallas import tpu as pltpu

F32 = jnp.float32
BF16 = jnp.bfloat16
HIGHEST = lax.Precision.HIGHEST

D_MODEL = 2048
DEPTH = 2
POOL_WINDOWS = (2, 4, 8, 16)
POOL_GROUP = 128
POOL_WIDTH = 512
ATT_GROUPS = ((128, 1), (512, 4), (2048, 16))
ATT_HEAD_DIM = 64
ATT_HEADS = 12
ATT_WIDTH = 768
ATT_OUT_WIDTH = 256
ATT_BLOCK = 128
ALIBI_SLOPES = tuple(2.0 ** (-8.0 * (h + 1) / ATT_HEADS) for h in range(ATT_HEADS))
GLA_HEADS = 4
GLA_DK = 64
GLA_DV = 128
GLA_KEY_WIDTH = 256
GLA_VAL_WIDTH = 512
GLA_GATE_RANK = 16
GLA_GATE_TEMP = 16.0
GLA_CHUNK = 32
GLA_NORM_EPS = 1e-6
RWKV_HEADS = 8
RWKV_HEAD_DIM = 64
RWKV_WIDTH = 512
RWKV_DECAY_RANK = 32
RWKV_AAA_RANK = 32
RWKV_GATE_RANK = 96
RWKV_GN_EPS = 64e-5
N_BRANCHES = 4
N_EXPERTS = 32
TOP_K = 4
EXPERT_FF = 1024
SWIGLU_LIMIT = 7.0
SWIGLU_ALPHA = 1.702
LN_EPS = 1e-5
DEEPNORM_ALPHA = (2 * DEPTH) ** 0.25

LANES = 128
SUBLANES = 8
VMEM_LIMIT = 56 * 1024 * 1024

SRC_POOL = (0, 512)
SRC_ATT = (512, 2816)
SRC_GLA_Q = (2816, 3072)
SRC_GLA_K = (3072, 3328)
SRC_GLA_V = (3328, 3840)
SRC_GLA_G = (3840, 4352)
SRC_GLA_A = (4352, 4368)
SRC_RWKV_RKV = (4368, 5904)
SRC_RWKV_LO = (5904, 6064)
SRC_GATES = (6064, 14256)
OFF_POOL = 0
OFF_GLA_V = 512
OFF_GLA_G = 1024
OFF_RWKV_RKV = 1536
OFF_ATT_Q = 3072
OFF_ATT_K = OFF_ATT_Q + ATT_WIDTH
OFF_ATT_V = OFF_ATT_K + ATT_WIDTH
OFF_GLA_Q = 5376
OFF_GLA_K = 5632
OFF_RWKV_LO = 5888
RWKV_LO_PAD = 256
OFF_GLA_A = 6144
GLA_A_PAD = 128
MIX_WIDTH = 6272
MIX_TN = 896

MOE_BM = 256
NEG_BIG = -1e30


def _cparams(sem):
    return pltpu.CompilerParams(dimension_semantics=sem, vmem_limit_bytes=VMEM_LIMIT)


def _split_bf16(x):
    hi = x.astype(BF16)
    return hi, (x - hi.astype(F32)).astype(BF16)


def _dot3(x, w):
    x_hi, x_lo = _split_bf16(x)
    w_hi, w_lo = _split_bf16(w)
    return (jnp.dot(x_hi, w_hi, preferred_element_type=F32) + jnp.dot(x_hi, w_lo, preferred_element_type=F32)
            + jnp.dot(x_lo, w_hi, preferred_element_type=F32))


def _dot_exact_lhs(m, x):
    hi = x.astype(BF16)
    r1 = x - hi.astype(F32)
    mid = r1.astype(BF16)
    lo = (r1 - mid.astype(F32)).astype(BF16)
    mb = m.astype(BF16)
    return (jnp.dot(mb, hi, preferred_element_type=F32) + jnp.dot(mb, mid, preferred_element_type=F32)
            + jnp.dot(mb, lo, preferred_element_type=F32))


def _ada_kernel(c_ref, w_ref, b_ref, o_ref):
    c = c_ref[...]
    s = c * jax.nn.sigmoid(c)
    o_ref[0] = _dot3(s, w_ref[0]) + b_ref[0]


def ada_modulation(c, ada_w, ada_b):
    L, D, N = ada_w.shape
    B = c.shape[0]
    cp = jnp.zeros((SUBLANES, D), F32).at[:B].set(c)
    tn = 1024
    out = pl.pallas_call(
        _ada_kernel,
        out_shape=jax.ShapeDtypeStruct((L, SUBLANES, N), F32),
        grid=(L, N // tn),
        in_specs=[pl.BlockSpec((SUBLANES, D), lambda l, j: (0, 0)),
                  pl.BlockSpec((1, D, tn), lambda l, j: (l, 0, j)),
                  pl.BlockSpec((1, 1, tn), lambda l, j: (l, 0, j))],
        out_specs=pl.BlockSpec((1, SUBLANES, tn), lambda l, j: (l, 0, j)),
        compiler_params=_cparams(("parallel", "parallel")),
        name="ada_modulation",
    )(cp, ada_w, ada_b.reshape(L, 1, N))
    return out[:, :B]


def _modulate_kernel(x_ref, sc_ref, sh_ref, o_ref):
    o_ref[0] = (x_ref[0] * (1.0 + sc_ref[0]) + sh_ref[0]).astype(o_ref.dtype)


def modulate(x, scale, shift):
    B, S, D = x.shape
    ts = 1024
    return pl.pallas_call(
        _modulate_kernel,
        out_shape=jax.ShapeDtypeStruct((B, S, D), BF16),
        grid=(B, S // ts),
        in_specs=[pl.BlockSpec((1, ts, D), lambda b, i: (b, i, 0)),
                  pl.BlockSpec((1, 1, D), lambda b, i: (b, 0, 0)),
                  pl.BlockSpec((1, 1, D), lambda b, i: (b, 0, 0))],
        out_specs=pl.BlockSpec((1, ts, D), lambda b, i: (b, i, 0)),
        compiler_params=_cparams(("parallel", "parallel")),
        name="modulate",
    )(x, scale, shift)


def _mm_kernel(x_ref, w_ref, o_ref):
    o_ref[...] = jnp.dot(x_ref[...], w_ref[...], preferred_element_type=F32).astype(o_ref.dtype)


def matmul(x, w, tm, tn, out_dtype=F32):
    M, K = x.shape
    N = w.shape[1]
    return pl.pallas_call(
        _mm_kernel,
        out_shape=jax.ShapeDtypeStruct((M, N), out_dtype),
        grid=(N // tn, M // tm),
        in_specs=[pl.BlockSpec((tm, K), lambda j, i: (i, 0)),
                  pl.BlockSpec((K, tn), lambda j, i: (0, j))],
        out_specs=pl.BlockSpec((tm, tn), lambda j, i: (i, j)),
        compiler_params=_cparams(("parallel", "parallel")),
        name="matmul",
    )(x, w)


def _pool_kernel(p_ref, w_ref, sc_ref, o_ref):
    g = pl.program_id(1)
    v = p_ref[0]
    S = v.shape[0]
    row = lax.broadcasted_iota(jnp.int32, v.shape, 0)
    win = jnp.left_shift(2, g)
    s = v
    pooled_sum = v
    for k, sh in enumerate((1, 2, 4, 8)):
        s = s + jnp.where(row >= sh, pltpu.roll(s, sh, 0), 0.0)
        pooled_sum = jnp.where(g >= k, s, pooled_sum)
    cnt = jnp.minimum(row + 1, win).astype(F32)
    diff = pooled_sum / cnt - v
    y = jnp.dot(diff.astype(BF16), w_ref[0].astype(BF16), preferred_element_type=F32)
    o_ref[0] = y * sc_ref[...]


def pool_mixer(P, pool_w, pool_scale):
    B, S, _ = P.shape
    G = len(POOL_WINDOWS)
    return pl.pallas_call(
        _pool_kernel,
        out_shape=jax.ShapeDtypeStruct((B, S, POOL_WIDTH), F32),
        grid=(B, G),
        in_specs=[pl.BlockSpec((1, S, POOL_GROUP), lambda b, g: (b, 0, OFF_POOL // POOL_GROUP + g)),
                  pl.BlockSpec((1, POOL_GROUP, POOL_GROUP), lambda b, g: (g, 0, 0)),
                  pl.BlockSpec((1, POOL_GROUP), lambda b, g: (0, g))],
        out_specs=pl.BlockSpec((1, S, POOL_GROUP), lambda b, g: (b, 0, g)),
        compiler_params=_cparams(("parallel", "parallel")),
        name="pool_mixer",
    )(P, pool_w, pool_scale.reshape(1, POOL_WIDTH))


ATT_GROUP_HEADS = 4
ATT_RESIDUE_UNROLL = 4


def _att_kernel(q_ref, kc_ref, kp_ref, vc_ref, vp_ref, num_ref, den_ref, m_ref, *, slopes, dilation, heads):
    if heads == ATT_GROUP_HEADS:
        n = pl.program_id(1)
    else:
        pair = pl.program_id(1)
        n = pl.program_id(2)
        slopes = [jnp.where(pair == 0, slopes[h], slopes[heads + h]) for h in range(heads)]
    i = lax.broadcasted_iota(jnp.int32, (ATT_BLOCK, ATT_BLOCK), 0)
    j = lax.broadcasted_iota(jnp.int32, (ATT_BLOCK, ATT_BLOCK), 1)
    dist_c = i - j
    dist_p = dist_c + ATT_BLOCK
    valid_c = dist_c >= 0
    valid_p = jnp.logical_and(dist_p <= ATT_BLOCK, n > 0)
    bias_c = [-(s * dilation) * dist_c.astype(F32) for s in slopes]
    bias_p = [-(s * dilation) * dist_p.astype(F32) for s in slopes]
    head_of_lane = lax.broadcasted_iota(jnp.int32, (ATT_BLOCK, heads * ATT_HEAD_DIM), 1) // ATT_HEAD_DIM
    dn = (((1,), (1,)), ((), ()))

    def per_head_lanes(cols):
        out = cols[-1]
        for h in range(heads - 2, -1, -1):
            out = jnp.where(head_of_lane == h, cols[h], out)
        return out

    def residue(r):
        rows = slice(None) if dilation == 1 else pl.ds(r, ATT_BLOCK, stride=dilation)
        q4, kc4, kp4, vc4, vp4 = (ref[0, rows, :] for ref in (q_ref, kc_ref, kp_ref, vc_ref, vp_ref))
        nums, dens, ms = [], [], []
        for h in range(heads):
            sl = slice(h * ATT_HEAD_DIM, (h + 1) * ATT_HEAD_DIM)
            q = (q4[:, sl] * (ATT_HEAD_DIM ** -0.5)).astype(BF16)
            s_c = lax.dot_general(q, kc4[:, sl].astype(BF16), dn, preferred_element_type=F32)
            s_p = lax.dot_general(q, kp4[:, sl].astype(BF16), dn, preferred_element_type=F32)
            s_c = jnp.where(valid_c, s_c + bias_c[h], NEG_BIG)
            s_p = jnp.where(valid_p, s_p + bias_p[h], NEG_BIG)
            m = jnp.maximum(jnp.max(s_c, -1, keepdims=True), jnp.max(s_p, -1, keepdims=True))
            p_c = jnp.exp(s_c - m)
            p_p = jnp.exp(s_p - m)
            dens.append(jnp.sum(p_c, -1, keepdims=True) + jnp.sum(p_p, -1, keepdims=True))
            nums.append(jnp.dot(p_c.astype(BF16), vc4[:, sl].astype(BF16), preferred_element_type=F32)
                        + jnp.dot(p_p.astype(BF16), vp4[:, sl].astype(BF16), preferred_element_type=F32))
            ms.append(m)
        num_ref[0, rows, :] = jnp.concatenate(nums, axis=-1)
        den_ref[0, rows, :] = per_head_lanes(dens)
        m_ref[0, rows, :] = per_head_lanes(ms)

    if dilation == 1:
        residue(0)
    else:
        def body(it, carry):
            for u in range(ATT_RESIDUE_UNROLL):
                residue(it * ATT_RESIDUE_UNROLL + u)
            return carry

        lax.fori_loop(0, dilation // ATT_RESIDUE_UNROLL, body, 0)


def att_group(P, g, dilation):
    B, S, NP = P.shape
    rows = ATT_BLOCK * dilation
    nb = S // rows
    heads = ATT_GROUP_HEADS if dilation == 1 else 2
    width = heads * ATT_HEAD_DIM
    npairs = ATT_GROUP_HEADS // heads
    qo, ko, vo = (o // width + g * npairs for o in (OFF_ATT_Q, OFF_ATT_K, OFF_ATT_V))
    blk = (1, rows, width)
    if npairs == 1:
        grid, sem = (B, nb), ("parallel", "arbitrary")
        cur = lambda off: pl.BlockSpec(blk, lambda b, n: (b, n, off))
        prev = lambda off: pl.BlockSpec(blk, lambda b, n: (b, jnp.maximum(n - 1, 0), off))
        ospec = pl.BlockSpec(blk, lambda b, n: (b, n, 0))
    else:
        grid, sem = (B, npairs, nb), ("parallel", "parallel", "arbitrary")
        cur = lambda off: pl.BlockSpec(blk, lambda b, hp, n: (b, n, off + hp))
        prev = lambda off: pl.BlockSpec(blk, lambda b, hp, n: (b, jnp.maximum(n - 1, 0), off + hp))
        ospec = pl.BlockSpec(blk, lambda b, hp, n: (b, n, hp))
    oshape = jax.ShapeDtypeStruct((B, S, ATT_OUT_WIDTH), F32)
    slopes = ALIBI_SLOPES[g * ATT_GROUP_HEADS:(g + 1) * ATT_GROUP_HEADS]
    return pl.pallas_call(
        functools.partial(_att_kernel, slopes=slopes, dilation=dilation, heads=heads),
        out_shape=(oshape, oshape, oshape),
        grid=grid,
        in_specs=[cur(qo), cur(ko), prev(ko), cur(vo), prev(vo)],
        out_specs=(ospec, ospec, ospec),
        compiler_params=_cparams(sem),
        name=f"dilated_attention_g{g}",
    )(P, P, P, P, P)


def _att_merge_kernel(*refs):
    o_ref = refs[-1]
    nums, dens, ms = refs[0:3], refs[3:6], refs[6:9]
    mx = jnp.maximum(jnp.maximum(ms[0][0], ms[1][0]), ms[2][0])
    num = jnp.zeros_like(mx)
    den = jnp.zeros_like(mx)
    for g in range(3):
        e = jnp.exp(ms[g][0] - mx)
        num = num + nums[g][0] * e
        den = den + dens[g][0] * e
    o_ref[0] = num / den


def dilated_attention(P):
    B, S, _ = P.shape
    parts = [att_group(P, g, d) for g, (_, d) in enumerate(ATT_GROUPS)]
    args = [p[0] for p in parts] + [p[1] for p in parts] + [p[2] for p in parts]
    ts = 1024
    spec = pl.BlockSpec((1, ts, ATT_OUT_WIDTH), lambda b, i: (b, i, 0))
    return pl.pallas_call(
        _att_merge_kernel,
        out_shape=jax.ShapeDtypeStruct((B, S, ATT_OUT_WIDTH), F32),
        grid=(B, S // ts),
        in_specs=[spec] * 9,
        out_specs=spec,
        compiler_params=_cparams(("parallel", "parallel")),
        name="dilated_attention_merge",
    )(*args)


GLA_TS = 256


def _gla_kernel(q_ref, k_ref, v_ref, g_ref, a_ref, wa_ref, ba_ref, ng_ref, o_ref, st_ref):
    @pl.when(pl.program_id(1) == 0)
    def _():
        st_ref[...] = jnp.zeros_like(st_ref)

    TS = GLA_TS
    nchunk = TS // GLA_CHUNK
    logit = jnp.dot(a_ref[0], wa_ref[...], precision=HIGHEST, preferred_element_type=F32) + ba_ref[...]
    log_a = (jnp.minimum(logit, 0.0) - jnp.log1p(jnp.exp(-jnp.abs(logit)))) / GLA_GATE_TEMP
    i = lax.broadcasted_iota(jnp.int32, (TS, TS), 0)
    j = lax.broadcasted_iota(jnp.int32, (TS, TS), 1)
    same_chunk_causal = jnp.logical_and(i // GLA_CHUNK == j // GLA_CHUNK, j <= i)
    tri = same_chunk_causal.astype(F32)
    bcum = _dot_exact_lhs(tri, log_a)
    eb = jnp.exp(bcum)
    qd = q_ref[0] * (GLA_DK ** -0.5) * eb
    kd = k_ref[0] * jnp.exp(-bcum)
    k_all = k_ref[0]
    v_all = v_ref[0]
    dn_nt = (((1,), (1,)), ((), ()))
    dn_tn = (((0,), (0,)), ((), ()))
    heads = range(GLA_HEADS)
    ks = [slice(h * GLA_DK, (h + 1) * GLA_DK) for h in heads]
    vs = [slice(h * GLA_DV, (h + 1) * GLA_DV) for h in heads]
    qh = [qd[:, ks[h]].astype(BF16) for h in heads]
    vh = [v_all[:, vs[h]].astype(BF16) for h in heads]
    scores = [lax.dot_general(qh[h], kd[:, ks[h]].astype(BF16), dn_nt, preferred_element_type=F32) for h in heads]
    scores = [jnp.where(same_chunk_causal, sc, 0.0).astype(BF16) for sc in scores]
    o_intra = [jnp.dot(scores[h], vh[h], preferred_element_type=F32) for h in heads]
    st = [st_ref[h] for h in heads]
    outs = [[] for _ in heads]
    for c in range(nchunk):
        rs = slice(c * GLA_CHUNK, (c + 1) * GLA_CHUNK)
        last = (c + 1) * GLA_CHUNK - 1
        for h in heads:
            o_c = o_intra[h][rs] + lax.dot_general(qh[h][rs], st[h].astype(BF16), dn_nt,
                                                   preferred_element_type=F32)
            outs[h].append(o_c)
            b_last = bcum[last:last + 1, ks[h]]
            k_tail = (k_all[rs, ks[h]] * jnp.exp(b_last - bcum[rs, ks[h]])).astype(BF16)
            st[h] = st[h] * jnp.exp(b_last) + lax.dot_general(vh[h][rs], k_tail, dn_tn,
                                                              preferred_element_type=F32)
    for h in heads:
        st_ref[h] = st[h]
        o = jnp.concatenate(outs[h], axis=0)
        o = o * lax.rsqrt(jnp.mean(o * o, -1, keepdims=True) + GLA_NORM_EPS)
        gg = g_ref[0][:, vs[h]]
        o_ref[0, :, vs[h]] = o * ng_ref[:, vs[h]] * (gg * jax.nn.sigmoid(gg))


def gla_mixer(P, w_alpha, b_alpha, norm_g):
    B, S, _ = P.shape
    TS = GLA_TS
    wa = jnp.zeros((GLA_A_PAD, GLA_KEY_WIDTH), F32).at[:GLA_GATE_RANK].set(w_alpha)
    col = lambda off, w: pl.BlockSpec((1, TS, w), lambda b, i: (b, i, off // w))
    full = lambda r, c: pl.BlockSpec((r, c), lambda b, i: (0, 0))
    return pl.pallas_call(
        _gla_kernel,
        out_shape=jax.ShapeDtypeStruct((B, S, GLA_VAL_WIDTH), F32),
        grid=(B, S // TS),
        in_specs=[col(OFF_GLA_Q, GLA_KEY_WIDTH), col(OFF_GLA_K, GLA_KEY_WIDTH),
                  col(OFF_GLA_V, GLA_VAL_WIDTH), col(OFF_GLA_G, GLA_VAL_WIDTH),
                  col(OFF_GLA_A, GLA_A_PAD),
                  full(GLA_A_PAD, GLA_KEY_WIDTH), full(1, GLA_KEY_WIDTH), full(1, GLA_VAL_WIDTH)],
        out_specs=pl.BlockSpec((1, TS, GLA_VAL_WIDTH), lambda b, i: (b, i, 0)),
        scratch_shapes=[pltpu.VMEM((GLA_HEADS, GLA_DV, GLA_DK), F32)],
        compiler_params=_cparams(("parallel", "arbitrary")),
        name="gla_mixer",
    )(P, P, P, P, P, wa, b_alpha.reshape(1, -1), norm_g.reshape(1, -1))


RWKV_PREP_TS = 512
RWKV_SHIFT_LO = RWKV_DECAY_RANK + RWKV_AAA_RANK + RWKV_GATE_RANK


def _segment_sum(x, seg):
    n = x.shape[-1]
    i = lax.broadcasted_iota(jnp.int32, (n, n), 0)
    j = lax.broadcasted_iota(jnp.int32, (n, n), 1)
    ones = jnp.where(i // seg == j // seg, 1.0, 0.0).astype(BF16)
    hi, lo = _split_bf16(x)
    return jnp.dot(hi, ones, preferred_element_type=F32) + jnp.dot(lo, ones, preferred_element_type=F32)


def _rwkv_prep_kernel(rkv_ref, lo_ref, rkv_prev_ref, lo_prev_ref, mu_rkv_ref, mu_lo_ref, w0_ref, w2_ref,
                      a0_ref, a2_ref, g2_ref, kk_ref, ka_ref, rk_ref,
                      r_out, w_out, k_out, v_out, kk_out, b_out, gate_out, bonus_out):
    first = pl.program_id(1) == 0

    def shifted(cur, prev_ref, mu):
        prev_row = jnp.where(first, 0.0, prev_ref[0][SUBLANES - 1:SUBLANES, :])
        row = lax.broadcasted_iota(jnp.int32, cur.shape, 0)
        prev = jnp.where(row == 0, prev_row, pltpu.roll(cur, 1, 0))
        return cur + (prev - cur) * mu

    xs = shifted(rkv_ref[0], rkv_prev_ref, mu_rkv_ref[...])
    lo = shifted(lo_ref[0], lo_prev_ref, mu_lo_ref[...])
    W = RWKV_WIDTH
    r, k, v = xs[:, 0:W], xs[:, W:2 * W], xs[:, 2 * W:3 * W]
    z = w0_ref[...] + _dot3(jnp.tanh(lo), w2_ref[...])
    log_decay = -jax.nn.sigmoid(z) * float(np.exp(-0.5))
    a = jax.nn.sigmoid(a0_ref[...] + _dot3(lo, a2_ref[...]))
    gate = _dot3(jax.nn.sigmoid(lo), g2_ref[...])
    kk = k * kk_ref[...]
    kk_norm = jnp.sqrt(_segment_sum(kk * kk, RWKV_HEAD_DIM))
    kk = kk / jnp.maximum(kk_norm, 1e-12)
    k2 = k * (1.0 + (a - 1.0) * ka_ref[...])
    bonus = _segment_sum(r * k2 * rk_ref[...], RWKV_HEAD_DIM) * v
    r_out[0] = r
    w_out[0] = log_decay
    k_out[0] = k2
    v_out[0] = v
    kk_out[0] = kk
    b_out[0] = kk * a
    gate_out[0] = gate
    bonus_out[0] = bonus


def rwkv_prep(P, mu, w0, w2, a0, a2, g2, k_k, k_a, r_k):
    B, S, _ = P.shape
    TS = RWKV_PREP_TS
    W = RWKV_WIDTH
    W3 = 3 * W
    n_rkv = SRC_RWKV_RKV[1] - SRC_RWKV_RKV[0]
    mu_rkv = mu[:n_rkv].reshape(1, W3)
    mu_lo = jnp.zeros((1, RWKV_LO_PAD), F32).at[0, :RWKV_SHIFT_LO].set(mu[n_rkv:])
    w2p = jnp.zeros((RWKV_LO_PAD, W), F32).at[0:RWKV_DECAY_RANK].set(w2)
    a2p = jnp.zeros((RWKV_LO_PAD, W), F32).at[RWKV_DECAY_RANK:RWKV_DECAY_RANK + RWKV_AAA_RANK].set(a2)
    g2p = jnp.zeros((RWKV_LO_PAD, W), F32).at[RWKV_DECAY_RANK + RWKV_AAA_RANK:RWKV_SHIFT_LO].set(g2)
    row = lambda a: a.reshape(1, -1)
    full = lambda r, c: pl.BlockSpec((r, c), lambda b, i: (0, 0))
    tpb = TS // SUBLANES
    prev_map = lambda off, w: pl.BlockSpec((1, SUBLANES, w), lambda b, i: (b, jnp.maximum(i * tpb - 1, 0), off // w))
    oshape = jax.ShapeDtypeStruct((B, S, W), F32)
    ospec = pl.BlockSpec((1, TS, W), lambda b, i: (b, i, 0))
    return pl.pallas_call(
        _rwkv_prep_kernel,
        out_shape=(oshape,) * 8,
        grid=(B, S // TS),
        in_specs=[pl.BlockSpec((1, TS, W3), lambda b, i: (b, i, OFF_RWKV_RKV // W3)),
                  pl.BlockSpec((1, TS, RWKV_LO_PAD), lambda b, i: (b, i, OFF_RWKV_LO // RWKV_LO_PAD)),
                  prev_map(OFF_RWKV_RKV, W3), prev_map(OFF_RWKV_LO, RWKV_LO_PAD),
                  full(1, W3), full(1, RWKV_LO_PAD), full(1, W), full(RWKV_LO_PAD, W),
                  full(1, W), full(RWKV_LO_PAD, W), full(RWKV_LO_PAD, W),
                  full(1, W), full(1, W), full(1, W)],
        out_specs=(ospec,) * 8,
        compiler_params=_cparams(("parallel", "arbitrary")),
        name="rwkv7_prep",
    )(P, P, P, P, mu_rkv, mu_lo, row(w0), w2p, row(a0), a2p, g2p, row(k_k), row(k_a), row(r_k))


RWKV_CHUNK = 64
RWKV_CHUNK_TT = 512
RWKV_PAIRS_PER_STEP = 4


def _rwkv_chunk_kernel(r_ref, lw_ref, k_ref, v_ref, kk_ref, b_ref, y_ref, h_ref):
    @pl.when(pl.program_id(2) == 0)
    def _():
        h_ref[...] = jnp.zeros_like(h_ref)

    C = RWKV_CHUNK
    TT = RWKV_CHUNK_TT
    N = RWKV_HEAD_DIM
    R2 = 2 * C
    bdot = lambda x, y: jnp.dot(x.astype(BF16), y.astype(BF16), preferred_element_type=F32)
    r, lw, k, v, kk, b = (ref[0] for ref in (r_ref, lw_ref, k_ref, v_ref, kk_ref, b_ref))
    i = lax.broadcasted_iota(jnp.int32, (TT, TT), 0)
    j = lax.broadcasted_iota(jnp.int32, (TT, TT), 1)
    tri = jnp.logical_and(i // C == j // C, j <= i).astype(F32)
    cum = _dot_exact_lhs(tri, lw)
    lane_lo = lax.broadcasted_iota(jnp.int32, (C, LANES), 1) < N

    def stack(x):
        return jnp.concatenate([jnp.where(lane_lo, x, 0.0), jnp.where(lane_lo, 0.0, x)], axis=0)

    ti = lax.broadcasted_iota(jnp.int32, (R2, R2), 0)
    si = lax.broadcasted_iota(jnp.int32, (R2, R2), 1)
    strict = ti > si
    incl = ti >= si
    eye = (ti == si).astype(F32)
    blk16 = ti // 16 == si // 16
    off32 = jnp.logical_and(ti // 32 == si // 32, jnp.logical_not(blk16))
    off64 = jnp.logical_and(ti // 64 == si // 64, ti // 32 != si // 32)
    ones_c = jnp.ones((C, LANES), F32)

    items = [(c, p) for c in range(TT // C) for p in range(RWKV_PAIRS_PER_STEP)]
    each = lambda f, *lists: [f(*args) for args in zip(*lists)]
    lanes_of = lambda p: slice(p * LANES, (p + 1) * LANES)
    rows = [(slice(c * C, (c + 1) * C), lanes_of(p)) for c, p in items]
    cu = [cum[rw] for rw in rows]
    cu_last = [cum[(c + 1) * C - 1:(c + 1) * C, lanes_of(p)] for c, p in items]
    e_neg = each(lambda x: jnp.exp(-x), cu)
    e_tail = each(lambda x, xl: jnp.exp(xl - x), cu, cu_last)
    a_t = each(lambda rw, x: stack(-kk[rw] * jnp.exp(x - lw[rw])), rows, cu)
    r_t = each(lambda rw, x: stack(r[rw] * jnp.exp(x)), rows, cu)
    b_t = each(lambda rw, e: stack(b[rw] * e), rows, e_neg)
    k_t = each(lambda rw, e: stack(k[rw] * e), rows, e_neg)
    v_s = each(lambda rw: stack(v[rw]), rows)
    gram = each(lambda a_, r_, b_, k_: lax.dot_general(
        jnp.concatenate([a_, r_], 0).astype(BF16), jnp.concatenate([b_, k_], 0).astype(BF16),
        (((1,), (1,)), ((), ())), preferred_element_type=F32), a_t, r_t, b_t, k_t)
    n_ab = each(lambda g: jnp.where(strict, g[:R2, :R2], 0.0), gram)
    a_ak = each(lambda g: jnp.where(strict, g[:R2, R2:], 0.0), gram)
    a_rb = each(lambda g: jnp.where(incl, g[R2:, :R2], 0.0), gram)
    a_rk = each(lambda g: jnp.where(incl, g[R2:, R2:], 0.0), gram)
    av = each(lambda ak, rk, vv: bdot(jnp.concatenate([ak, rk], 0), vv), a_ak, a_rk, v_s)
    n1 = each(lambda n: jnp.where(blk16, n, 0.0), n_ab)
    n2 = each(lambda x: bdot(x, x), n1)
    n4 = each(lambda x: bdot(x, x), n2)
    n8 = each(lambda x: bdot(x, x), n4)
    xa = each(lambda x1, x2: x1 + x2 + bdot(x1, x2), n1, n2)
    xb = each(lambda x4, x8: x4 + x8 + bdot(x4, x8), n4, n8)
    t_inv = each(lambda p, q: eye + p + q + bdot(p, q), xa, xb)
    for off in (off32, off64):
        mid = each(lambda n, t: bdot(jnp.where(off, n, 0.0), t), n_ab, t_inv)
        t_inv = each(lambda t, m_: t + bdot(t, m_), t_inv, mid)
    tw = each(lambda t, a_, av_: bdot(t, jnp.concatenate([a_, av_[:R2]], axis=1)), t_inv, a_t, av)
    bk_t = each(lambda rw, e: jnp.transpose(jnp.concatenate([stack(b[rw] * e), stack(k[rw] * e)], 0)),
                rows, e_tail)
    pc_col = each(lambda rw: jnp.exp(jnp.dot(jnp.transpose(lw[rw]), ones_c, precision=HIGHEST,
                                             preferred_element_type=F32)), rows)
    w1r = each(lambda t, r_: jnp.concatenate([t[:, :LANES], r_], 0), tw, r_t)

    h = [h_ref[p] for p in range(RWKV_PAIRS_PER_STEP)]
    for it, (c, p) in enumerate(items):
        x = bdot(w1r[it], h[p])
        u = x[:R2] + tw[it][:, LANES:]
        y_bd = x[R2:] + bdot(a_rb[it], u) + av[it][R2:]
        y_ref[(0,) + rows[it]] = y_bd[:C] + y_bd[C:]
        h[p] = pc_col[it] * h[p] + bdot(bk_t[it], jnp.concatenate([u, v_s[it]], 0))
    for p in range(RWKV_PAIRS_PER_STEP):
        h_ref[p] = h[p]


def rwkv_chunked(r, lw, k, v, kk, b):
    B, S, W = r.shape
    TT = RWKV_CHUNK_TT
    lanes = RWKV_PAIRS_PER_STEP * LANES
    spec = pl.BlockSpec((1, TT, lanes), lambda bb, p, i: (bb, i, p))
    return pl.pallas_call(
        _rwkv_chunk_kernel,
        out_shape=jax.ShapeDtypeStruct((B, S, W), F32),
        grid=(B, W // lanes, S // TT),
        in_specs=[spec] * 6,
        out_specs=spec,
        scratch_shapes=[pltpu.VMEM((RWKV_PAIRS_PER_STEP, LANES, LANES), F32)],
        compiler_params=_cparams(("parallel", "parallel", "arbitrary")),
        name="rwkv7_chunked",
    )(r, lw, k, v, kk, b)


def _rwkv_post_kernel(y_ref, gate_ref, bonus_ref, g_ref, b_ref, o_ref):
    y = y_ref[0]
    mean = _segment_sum(y, RWKV_HEAD_DIM) * (1.0 / RWKV_HEAD_DIM)
    d = y - mean
    var = _segment_sum(d * d, RWKV_HEAD_DIM) * (1.0 / RWKV_HEAD_DIM)
    yn = d * lax.rsqrt(var + RWKV_GN_EPS) * g_ref[...] + b_ref[...]
    o_ref[0] = (yn + bonus_ref[0]) * gate_ref[0]


def rwkv_post(y, gate, bonus, ln_g, ln_b):
    B, S, W = y.shape
    ts = 512
    spec = pl.BlockSpec((1, ts, W), lambda b, i: (b, i, 0))
    full = pl.BlockSpec((1, W), lambda b, i: (0, 0))
    return pl.pallas_call(
        _rwkv_post_kernel,
        out_shape=jax.ShapeDtypeStruct((B, S, W), F32),
        grid=(B, S // ts),
        in_specs=[spec, spec, spec, full, full],
        out_specs=spec,
        compiler_params=_cparams(("parallel", "parallel")),
        name="rwkv7_post",
    )(y, gate, bonus, ln_g.reshape(1, W), ln_b.reshape(1, W))


def _merge_kernel(h_ref, wg0_ref, wg1_ref, wg2_ref, wg3_ref, ya_ref, yb_ref, yc_ref, yd_ref,
                  wa_ref, wb_ref, wc_ref, wd_ref, o_ref):
    h = h_ref[...]
    acc = None
    branches = ((wg0_ref, ya_ref, wa_ref), (wg1_ref, yb_ref, wb_ref), (wg2_ref, yc_ref, wc_ref),
                (wg3_ref, yd_ref, wd_ref))
    for wg_ref, y_ref, w_ref in branches:
        gate = jax.nn.sigmoid(jnp.dot(h, wg_ref[...], preferred_element_type=F32))
        proj = jnp.dot(y_ref[...].astype(BF16), w_ref[...], preferred_element_type=F32)
        acc = gate * proj if acc is None else acc + gate * proj
    o_ref[...] = acc.astype(o_ref.dtype)


def branch_merge(hb, wg, ys, ws):
    T, D = hb.shape
    tm, tn = 512, 512
    nj = D // tn
    gspecs = [pl.BlockSpec((D, tn), functools.partial(lambda j, i, br: (0, br * nj + j), br=br))
              for br in range(N_BRANCHES)]
    yspecs = [pl.BlockSpec((tm, y.shape[1]), lambda j, i: (i, 0)) for y in ys]
    wspecs = [pl.BlockSpec((w.shape[0], tn), lambda j, i: (0, j)) for w in ws]
    return pl.pallas_call(
        _merge_kernel,
        out_shape=jax.ShapeDtypeStruct((T, D), BF16),
        grid=(nj, T // tm),
        in_specs=[pl.BlockSpec((tm, D), lambda j, i: (i, 0))] + gspecs + yspecs + wspecs,
        out_specs=pl.BlockSpec((tm, tn), lambda j, i: (i, j)),
        compiler_params=_cparams(("parallel", "parallel")),
        name="branch_merge",
    )(hb, wg, wg, wg, wg, *ys, *ws)


ROUTE_TM = 256
PIECE = 16
TILE_USED_ROWS = ROUTE_TM * TOP_K + N_EXPERTS * (PIECE - 1)
TILE_ROWS = 1536
BLOCK_PIECES = MOE_BM // PIECE
DISPATCH_ROWS = 512


def _layer_norm(z, g, b):
    mu = jnp.mean(z, -1, keepdims=True)
    d = z - mu
    var = jnp.mean(d * d, -1, keepdims=True)
    return d * lax.rsqrt(var + LN_EPS) * g + b


def _out_ln_router_kernel(m_ref, wo_ref, x_ref, gate_ref, g_ref, b_ref, sc_ref, sh_ref, rwh_ref, rwl_ref, rb_ref,
                          x1_ref, xbt_ref, wt_ref, lrow_ref, cnt_ref):
    y = jnp.dot(m_ref[...], wo_ref[...], preferred_element_type=F32)
    x1 = _layer_norm(DEEPNORM_ALPHA * x_ref[...] + gate_ref[0] * y, g_ref[...], b_ref[...])
    x1_ref[...] = x1
    h2 = x1 * (1.0 + sc_ref[0]) + sh_ref[0]
    h2_hi, h2_lo = _split_bf16(h2)
    logits = (jnp.dot(h2_hi, rwh_ref[...], preferred_element_type=F32)
              + jnp.dot(h2_hi, rwl_ref[...], preferred_element_type=F32)
              + jnp.dot(h2_lo, rwh_ref[...], preferred_element_type=F32)) + rb_ref[...]
    lane = lax.broadcasted_iota(jnp.int32, logits.shape, 1).astype(F32)
    vals, idxs = [], []
    cur = logits
    for _ in range(TOP_K):
        m = jnp.max(cur, -1, keepdims=True)
        ix = jnp.min(jnp.where(cur == m, lane, float(LANES)), -1, keepdims=True)
        vals.append(m)
        idxs.append(ix)
        cur = jnp.where(lane == ix, -jnp.inf, cur)
    es = [jnp.exp(v - vals[0]) for v in vals]
    tot = es[0] + es[1] + es[2] + es[3]
    sel = jnp.zeros_like(logits)
    for kq in range(TOP_K):
        sel = jnp.where(lane == idxs[kq], 1.0, sel)
    tm = logits.shape[0]
    ti = lax.broadcasted_iota(jnp.int32, (tm, tm), 0)
    si = lax.broadcasted_iota(jnp.int32, (tm, tm), 1)
    before = jnp.dot((si < ti).astype(BF16), sel.astype(BF16), preferred_element_type=F32)
    cnt = jnp.sum(sel, axis=0, keepdims=True)
    cnt_ref[0] = jnp.broadcast_to(cnt, (SUBLANES, LANES)).astype(jnp.int32)
    pieces = jnp.floor((cnt + (PIECE - 1.0)) * (1.0 / PIECE))
    ei = lax.broadcasted_iota(jnp.int32, (LANES, LANES), 0)
    ej = lax.broadcasted_iota(jnp.int32, (LANES, LANES), 1)
    group_row = PIECE * jnp.dot(jnp.broadcast_to(pieces, (SUBLANES, LANES)).astype(BF16), (ei < ej).astype(BF16),
                                preferred_element_type=F32)[0:1]
    pos = before + group_row
    wt_o = jnp.zeros_like(logits)
    lrow_o = jnp.full(logits.shape, -1.0, F32)
    for kq in range(TOP_K):
        lrow_k = jnp.sum(jnp.where(lane == idxs[kq], pos, 0.0), -1, keepdims=True)
        wt_o = jnp.where(lane == kq, es[kq] / tot, wt_o)
        lrow_o = jnp.where(lane == kq, lrow_k, lrow_o)
    wt_ref[...] = wt_o
    lrow_ref[...] = lrow_o.astype(jnp.int32)
    lrow_t = jnp.transpose(lrow_o).astype(jnp.int32)
    for rc in range(TILE_ROWS // DISPATCH_ROWS):
        row = lax.broadcasted_iota(jnp.int32, (DISPATCH_ROWS, tm), 0) + rc * DISPATCH_ROWS
        hit = row == lrow_t[0:1, :]
        for kq in range(1, TOP_K):
            hit = jnp.logical_or(hit, row == lrow_t[kq:kq + 1, :])
        onehot = jnp.where(hit, 1.0, 0.0).astype(BF16)
        xbt_ref[pl.ds(rc * DISPATCH_ROWS, DISPATCH_ROWS), :] = jnp.dot(
            onehot, h2_hi, preferred_element_type=F32).astype(xbt_ref.dtype)


def out_ln_router(merged, w_out_b, x2d, gate1, ln_g, ln_b, scale2, shift2, router_w, router_b, S):
    T, D = x2d.shape
    tm = ROUTE_TM
    nt = T // tm
    spb = S // tm
    rw = jnp.zeros((D, LANES), F32).at[:, :N_EXPERTS].set(router_w)
    rw_hi, rw_lo = _split_bf16(rw)
    rb = jnp.full((1, LANES), NEG_BIG, F32).at[0, :N_EXPERTS].set(router_b)
    rowblk = lambda w: pl.BlockSpec((tm, w), lambda i: (i, 0))
    full = lambda r, c: pl.BlockSpec((r, c), lambda i: (0, 0))
    perb = pl.BlockSpec((1, 1, D), lambda i: (i // spb, 0, 0))
    tile_spec = pl.BlockSpec((1, SUBLANES, LANES), lambda i: (i, 0, 0))
    tile_shape = jax.ShapeDtypeStruct((nt, SUBLANES, LANES), jnp.int32)
    return pl.pallas_call(
        _out_ln_router_kernel,
        out_shape=(jax.ShapeDtypeStruct((T, D), F32), jax.ShapeDtypeStruct((nt * TILE_ROWS, D), BF16),
                   jax.ShapeDtypeStruct((T, LANES), F32), jax.ShapeDtypeStruct((T, LANES), jnp.int32), tile_shape),
        grid=(nt,),
        in_specs=[rowblk(D), full(D, D), rowblk(D), perb, full(1, D), full(1, D), perb, perb,
                  full(D, LANES), full(D, LANES), full(1, LANES)],
        out_specs=(rowblk(D), pl.BlockSpec((TILE_ROWS, D), lambda i: (i, 0)), rowblk(LANES), rowblk(LANES),
                   tile_spec),
        compiler_params=_cparams(("parallel",)),
        name="out_proj_ln_router",
    )(merged, w_out_b, x2d, gate1, ln_g.reshape(1, D), ln_b.reshape(1, D), scale2, shift2, rw_hi, rw_lo, rb)


W_CHUNK_ROWS = 512
W_STAGES = 3
W_DMA_PRIORITY = 1
W_NONE, W_FIRST_HALF, W_SECOND_HALF, W_ALL = 0, 1, 2, 3
GU_CHUNKS = D_MODEL // W_CHUNK_ROWS
DN_CHUNKS = EXPERT_FF // W_CHUNK_ROWS


def _expert_kernel(be_ref, nu_ref, src_ref, dst_ref, par_ref, nxt_ref, mode_ref, x_hbm, wgu_hbm, bgu_ref, wd_hbm, bd_ref,
                   y_hbm, xs_ref, ys_ref, wgu_b, wd_b, stg_ref, isem, osem, wsem, *, layer):
    i = pl.program_id(0)
    nu = nu_ref[0]
    slot = i % 2
    n_chunks = GU_CHUNKS + DN_CHUNKS

    def in_copy(blk, k, s):
        src = pl.multiple_of(src_ref[blk * BLOCK_PIECES + k], PIECE)
        return pltpu.make_async_copy(x_hbm.at[pl.ds(src, PIECE), :], xs_ref.at[s, pl.ds(k * PIECE, PIECE), :],
                                     isem.at[s])

    def out_copy(blk, k, s):
        dst = pl.multiple_of(dst_ref[blk * BLOCK_PIECES + k], PIECE)
        return pltpu.make_async_copy(ys_ref.at[s, pl.ds(k * PIECE, PIECE), :], y_hbm.at[pl.ds(dst, PIECE), :],
                                     osem.at[s])

    def all_pieces(fn):
        for k in range(BLOCK_PIECES):
            fn(k)

    def w_copy(e, c):
        s = c % W_STAGES
        if c < GU_CHUNKS:
            src = wgu_hbm.at[layer, e, pl.ds(c * W_CHUNK_ROWS, W_CHUNK_ROWS), :]
        else:
            src = wd_hbm.at[layer, e, pl.ds((c - GU_CHUNKS) * W_CHUNK_ROWS, W_CHUNK_ROWS), :]
        return pltpu.make_async_copy(src, stg_ref.at[s], wsem.at[s])

    def w_convert(c, p):
        w = stg_ref[c % W_STAGES].astype(BF16)
        if c < GU_CHUNKS:
            wgu_b[p, pl.ds(c * W_CHUNK_ROWS, W_CHUNK_ROWS), :] = w
        else:
            wd_b[p, pl.ds((c - GU_CHUNKS) * W_CHUNK_ROWS, W_CHUNK_ROWS), :] = w

    def w_finish(e, p, chunks, refill):
        for c in chunks:
            w_copy(e, c).wait()
            w_convert(c, p)
            if refill and c + W_STAGES < n_chunks:
                w_copy(e, c + W_STAGES).start(priority=W_DMA_PRIORITY)

    @pl.when(i == 0)
    def _():
        all_pieces(lambda k: in_copy(0, k, 0).start())
        e0 = be_ref[0]
        for c in range(W_STAGES):
            w_copy(e0, c).start(priority=W_DMA_PRIORITY)
        w_finish(e0, par_ref[0], range(n_chunks), True)

    @pl.when(i + 1 < nu)
    def _():
        all_pieces(lambda k: in_copy(i + 1, k, 1 - slot).start())

    @pl.when(i < nu)
    def _():
        p = par_ref[i]
        e_next = nxt_ref[i]
        mode = mode_ref[i]
        first_half = range(W_STAGES)
        second_half = range(W_STAGES, n_chunks)

        @pl.when(jnp.logical_or(mode == W_FIRST_HALF, mode == W_ALL))
        def _():
            for c in first_half:
                w_copy(e_next, c).start(priority=W_DMA_PRIORITY)

        @pl.when(mode == W_SECOND_HALF)
        def _():
            for c in second_half:
                w_copy(e_next, c).start(priority=W_DMA_PRIORITY)

        all_pieces(lambda k: in_copy(i, k, slot).wait())

        @pl.when(i >= 2)
        def _():
            all_pieces(lambda k: out_copy(i - 2, k, slot).wait())

        gu = jnp.dot(xs_ref[slot], wgu_b[p], preferred_element_type=F32) + bgu_ref[0]
        glu = jnp.minimum(gu[:, :EXPERT_FF], SWIGLU_LIMIT)
        lin = jnp.clip(gu[:, EXPERT_FF:], -SWIGLU_LIMIT, SWIGLU_LIMIT)
        act = glu * jax.nn.sigmoid(SWIGLU_ALPHA * glu) * (lin + 1.0)
        y = jnp.dot(act.astype(BF16), wd_b[p], preferred_element_type=F32) + bd_ref[0]
        ys_ref[slot] = y.astype(ys_ref.dtype)
        all_pieces(lambda k: out_copy(i, k, slot).start())

        @pl.when(mode == W_ALL)
        def _():
            w_finish(e_next, 1 - p, first_half, True)

        @pl.when(mode == W_FIRST_HALF)
        def _():
            w_finish(e_next, 1 - p, first_half, False)

        @pl.when(jnp.logical_or(mode == W_SECOND_HALF, mode == W_ALL))
        def _():
            w_finish(e_next, 1 - p, second_half, False)

    @pl.when(i == nu - 1)
    def _():
        @pl.when(i >= 1)
        def _():
            all_pieces(lambda k: out_copy(i - 1, k, 1 - slot).wait())
        all_pieces(lambda k: out_copy(i, k, slot).wait())


def expert_ffn(block_e, n_used, piece_src, piece_dst, w_parity, next_e, w_mode, xbt, wgu, bgu, wd, bd, layer):
    rows, D = xbt.shape
    _, E, _, F2 = wgu.shape
    nblk = block_e.shape[0]
    bias = lambda w: pl.BlockSpec((1, 1, w), lambda i, be, *_: (be[i], 0, 0))
    grid_spec = pltpu.PrefetchScalarGridSpec(
        num_scalar_prefetch=7,
        grid=(nblk,),
        in_specs=[pl.BlockSpec(memory_space=pl.ANY), pl.BlockSpec(memory_space=pl.ANY), bias(F2),
                  pl.BlockSpec(memory_space=pl.ANY), bias(D)],
        out_specs=pl.BlockSpec(memory_space=pl.ANY),
        scratch_shapes=[pltpu.VMEM((2, MOE_BM, D), BF16), pltpu.VMEM((2, MOE_BM, D), BF16),
                        pltpu.VMEM((2, D, F2), BF16), pltpu.VMEM((2, F2 // 2, D), BF16),
                        pltpu.VMEM((W_STAGES, W_CHUNK_ROWS, D), F32),
                        pltpu.SemaphoreType.DMA((2,)), pltpu.SemaphoreType.DMA((2,)),
                        pltpu.SemaphoreType.DMA((W_STAGES,))],
    )
    return pl.pallas_call(
        functools.partial(_expert_kernel, layer=layer),
        out_shape=jax.ShapeDtypeStruct((rows, D), BF16),
        grid_spec=grid_spec,
        input_output_aliases={7: 0},
        compiler_params=_cparams(("arbitrary",)),
        name="expert_ffn",
    )(block_e, n_used, piece_src, piece_dst, w_parity, next_e, w_mode, xbt, wgu, bgu.reshape(E, 1, F2), wd,
      bd.reshape(E, 1, D))


COMBINE_COLS = 512


def _combine_kernel(meta_ref, wt_ref, x1_ref, gate_ref, g_ref, b_ref, sc_ref, sh_ref, y_ref,
                    x2_ref, hb_ref, acc_ref):
    meta = meta_ref[...]
    wts = wt_ref[...]
    CW = COMBINE_COLS
    col0 = lax.broadcasted_iota(jnp.int32, (meta.shape[0], CW), 1)
    for cc in range(TILE_ROWS // CW):
        hi = jnp.zeros(col0.shape, F32)
        lo = jnp.zeros(col0.shape, F32)
        for kq in range(TOP_K):
            hit = col0 == meta[:, kq:kq + 1] - cc * CW
            w = wts[:, kq:kq + 1]
            w_hi = w.astype(BF16).astype(F32)
            hi = jnp.where(hit, w_hi, hi)
            lo = jnp.where(hit, w - w_hi, lo)
        rows = y_ref[pl.ds(cc * CW, CW), :]
        part = (jnp.dot(hi.astype(BF16), rows, preferred_element_type=F32)
                + jnp.dot(lo.astype(BF16), rows, preferred_element_type=F32))
        if cc == 0:
            acc_ref[...] = part
        else:
            acc_ref[...] += part

    x2 = _layer_norm(DEEPNORM_ALPHA * x1_ref[...] + gate_ref[0] * acc_ref[...], g_ref[...], b_ref[...])
    x2_ref[...] = x2
    hb_ref[...] = (x2 * (1.0 + sc_ref[0]) + sh_ref[0]).astype(hb_ref.dtype)


def moe_combine_ln(ybt, meta, wt, x1, gate2, ln_g, ln_b, scale_next, shift_next, S):
    T, D = x1.shape
    tm = ROUTE_TM
    spb = S // tm
    rowblk = lambda w: pl.BlockSpec((tm, w), lambda i: (i, 0))
    full = pl.BlockSpec((1, D), lambda i: (0, 0))
    perb = pl.BlockSpec((1, 1, D), lambda i: (i // spb, 0, 0))
    return pl.pallas_call(
        _combine_kernel,
        out_shape=(jax.ShapeDtypeStruct((T, D), F32), jax.ShapeDtypeStruct((T, D), BF16)),
        grid=(T // tm,),
        in_specs=[rowblk(LANES), rowblk(LANES), rowblk(D), perb, full, full, perb, perb,
                  pl.BlockSpec((TILE_ROWS, D), lambda i: (i, 0))],
        out_specs=(rowblk(D), rowblk(D)),
        scratch_shapes=[pltpu.VMEM((tm, D), F32)],
        compiler_params=_cparams(("parallel",)),
        name="moe_combine_ln",
    )(meta, wt, x1, gate2, ln_g.reshape(1, D), ln_b.reshape(1, D), scale_next, shift_next, ybt)


def routed_ffn_ln(xbt, lrow, wt, tile_cnt, wgu, bgu, wd, bd, layer, x1, gate2, ln_g, ln_b,
                  scale_next, shift_next, S):
    T, D = x1.shape
    E = N_EXPERTS
    nt = T // ROUTE_TM
    i32 = jnp.int32
    cnt = tile_cnt[:, 0, :E]
    npc = (cnt + PIECE - 1) // PIECE
    seg_off = (jnp.cumsum(npc, axis=1) - npc) * PIECE
    per_e = jnp.sum(npc, axis=0)
    per_e_pad = ((per_e + BLOCK_PIECES - 1) // BLOCK_PIECES) * BLOCK_PIECES
    e_end = jnp.cumsum(per_e_pad)
    e_start = e_end - per_e_pad
    t_end = jnp.cumsum(npc, axis=0).T
    t_start = t_end - npc.T
    max_pieces = nt * (TILE_USED_ROWS // PIECE) + E * (BLOCK_PIECES - 1)
    nblk = -(-max_pieces // BLOCK_PIECES)
    q = jnp.arange(nblk * BLOCK_PIECES, dtype=i32)
    e_q = jnp.minimum(jnp.sum((e_end[None, :] <= q[:, None]).astype(i32), axis=1), E - 1)
    sel_e = e_q[:, None] == jnp.arange(E, dtype=i32)[None, :]
    of_expert = lambda table: jnp.sum(jnp.where(sel_e, table[None, :], 0), axis=1)
    w_q = q - of_expert(e_start)
    real = w_q < of_expert(per_e)
    rows_of = lambda table: jnp.sum(jnp.where(sel_e[:, :, None], table[None], 0), axis=1)
    t_q = jnp.minimum(jnp.sum((rows_of(t_end) <= w_q[:, None]).astype(i32), axis=1), nt - 1)
    sel_t = t_q[:, None] == jnp.arange(nt, dtype=i32)[None, :]
    of_tile = lambda table: jnp.sum(jnp.where(sel_t, rows_of(table), 0), axis=1)
    k_q = w_q - of_tile(t_start)
    row_q = t_q * TILE_ROWS + of_tile(seg_off.T) + k_q * PIECE
    tail_pieces = (TILE_ROWS - TILE_USED_ROWS) // PIECE
    assert 2 * BLOCK_PIECES <= nt * tail_pieces, "not enough tile tail rows for the padding pieces"
    scratch_id = ((q // BLOCK_PIECES) % 2) * BLOCK_PIECES + q % BLOCK_PIECES
    scratch_q = (scratch_id // tail_pieces) * TILE_ROWS + TILE_USED_ROWS + (scratch_id % tail_pieces) * PIECE
    piece_src = jnp.where(real, row_q, 0).astype(i32)
    piece_dst = jnp.where(real, row_q, scratch_q).astype(i32)
    block_e = e_q[::BLOCK_PIECES]
    n_used = (e_end[-1] // BLOCK_PIECES).astype(i32)
    blk = jnp.arange(nblk, dtype=i32)
    nxt = jnp.concatenate([block_e[1:], block_e[-1:]])
    switch_next = jnp.logical_and(nxt != block_e, blk + 1 < n_used)
    w_parity = (jnp.cumsum(switch_next.astype(i32)) - switch_next.astype(i32)) % 2
    prv = jnp.concatenate([block_e[:1] - 1, block_e[:-1]])
    only_block = jnp.logical_and(switch_next, prv != block_e)
    before_last = jnp.logical_and(jnp.concatenate([switch_next[1:], switch_next[-1:] & False]), nxt == block_e)
    w_mode = jnp.where(only_block, W_ALL, jnp.where(switch_next, W_SECOND_HALF,
                                                    jnp.where(before_last, W_FIRST_HALF, W_NONE))).astype(i32)
    nxt2 = jnp.concatenate([block_e[2:], block_e[-1:], block_e[-1:]])
    next_e = jnp.where(switch_next, nxt, jnp.where(before_last, nxt2, 0)).astype(i32)

    ybt = expert_ffn(block_e, n_used.reshape(1), piece_src, piece_dst, w_parity, next_e, w_mode, xbt, wgu, bgu, wd,
                     bd, layer)
    return moe_combine_ln(ybt, lrow, wt, x1, gate2, ln_g, ln_b, scale_next, shift_next, S)


def _mix_weights(w_in_l):
    def cols(rng, pad=0):
        part = w_in_l[:, rng[0]:rng[1]]
        if pad:
            part = jnp.pad(part, ((0, 0), (0, pad)))
        return part
    w_mix = jnp.concatenate([
        cols(SRC_POOL), cols(SRC_GLA_V), cols(SRC_GLA_G), cols(SRC_RWKV_RKV), cols(SRC_ATT),
        cols(SRC_GLA_Q), cols(SRC_GLA_K),
        cols(SRC_RWKV_LO, RWKV_LO_PAD - (SRC_RWKV_LO[1] - SRC_RWKV_LO[0])),
        cols(SRC_GLA_A, GLA_A_PAD - (SRC_GLA_A[1] - SRC_GLA_A[0]))], axis=1).astype(BF16)
    w_gates = w_in_l[:, SRC_GATES[0]:].astype(BF16)
    return w_mix, w_gates


def kernel(x, c, ada_w, ada_b, w_in, pool_w, pool_scale, gla_w_alpha, gla_b_alpha, gla_norm_g, rwkv_mu, rwkv_w0, rwkv_w2, rwkv_a0, rwkv_a2, rwkv_g2, rwkv_k_k, rwkv_k_a, rwkv_r_k, rwkv_ln_g, rwkv_ln_b, w_branch_a, w_branch_b, w_branch_c, w_branch_d, w_out, ln1_g, ln1_b, router_w, router_b, w_gate_up, b_gate_up, w_down, b_down, ln2_g, ln2_b):
    B, S, D = x.shape
    T = B * S
    mod = ada_modulation(c, ada_w, ada_b)
    mods = [[mod[l, :, None, i * D:(i + 1) * D] for i in range(6)] for l in range(DEPTH)]
    x2d = x.reshape(T, D)
    hb2d = modulate(x, mods[0][1], mods[0][0]).reshape(T, D)
    for l in range(DEPTH):
        shift1, scale1, gate1, shift2, scale2, gate2 = mods[l]
        shift_next, scale_next = (mods[l + 1][0], mods[l + 1][1]) if l + 1 < DEPTH else (shift1, scale1)
        w_mix, w_gates = _mix_weights(w_in[l])
        P = matmul(hb2d, w_mix, 1024, MIX_TN).reshape(B, S, MIX_WIDTH)
        y_a = pool_mixer(P, pool_w[l], pool_scale[l])
        y_b = dilated_attention(P)
        y_c = gla_mixer(P, gla_w_alpha[l], gla_b_alpha[l], gla_norm_g[l])
        r_, w_, k_, v_, kk_, b_, gate_, bonus_ = rwkv_prep(
            P, rwkv_mu[l], rwkv_w0[l], rwkv_w2[l], rwkv_a0[l], rwkv_a2[l], rwkv_g2[l],
            rwkv_k_k[l], rwkv_k_a[l], rwkv_r_k[l].reshape(-1))
        y_t = rwkv_chunked(r_, w_, k_, v_, kk_, b_)
        y_d = rwkv_post(y_t, gate_, bonus_, rwkv_ln_g[l], rwkv_ln_b[l])
        ys = [y.reshape(T, -1) for y in (y_a, y_b, y_c, y_d)]
        ws = [w[l].astype(BF16) for w in (w_branch_a, w_branch_b, w_branch_c, w_branch_d)]
        merged = branch_merge(hb2d, w_gates, ys, ws)
        x1, xbt, wt, lrow, tile_cnt = out_ln_router(
            merged, w_out[l].astype(BF16), x2d, gate1, ln1_g[l], ln1_b[l], scale2, shift2,
            router_w[l], router_b[l], S)
        x2d, hb2d = routed_ffn_ln(xbt, lrow, wt, tile_cnt, w_gate_up, b_gate_up[l], w_down, b_down[l], l,
                                  x1, gate2, ln2_g[l], ln2_b[l], scale_next, shift_next, S)
    return x2d.reshape(B, S, D)
```
